```python
import jax, jax.numpy as jnp
from jax import lax
import numpy as np

D_MODEL = 1024
BATCH = 4
SEQ = 4096
DEPTH = 1

N_META = 16
CHUNK = 64
PAD = CHUNK - N_META
MIX_WIDTH = D_MODEL
HGRN_WIDTH = MIX_WIDTH // 2
HGRN_HEAD_DIM = 128
HGRN_HEADS = HGRN_WIDTH // HGRN_HEAD_DIM
GLA_WIDTH = MIX_WIDTH - HGRN_WIDTH
GLA_HEADS = 4
GLA_V_DIM = GLA_WIDTH // GLA_HEADS
GLA_K_DIM = GLA_V_DIM // 2
GLA_GATE_RANK = 16
GLA_GATE_NORMALIZER = 16.0
FFN_HIDDEN = ((8 * D_MODEL + 3 * 256 - 1) // (3 * 256)) * 256
NORM_EPS = 1e-6

SPLIT_SIZES = (
    HGRN_WIDTH, HGRN_WIDTH, HGRN_WIDTH, HGRN_WIDTH,
    GLA_HEADS * GLA_K_DIM, GLA_HEADS * GLA_K_DIM,
    GLA_WIDTH, GLA_WIDTH,
    GLA_GATE_RANK,
)
SPLIT_POINTS = tuple(int(s) for s in np.cumsum(SPLIT_SIZES)[:-1])
IN_COLS = int(sum(SPLIT_SIZES))

kernel_name = "hymba_hgrn2_gla_hybrid_block"


def rmsnorm(x, g):
    xf = x.astype(jnp.float32)
    y = xf * lax.rsqrt(jnp.mean(xf * xf, axis=-1, keepdims=True) + NORM_EPS)
    return (y * g.astype(jnp.float32)).astype(x.dtype)


def chunked_gated_linear_recurrence(q, k, v, log_a):
    B, T, H, Dk = q.shape
    Dv = v.shape[-1]
    n = T // CHUNK

    def blk(t):
        return t.astype(jnp.float32).reshape(B, n, CHUNK, H, t.shape[-1]).transpose(0, 3, 1, 2, 4)

    q, k, v, g = blk(q), blk(k), blk(v), blk(log_a)
    b = jnp.cumsum(g, axis=3)
    b_last = b[:, :, :, -1:, :]
    q_d = q * jnp.exp(b)
    k_d = k * jnp.exp(-b)
    k_s = k * jnp.exp(b_last - b)

    causal = jnp.tril(jnp.ones((CHUNK, CHUNK), dtype=bool))
    scores = jnp.einsum('bhncd,bhnsd->bhncs', q_d, k_d)
    scores = jnp.where(causal, scores, 0.0)
    o_intra = jnp.einsum('bhncs,bhnse->bhnce', scores, v)

    d_state = jnp.einsum('bhnsd,bhnse->nbhde', k_s, v)
    chunk_decay = jnp.exp(b_last[:, :, :, 0, :]).transpose(2, 0, 1, 3)

    def step(S, inp):
        dec, ds = inp
        return dec[..., None] * S + ds, S

    _, S_start = lax.scan(step, jnp.zeros((B, H, Dk, Dv), jnp.float32), (chunk_decay, d_state))
    o_inter = jnp.einsum('bhncd,nbhde->bhnce', q_d, S_start)
    return (o_intra + o_inter).transpose(0, 2, 3, 1, 4).reshape(B, T, H, Dv)


def run_group(q, k, v, log_a):
    pad = lambda t: jnp.pad(t, ((0, 0), (PAD, 0), (0, 0), (0, 0)))
    o = chunked_gated_linear_recurrence(pad(q), pad(k), pad(v), pad(log_a))
    return o[:, PAD:]


def hybrid_mixer(u, w_in, w_gla_gate_up, b_gla_gate, lb, hgrn_norm_g, gla_norm_g, w_out):
    B, T, _ = u.shape
    proj = u @ w_in
    hq, hf, hi, hg, gq, gk, gv, gg, ga = jnp.split(proj, SPLIT_POINTS, axis=-1)

    f = lb + (1.0 - lb) * jax.nn.sigmoid(hf.astype(jnp.float32))
    hgrn_shape = (B, T, HGRN_HEADS, HGRN_HEAD_DIM)
    o_h = run_group(jax.nn.silu(hq).reshape(hgrn_shape),
                    (1.0 - f).reshape(hgrn_shape),
                    hi.reshape(hgrn_shape),
                    jnp.log(f).reshape(hgrn_shape))
    o_h = rmsnorm(o_h.astype(u.dtype), hgrn_norm_g) * jax.nn.silu(hg.reshape(hgrn_shape))

    gate_logits = (ga @ w_gla_gate_up + b_gla_gate).astype(jnp.float32)
    log_alpha = jax.nn.log_sigmoid(gate_logits) / GLA_GATE_NORMALIZER
    k_shape = (B, T, GLA_HEADS, GLA_K_DIM)
    v_shape = (B, T, GLA_HEADS, GLA_V_DIM)
    o_g = run_group(gq.reshape(k_shape) * (GLA_K_DIM ** -0.5),
                    gk.reshape(k_shape),
                    gv.reshape(v_shape),
                    log_alpha.reshape(k_shape))
    o_g = rmsnorm(o_g.astype(u.dtype), gla_norm_g) * jax.nn.silu(gg.reshape(v_shape))

    o = jnp.concatenate([o_h.reshape(B, T, HGRN_WIDTH), o_g.reshape(B, T, GLA_WIDTH)], axis=-1)
    return o @ w_out


def swiglu(u, w_ffn_in, w_ffn_out):
    gate, up = jnp.split(u @ w_ffn_in, 2, axis=-1)
    return (jax.nn.silu(gate) * up) @ w_ffn_out


def setup_inputs(seed: int = 0) -> dict:
    key = jax.random.key(seed)
    ks = jax.random.split(key, 16)
    f32 = jnp.float32
    nrm = lambda k, shape, scale: jax.random.normal(k, shape, f32) * scale
    return {
        "x": nrm(ks[0], (BATCH, SEQ, D_MODEL), 1.0),
        "meta_tokens": nrm(ks[1], (N_META, D_MODEL), 1.0),
        "lb_table": nrm(ks[2], (DEPTH + 1, HGRN_WIDTH), 0.1),
        "norm_mix_g": 1.0 + nrm(ks[3], (DEPTH, D_MODEL), 0.01),
        "w_in": nrm(ks[4], (DEPTH, D_MODEL, IN_COLS), D_MODEL ** -0.5),
        "w_gla_gate_up": nrm(ks[5], (DEPTH, GLA_GATE_RANK, GLA_HEADS * GLA_K_DIM), GLA_GATE_RANK ** -0.5),
        "b_gla_gate": nrm(ks[6], (DEPTH, GLA_HEADS * GLA_K_DIM), 0.01),
        "hgrn_norm_g": 1.0 + nrm(ks[7], (DEPTH, HGRN_HEAD_DIM), 0.01),
        "gla_norm_g": 1.0 + nrm(ks[8], (DEPTH, GLA_V_DIM), 0.01),
        "w_out": nrm(ks[9], (DEPTH, MIX_WIDTH, D_MODEL), MIX_WIDTH ** -0.5),
        "norm_ffn_g": 1.0 + nrm(ks[10], (DEPTH, D_MODEL), 0.01),
        "w_ffn_in": nrm(ks[11], (DEPTH, D_MODEL, 2 * FFN_HIDDEN), D_MODEL ** -0.5),
        "w_ffn_out": nrm(ks[12], (DEPTH, FFN_HIDDEN, D_MODEL), FFN_HIDDEN ** -0.5),
        "norm_final_g": 1.0 + nrm(ks[13], (D_MODEL,), 0.01),
    }


def reference(x, meta_tokens, lb_table, norm_mix_g, w_in, w_gla_gate_up, b_gla_gate,
              hgrn_norm_g, gla_norm_g, w_out, norm_ffn_g, w_ffn_in, w_ffn_out, norm_final_g):
    B = x.shape[0]
    meta = jnp.broadcast_to(meta_tokens.astype(x.dtype)[None], (B, N_META, D_MODEL))
    h = jnp.concatenate([meta, x], axis=1)

    lb_all = jnp.cumsum(jax.nn.softmax(lb_table.astype(jnp.float32), axis=0), axis=0)

    for l in range(DEPTH):
        u = rmsnorm(h, norm_mix_g[l])
        h = h + hybrid_mixer(u, w_in[l], w_gla_gate_up[l], b_gla_gate[l], lb_all[l],
                             hgrn_norm_g[l], gla_norm_g[l], w_out[l])
        u = rmsnorm(h, norm_ffn_g[l])
        h = h + swiglu(u, w_ffn_in[l], w_ffn_out[l])

    h = rmsnorm(h, norm_final_g)
    return h[:, N_META:]
```

```python
import functools

import jax
import jax.numpy as jnp
from jax import lax
from jax.experimental import pallas as pl
from jax.experimental.pallas import tpu as pltpu

F32 = jnp.float32
BF16 = jnp.bfloat16

D_MODEL = 1024
N_META = 16
CHUNK = 64
HEADS = 4
HEAD_V = 128
HGRN_WIDTH = HEADS * HEAD_V
GLA_K = 64
GLA_QK_WIDTH = HEADS * GLA_K
GLA_WIDTH = HEADS * HEAD_V
GATE_RANK = 16
GATE_NORMALIZER = 16.0
FFN_HIDDEN = 2816
NORM_EPS = 1e-6
LANES = 128

C_HQ, C_HF, C_HI, C_HG = 0, 512, 1024, 1536
C_GQ, C_GK, C_GV, C_GG = 2048, 2304, 2560, 3072
MAIN_COLS = 3584
IN_COLS = MAIN_COLS + GATE_RANK

V7X_VMEM_LIMIT = 56 * 1024 * 1024

ROWS_INPROJ = 512
ROWS_MIXER = 512
ROWS_FFN = 512


def _rmsnorm(x, g):
    return x * lax.rsqrt(jnp.mean(x * x, axis=-1, keepdims=True) + NORM_EPS) * g


def _dot(a, b):
    return jnp.dot(a.astype(BF16), b.astype(BF16), preferred_element_type=F32)


def _dot_nt(a, b):
    return lax.dot_general(a.astype(BF16), b.astype(BF16), (((1,), (1,)), ((), ())),
                           preferred_element_type=F32)


def _dot_tn(a, b):
    return lax.dot_general(a.astype(BF16), b.astype(BF16), (((0,), (0,)), ((), ())),
                           preferred_element_type=F32)


def _silu(x):
    return x * jax.nn.sigmoid(x)


def _log_sigmoid(x):
    return jnp.minimum(x, 0.0) - jnp.log1p(jnp.exp(-jnp.abs(x)))


def _tril_ones(n):
    r = lax.broadcasted_iota(jnp.int32, (n, n), 0)
    c = lax.broadcasted_iota(jnp.int32, (n, n), 1)
    return r >= c


def _cumsum_rows(g, tri_bf16):
    g_hi = g.astype(BF16)
    g_lo = (g - g_hi.astype(F32)).astype(BF16)
    return (jnp.dot(tri_bf16, g_hi, preferred_element_type=F32)
            + jnp.dot(tri_bf16, g_lo, preferred_element_type=F32))


def _inproj_kernel(x_ref, g_ref, w_ref, wup_ref, bup_ref, proj_ref, logit_ref):
    u = _rmsnorm(x_ref[...], g_ref[...])
    p = _dot(u, w_ref[...])
    proj_ref[...] = p[:, :MAIN_COLS]
    logit_ref[...] = _dot(p[:, MAIN_COLS:], wup_ref[...]) + bup_ref[...]


def _inproj(x2d, g, w_in, w_up, b_up, rows):
    n = x2d.shape[0]
    const = lambda i: (0, 0)
    return pl.pallas_call(
        _inproj_kernel,
        grid=(n // rows,),
        in_specs=[
            pl.BlockSpec((rows, D_MODEL), lambda i: (i, 0)),
            pl.BlockSpec((1, D_MODEL), const),
            pl.BlockSpec((D_MODEL, IN_COLS), const, pipeline_mode=pl.Buffered(1)),
            pl.BlockSpec((GATE_RANK, GLA_QK_WIDTH), const),
            pl.BlockSpec((1, GLA_QK_WIDTH), const),
        ],
        out_specs=[
            pl.BlockSpec((rows, MAIN_COLS), lambda i: (i, 0)),
            pl.BlockSpec((rows, GLA_QK_WIDTH), lambda i: (i, 0)),
        ],
        out_shape=[
            jax.ShapeDtypeStruct((n, MAIN_COLS), F32),
            jax.ShapeDtypeStruct((n, GLA_QK_WIDTH), F32),
        ],
        compiler_params=pltpu.CompilerParams(
            dimension_semantics=("arbitrary",), vmem_limit_bytes=V7X_VMEM_LIMIT),
        name="inproj",
    )(x2d, g, w_in, w_up, b_up)


def _decay_terms(g, tri):
    b = _cumsum_rows(g, tri)
    b_last = b[g.shape[0] - 1:, :]
    return b, b_last


def _group_state_only(k, v, g, st_ref, tri, gla):
    b, b_last = _decay_terms(g, tri)
    k_s = k * jnp.exp(b_last - b)
    for h in range(HEADS):
        k_h = _head_qk(k_s, h, gla)
        st_ref[h] = _dot_tn(v[:, h * HEAD_V:(h + 1) * HEAD_V], k_h)


def _head_qk(a, h, gla):
    if not gla:
        return a[:, h * LANES:(h + 1) * LANES]
    slab = a[:, (h // 2) * LANES:(h // 2 + 1) * LANES]
    lane = lax.broadcasted_iota(jnp.int32, slab.shape, 1)
    own = (lane >= GLA_K) == bool(h % 2)
    return jnp.where(own, slab, 0.0)


def _group_chunk(q, k, v, g, gate, norm_g, st_ref, tri, causal, gla, o_ref, rows, col0):
    b, b_last = _decay_terms(g, tri)
    q_d = q * jnp.exp(b)
    k_d = k * jnp.exp(-b)
    k_s = k * jnp.exp(b_last - b)
    dec = jnp.exp(b_last)
    for h in range(HEADS):
        q_h = _head_qk(q_d, h, gla)
        v_h = v[:, h * HEAD_V:(h + 1) * HEAD_V]
        if gla:
            k_h = k_d[:, (h // 2) * LANES:(h // 2 + 1) * LANES]
            dec_h = dec[:, (h // 2) * LANES:(h // 2 + 1) * LANES]
        else:
            k_h = k_d[:, h * LANES:(h + 1) * LANES]
            dec_h = dec[:, h * LANES:(h + 1) * LANES]
        scores = jnp.where(causal, _dot_nt(q_h, k_h), 0.0)
        st = st_ref[h]
        o = _dot(scores, v_h) + _dot_nt(q_h, st)
        st_ref[h] = st * dec_h + _dot_tn(v_h, _head_qk(k_s, h, gla))
        o = _rmsnorm(o, norm_g) * _silu(gate[:, h * HEAD_V:(h + 1) * HEAD_V])
        o_ref[rows, col0 + h * HEAD_V:col0 + (h + 1) * HEAD_V] = o.astype(o_ref.dtype)


def _hgrn_gates(hq, hf, lb):
    f = lb + (1.0 - lb) * jax.nn.sigmoid(hf)
    return _silu(hq), 1.0 - f, jnp.log(f)


def _mixer_kernel(proj_ref, logit_ref, pmeta_ref, lmeta_ref, lbt_ref, hnorm_ref, gnorm_ref,
                  o_ref, sth_ref, stg_ref):
    t0 = lbt_ref[0:1, :]
    t1 = lbt_ref[1:2, :]
    m = jnp.maximum(t0, t1)
    e0 = jnp.exp(t0 - m)
    lb = e0 / (e0 + jnp.exp(t1 - m))
    q_scale = GLA_K ** -0.5

    @pl.when(pl.program_id(1) == 0)
    def _():
        tri = _tril_ones(N_META).astype(BF16)
        _, k, g = _hgrn_gates(pmeta_ref[:, C_HQ:C_HQ + HGRN_WIDTH],
                              pmeta_ref[:, C_HF:C_HF + HGRN_WIDTH], lb)
        _group_state_only(k, pmeta_ref[:, C_HI:C_HI + HGRN_WIDTH], g, sth_ref, tri, False)
        g = _log_sigmoid(lmeta_ref[...]) / GATE_NORMALIZER
        _group_state_only(pmeta_ref[:, C_GK:C_GK + GLA_QK_WIDTH],
                          pmeta_ref[:, C_GV:C_GV + GLA_WIDTH], g, stg_ref, tri, True)

    causal = _tril_ones(CHUNK)
    tri = causal.astype(BF16)
    hnorm = hnorm_ref[...]
    gnorm = gnorm_ref[...]

    def chunk_body(c, carry):
        rows = pl.ds(pl.multiple_of(c * CHUNK, CHUNK), CHUNK)
        q, k, g = _hgrn_gates(proj_ref[rows, C_HQ:C_HQ + HGRN_WIDTH],
                              proj_ref[rows, C_HF:C_HF + HGRN_WIDTH], lb)
        _group_chunk(q, k, proj_ref[rows, C_HI:C_HI + HGRN_WIDTH], g,
                     proj_ref[rows, C_HG:C_HG + HGRN_WIDTH], hnorm, sth_ref, tri, causal,
                     False, o_ref, rows, 0)
        g = _log_sigmoid(logit_ref[rows, :]) / GATE_NORMALIZER
        _group_chunk(proj_ref[rows, C_GQ:C_GQ + GLA_QK_WIDTH] * q_scale,
                     proj_ref[rows, C_GK:C_GK + GLA_QK_WIDTH],
                     proj_ref[rows, C_GV:C_GV + GLA_WIDTH], g,
                     proj_ref[rows, C_GG:C_GG + GLA_WIDTH], gnorm, stg_ref, tri, causal,
                     True, o_ref, rows, HGRN_WIDTH)
        return carry

    lax.fori_loop(0, proj_ref.shape[0] // CHUNK, chunk_body, 0)


def _mixer(proj, logits, proj_meta, logits_meta, lb_table, hnorm, gnorm, batch, seq, rows):
    steps = seq // rows
    const = lambda b, t: (0, 0)
    tile = lambda b, t: (b * steps + t, 0)
    return pl.pallas_call(
        _mixer_kernel,
        grid=(batch, steps),
        in_specs=[
            pl.BlockSpec((rows, MAIN_COLS), tile),
            pl.BlockSpec((rows, GLA_QK_WIDTH), tile),
            pl.BlockSpec((N_META, MAIN_COLS), const),
            pl.BlockSpec((N_META, GLA_QK_WIDTH), const),
            pl.BlockSpec(lb_table.shape, const),
            pl.BlockSpec((1, HEAD_V), const),
            pl.BlockSpec((1, HEAD_V), const),
        ],
        out_specs=pl.BlockSpec((rows, HGRN_WIDTH + GLA_WIDTH), tile),
        out_shape=jax.ShapeDtypeStruct((batch * seq, HGRN_WIDTH + GLA_WIDTH), BF16),
        scratch_shapes=[pltpu.VMEM((HEADS, HEAD_V, LANES), F32),
                        pltpu.VMEM((HEADS, HEAD_V, LANES), F32)],
        compiler_params=pltpu.CompilerParams(
            dimension_semantics=("arbitrary", "arbitrary"), vmem_limit_bytes=V7X_VMEM_LIMIT),
        name="mixer",
    )(proj, logits, proj_meta, logits_meta, lb_table, hnorm, gnorm)


def _ffn_kernel(x_ref, o_ref, wout_ref, g2_ref, w1_ref, w2_ref, g3_ref, y_ref):
    h = x_ref[...] + jnp.dot(o_ref[...], wout_ref[...], preferred_element_type=F32)
    u = _rmsnorm(h, g2_ref[...]).astype(BF16)
    gate = jnp.dot(u, w1_ref[:, :FFN_HIDDEN], preferred_element_type=F32)
    up = jnp.dot(u, w1_ref[:, FFN_HIDDEN:], preferred_element_type=F32)
    h = h + _dot(_silu(gate) * up, w2_ref[...])
    y_ref[...] = _rmsnorm(h, g3_ref[...])


def _ffn(x2d, o, w_out, g2, w1, w2, g3, rows):
    n = x2d.shape[0]
    const = lambda i: (0, 0)
    tile = lambda i: (i, 0)
    resident = functools.partial(pl.BlockSpec, index_map=const, pipeline_mode=pl.Buffered(1))
    return pl.pallas_call(
        _ffn_kernel,
        grid=(n // rows,),
        in_specs=[
            pl.BlockSpec((rows, D_MODEL), tile),
            pl.BlockSpec((rows, D_MODEL), tile),
            resident((D_MODEL, D_MODEL)),
            pl.BlockSpec((1, D_MODEL), const),
            resident((D_MODEL, 2 * FFN_HIDDEN)),
            resident((FFN_HIDDEN, D_MODEL)),
            pl.BlockSpec((1, D_MODEL), const),
        ],
        out_specs=pl.BlockSpec((rows, D_MODEL), tile),
        out_shape=jax.ShapeDtypeStruct((n, D_MODEL), F32),
        compiler_params=pltpu.CompilerParams(
            dimension_semantics=("arbitrary",), vmem_limit_bytes=V7X_VMEM_LIMIT),
        name="outproj_ffn",
    )(x2d, o, w_out, g2, w1, w2, g3)


def kernel(x, meta_tokens, lb_table, norm_mix_g, w_in, w_gla_gate_up, b_gla_gate, hgrn_norm_g,
           gla_norm_g, w_out, norm_ffn_g, w_ffn_in, w_ffn_out, norm_final_g):
    batch, seq, _ = x.shape
    assert w_in.shape[0] == 1, "single-layer block"
    x2d = x.reshape(batch * seq, D_MODEL)
    g_mix = norm_mix_g[0][None, :]
    w_in_b = w_in[0].astype(BF16)
    w_up_b = w_gla_gate_up[0].astype(BF16)
    b_up = b_gla_gate[0][None, :]

    proj, logits = _inproj(x2d, g_mix, w_in_b, w_up_b, b_up, ROWS_INPROJ)
    proj_meta, logits_meta = _inproj(meta_tokens, g_mix, w_in_b, w_up_b, b_up, N_META)
    o = _mixer(proj, logits, proj_meta, logits_meta, lb_table, hgrn_norm_g[0][None, :],
               gla_norm_g[0][None, :], batch, seq, ROWS_MIXER)
    y = _ffn(x2d, o, w_out[0].astype(BF16), norm_ffn_g[0][None, :], w_ffn_in[0].astype(BF16),
             w_ffn_out[0].astype(BF16), norm_final_g[None, :], ROWS_FFN)
    return y.reshape(batch, seq, D_MODEL)
```

```python
import functools

import jax
import jax.numpy as jnp
from jax import lax
from jax.experimental import pallas as pl
from jax.experimental.pallas import tpu as pltpu

F32 = jnp.float32
BF16 = jnp.bfloat16

D_MODEL = 1024
N_META = 16
CHUNK = 64
HEADS = 4
HEAD_V = 128
HGRN_WIDTH = HEADS * HEAD_V
GLA_K = 64
GLA_QK_WIDTH = HEADS * GLA_K
GLA_WIDTH = HEADS * HEAD_V
MIX_WIDTH = HGRN_WIDTH + GLA_WIDTH
GATE_RANK = 16
GATE_NORMALIZER = 16.0
FFN_HIDDEN = 2816
NORM_EPS = 1e-6
LANES = 128

C_HQ, C_HF, C_HI, C_HG = 0, 512, 1024, 1536
C_GQ, C_GK, C_GV, C_GG = 2048, 2304, 2560, 3072
C_GA = 3584
IN_COLS = C_GA + GATE_RANK

O_HQD, O_HKD, O_HKS, O_HV = 0, 512, 1024, 1536
O_GQD, O_GKD, O_GKS, O_GV = 2048, 2560, 2816, 3328
OPS_COLS = O_GV + GLA_WIDTH
DEC_COLS = HGRN_WIDTH + GLA_QK_WIDTH
N_STATES = 2 * HEADS

HEAD_TABLE = tuple(
    [(O_HQD + LANES * h, O_HKD + LANES * h, O_HKS + LANES * h, O_HV + LANES * h, LANES * h)
     for h in range(HEADS)]
    + [(O_GQD + LANES * h, O_GKD + LANES * (h // 2), O_GKS + LANES * h, O_GV + LANES * h,
        HGRN_WIDTH + LANES * (h // 2)) for h in range(HEADS)])

V7X_VMEM_LIMIT = 56 * 1024 * 1024

ROWS_INPROJ = 512
ROWS_MIXER = 512
ROWS_FFN = 512


def _rmsnorm(x, g):
    return x * lax.rsqrt(jnp.mean(x * x, axis=-1, keepdims=True) + NORM_EPS) * g


def _dot(a, b):
    return jnp.dot(a.astype(BF16), b.astype(BF16), preferred_element_type=F32)


def _dot_nt(a, b):
    return lax.dot_general(a.astype(BF16), b.astype(BF16), (((1,), (1,)), ((), ())),
                           preferred_element_type=F32)


def _dot_tn(a, b):
    return lax.dot_general(a.astype(BF16), b.astype(BF16), (((0,), (0,)), ((), ())),
                           preferred_element_type=F32)


def _silu(x):
    return x * jax.nn.sigmoid(x)


def _log_sigmoid(x):
    return jnp.minimum(x, 0.0) - jnp.log1p(jnp.exp(-jnp.abs(x)))


def _tril_ones(n):
    r = lax.broadcasted_iota(jnp.int32, (n, n), 0)
    c = lax.broadcasted_iota(jnp.int32, (n, n), 1)
    return r >= c


def _cumsum_rows(g, tri_bf16):
    g_hi = g.astype(BF16)
    g_lo = (g - g_hi.astype(F32)).astype(BF16)
    return (jnp.dot(tri_bf16, g_hi, preferred_element_type=F32)
            + jnp.dot(tri_bf16, g_lo, preferred_element_type=F32))


def _own_half(x, h):
    lane = lax.broadcasted_iota(jnp.int32, x.shape, 1)
    return jnp.where((lane >= GLA_K) == bool(h % 2), x, 0.0)


def _inproj_kernel(x_ref, g_ref, w_ref, wup_ref, bup_ref, lbt_ref, ops_ref, gate_ref, dec_ref,
                   *, chunk):
    n_chunks = x_ref.shape[0] // chunk
    u = _rmsnorm(x_ref[...], g_ref[...]).astype(BF16)
    tri = _tril_ones(chunk).astype(BF16)

    def proj(col, width):
        return jnp.dot(u, w_ref[:, col:col + width], preferred_element_type=F32)

    def decay_factors(g, c):
        b = _cumsum_rows(g[c * chunk:(c + 1) * chunk], tri)
        b_last = b[chunk - 1:, :]
        return jnp.exp(b), jnp.exp(-b), jnp.exp(b_last - b), jnp.exp(b_last)

    t0 = lbt_ref[0:1, :]
    t1 = lbt_ref[1:2, :]
    m = jnp.maximum(t0, t1)
    e0 = jnp.exp(t0 - m)
    lb = e0 / (e0 + jnp.exp(t1 - m))

    f = lb + (1.0 - lb) * jax.nn.sigmoid(proj(C_HF, HGRN_WIDTH))
    g = jnp.log(f)
    k = 1.0 - f
    q = _silu(proj(C_HQ, HGRN_WIDTH))
    for c in range(n_chunks):
        rows = slice(c * chunk, (c + 1) * chunk)
        e_b, e_nb, e_end, dec = decay_factors(g, c)
        ops_ref[rows, O_HQD:O_HQD + HGRN_WIDTH] = (q[rows] * e_b).astype(BF16)
        ops_ref[rows, O_HKD:O_HKD + HGRN_WIDTH] = (k[rows] * e_nb).astype(BF16)
        ops_ref[rows, O_HKS:O_HKS + HGRN_WIDTH] = (k[rows] * e_end).astype(BF16)
        dec_ref[c:c + 1, 0:HGRN_WIDTH] = dec
    ops_ref[:, O_HV:O_HV + HGRN_WIDTH] = proj(C_HI, HGRN_WIDTH).astype(BF16)
    gate_ref[:, 0:HGRN_WIDTH] = _silu(proj(C_HG, HGRN_WIDTH))

    logits = _dot(proj(C_GA, GATE_RANK), wup_ref[...]) + bup_ref[...]
    g = _log_sigmoid(logits) / GATE_NORMALIZER
    q = proj(C_GQ, GLA_QK_WIDTH) * (GLA_K ** -0.5)
    k = proj(C_GK, GLA_QK_WIDTH)
    for c in range(n_chunks):
        rows = slice(c * chunk, (c + 1) * chunk)
        e_b, e_nb, e_end, dec = decay_factors(g, c)
        q_d = q[rows] * e_b
        k_s = k[rows] * e_end
        ops_ref[rows, O_GKD:O_GKD + GLA_QK_WIDTH] = (k[rows] * e_nb).astype(BF16)
        for h in range(HEADS):
            pair = slice(LANES * (h // 2), LANES * (h // 2 + 1))
            ops_ref[rows, O_GQD + LANES * h:O_GQD + LANES * (h + 1)] = (
                _own_half(q_d[:, pair], h).astype(BF16))
            ops_ref[rows, O_GKS + LANES * h:O_GKS + LANES * (h + 1)] = (
                _own_half(k_s[:, pair], h).astype(BF16))
        dec_ref[c:c + 1, HGRN_WIDTH:DEC_COLS] = dec
    ops_ref[:, O_GV:O_GV + GLA_WIDTH] = proj(C_GV, GLA_WIDTH).astype(BF16)
    gate_ref[:, HGRN_WIDTH:MIX_WIDTH] = _silu(proj(C_GG, GLA_WIDTH))


def _inproj(x2d, g, w_in, w_up, b_up, lb_table, rows, chunk):
    n = x2d.shape[0]
    chunks = rows // chunk
    const = lambda i: (0, 0)
    tile = lambda i: (i, 0)
    return pl.pallas_call(
        functools.partial(_inproj_kernel, chunk=chunk),
        grid=(n // rows,),
        in_specs=[
            pl.BlockSpec((rows, D_MODEL), tile),
            pl.BlockSpec((1, D_MODEL), const),
            pl.BlockSpec((D_MODEL, IN_COLS), const, pipeline_mode=pl.Buffered(1)),
            pl.BlockSpec((GATE_RANK, GLA_QK_WIDTH), const),
            pl.BlockSpec((1, GLA_QK_WIDTH), const),
            pl.BlockSpec(lb_table.shape, const),
        ],
        out_specs=[
            pl.BlockSpec((rows, OPS_COLS), tile),
            pl.BlockSpec((rows, MIX_WIDTH), tile),
            pl.BlockSpec((chunks, DEC_COLS), tile),
        ],
        out_shape=[
            jax.ShapeDtypeStruct((n, OPS_COLS), BF16),
            jax.ShapeDtypeStruct((n, MIX_WIDTH), F32),
            jax.ShapeDtypeStruct((n // chunk, DEC_COLS), F32),
        ],
        compiler_params=pltpu.CompilerParams(
            dimension_semantics=("arbitrary",), vmem_limit_bytes=V7X_VMEM_LIMIT),
        name="inproj",
    )(x2d, g, w_in, w_up, b_up, lb_table)


def _mixer_kernel(ops_ref, gate_ref, dec_ref, mops_ref, hnorm_ref, gnorm_ref, o_ref, st_ref):
    @pl.when(pl.program_id(1) == 0)
    def _():
        for i, (_, _, ks_col, v_col, _) in enumerate(HEAD_TABLE):
            st_ref[i] = _dot_tn(mops_ref[:, v_col:v_col + LANES],
                                mops_ref[:, ks_col:ks_col + LANES])

    causal = _tril_ones(CHUNK)
    norms = (hnorm_ref[...],) * HEADS + (gnorm_ref[...],) * HEADS

    def chunk_body(c, carry):
        rows = pl.ds(pl.multiple_of(c * CHUNK, CHUNK), CHUNK)
        dec = dec_ref[pl.ds(c, 1), :]
        for i, (q_col, k_col, ks_col, v_col, dec_col) in enumerate(HEAD_TABLE):
            q = ops_ref[rows, q_col:q_col + LANES]
            v = ops_ref[rows, v_col:v_col + LANES]
            scores = jnp.where(causal, _dot_nt(q, ops_ref[rows, k_col:k_col + LANES]), 0.0)
            st = st_ref[i]
            o = _dot(scores, v) + _dot_nt(q, st)
            st_ref[i] = (st * dec[:, dec_col:dec_col + LANES]
                         + _dot_tn(v, ops_ref[rows, ks_col:ks_col + LANES]))
            o = _rmsnorm(o, norms[i]) * gate_ref[rows, LANES * i:LANES * (i + 1)]
            o_ref[rows, LANES * i:LANES * (i + 1)] = o.astype(o_ref.dtype)
        return carry

    lax.fori_loop(0, ops_ref.shape[0] // CHUNK, chunk_body, 0)


def _mixer(ops, gates, dec, meta_ops, hnorm, gnorm, batch, seq, rows):
    steps = seq // rows
    chunks = rows // CHUNK
    const = lambda b, t: (0, 0)
    tile = lambda b, t: (b * steps + t, 0)
    return pl.pallas_call(
        _mixer_kernel,
        grid=(batch, steps),
        in_specs=[
            pl.BlockSpec((rows, OPS_COLS), tile),
            pl.BlockSpec((rows, MIX_WIDTH), tile),
            pl.BlockSpec((chunks, DEC_COLS), tile),
            pl.BlockSpec((N_META, OPS_COLS), const),
            pl.BlockSpec((1, HEAD_V), const),
            pl.BlockSpec((1, HEAD_V), const),
        ],
        out_specs=pl.BlockSpec((rows, MIX_WIDTH), tile),
        out_shape=jax.ShapeDtypeStruct((batch * seq, MIX_WIDTH), BF16),
        scratch_shapes=[pltpu.VMEM((N_STATES, HEAD_V, LANES), F32)],
        compiler_params=pltpu.CompilerParams(
            dimension_semantics=("arbitrary", "arbitrary"), vmem_limit_bytes=V7X_VMEM_LIMIT),
        name="mixer",
    )(ops, gates, dec, meta_ops, hnorm, gnorm)


def _ffn_kernel(x_ref, o_ref, wout_ref, g2_ref, w1_ref, w2_ref, g3_ref, y_ref):
    h = x_ref[...] + jnp.dot(o_ref[...], wout_ref[...], preferred_element_type=F32)
    u = _rmsnorm(h, g2_ref[...]).astype(BF16)
    gate = jnp.dot(u, w1_ref[:, :FFN_HIDDEN], preferred_element_type=F32)
    up = jnp.dot(u, w1_ref[:, FFN_HIDDEN:], preferred_element_type=F32)
    h = h + _dot(_silu(gate) * up, w2_ref[...])
    y_ref[...] = _rmsnorm(h, g3_ref[...])


def _ffn(x2d, o, w_out, g2, w1, w2, g3, rows):
    n = x2d.shape[0]
    const = lambda i: (0, 0)
    tile = lambda i: (i, 0)
    resident = functools.partial(pl.BlockSpec, index_map=const, pipeline_mode=pl.Buffered(1))
    return pl.pallas_call(
        _ffn_kernel,
        grid=(n // rows,),
        in_specs=[
            pl.BlockSpec((rows, D_MODEL), tile),
            pl.BlockSpec((rows, D_MODEL), tile),
            resident((D_MODEL, D_MODEL)),
            pl.BlockSpec((1, D_MODEL), const),
            resident((D_MODEL, 2 * FFN_HIDDEN)),
            resident((FFN_HIDDEN, D_MODEL)),
            pl.BlockSpec((1, D_MODEL), const),
        ],
        out_specs=pl.BlockSpec((rows, D_MODEL), tile),
        out_shape=jax.ShapeDtypeStruct((n, D_MODEL), F32),
        compiler_params=pltpu.CompilerParams(
            dimension_semantics=("arbitrary",), vmem_limit_bytes=V7X_VMEM_LIMIT),
        name="outproj_ffn",
    )(x2d, o, w_out, g2, w1, w2, g3)


def kernel(x, meta_tokens, lb_table, norm_mix_g, w_in, w_gla_gate_up, b_gla_gate, hgrn_norm_g,
           gla_norm_g, w_out, norm_ffn_g, w_ffn_in, w_ffn_out, norm_final_g):
    batch, seq, _ = x.shape
    assert w_in.shape[0] == 1 and lb_table.shape[0] == 2, "single-layer block"
    x2d = x.reshape(batch * seq, D_MODEL)
    g_mix = norm_mix_g[0][None, :]
    w_in_b = w_in[0].astype(BF16)
    w_up_b = w_gla_gate_up[0].astype(BF16)
    b_up = b_gla_gate[0][None, :]

    ops, gates, dec = _inproj(x2d, g_mix, w_in_b, w_up_b, b_up, lb_table, ROWS_INPROJ, CHUNK)
    meta_ops, _, _ = _inproj(meta_tokens, g_mix, w_in_b, w_up_b, b_up, lb_table, N_META, N_META)
    o = _mixer(ops, gates, dec, meta_ops, hgrn_norm_g[0][None, :], gla_norm_g[0][None, :],
               batch, seq, ROWS_MIXER)
    y = _ffn(x2d, o, w_out[0].astype(BF16), norm_ffn_g[0][None, :], w_ffn_in[0].astype(BF16),
             w_ffn_out[0].astype(BF16), norm_final_g[None, :], ROWS_FFN)
    return y.reshape(batch, seq, D_MODEL)
```

```python
import functools

import jax
import jax.numpy as jnp
from jax import lax
from jax.experimental import pallas as pl
from jax.experimental.pallas import tpu as pltpu

F32 = jnp.float32
BF16 = jnp.bfloat16

D_MODEL = 1024
N_META = 16
CHUNK = 64
HEADS = 4
HEAD_V = 128
HGRN_WIDTH = HEADS * HEAD_V
GLA_K = 64
GLA_QK_WIDTH = HEADS * GLA_K
GLA_WIDTH = HEADS * HEAD_V
MIX_WIDTH = HGRN_WIDTH + GLA_WIDTH
GATE_RANK = 16
GATE_NORMALIZER = 16.0
FFN_HIDDEN = 2816
NORM_EPS = 1e-6
LANES = 128

C_HQ, C_HF, C_HI, C_HG = 0, 512, 1024, 1536
C_GQ, C_GK, C_GV, C_GG = 2048, 2304, 2560, 3072
C_GA = 3584
IN_COLS = C_GA + GATE_RANK

O_HQD, O_HKD, O_HKS, O_HV = 0, 512, 1024, 1536
O_GQD, O_GKD, O_GKS, O_GV = 2048, 2560, 2816, 3328
OPS_COLS = O_GV + GLA_WIDTH
DEC_COLS = HGRN_WIDTH + GLA_QK_WIDTH
N_STATES = 2 * HEADS

HEAD_TABLE = tuple(
    [(O_HQD + LANES * h, O_HKD + LANES * h, O_HKS + LANES * h, O_HV + LANES * h, LANES * h)
     for h in range(HEADS)]
    + [(O_GQD + LANES * h, O_GKD + LANES * (h // 2), O_GKS + LANES * h, O_GV + LANES * h,
        HGRN_WIDTH + LANES * (h // 2)) for h in range(HEADS)])

V7X_VMEM_LIMIT = 56 * 1024 * 1024

ROWS_INPROJ = 512
ROWS_MIXER = 512
ROWS_FFN = 512


def _rmsnorm(x, g):
    return x * lax.rsqrt(jnp.mean(x * x, axis=-1, keepdims=True) + NORM_EPS) * g


def _dot(a, b):
    return jnp.dot(a.astype(BF16), b.astype(BF16), preferred_element_type=F32)


def _dot_nt(a, b):
    return lax.dot_general(a.astype(BF16), b.astype(BF16), (((1,), (1,)), ((), ())),
                           preferred_element_type=F32)


def _dot_tn(a, b):
    return lax.dot_general(a.astype(BF16), b.astype(BF16), (((0,), (0,)), ((), ())),
                           preferred_element_type=F32)


def _silu(x):
    return x * jax.nn.sigmoid(x)


def _log_sigmoid(x):
    return jnp.minimum(x, 0.0) - jnp.log1p(jnp.exp(-jnp.abs(x)))


def _tril_ones(n):
    r = lax.broadcasted_iota(jnp.int32, (n, n), 0)
    c = lax.broadcasted_iota(jnp.int32, (n, n), 1)
    return r >= c


def _cumsum_rows(g, tri_bf16):
    g_hi = g.astype(BF16)
    g_lo = (g - g_hi.astype(F32)).astype(BF16)
    return (jnp.dot(tri_bf16, g_hi, preferred_element_type=F32)
            + jnp.dot(tri_bf16, g_lo, preferred_element_type=F32))


def _own_half(x, h):
    lane = lax.broadcasted_iota(jnp.int32, x.shape, 1)
    return jnp.where((lane >= GLA_K) == bool(h % 2), x, 0.0)


def _inproj_kernel(x_ref, g_ref, w_ref, wup_ref, bup_ref, lbt_ref, ops_ref, gate_ref, dec_ref,
                   *, chunk):
    n_chunks = x_ref.shape[0] // chunk
    u = _rmsnorm(x_ref[...], g_ref[...]).astype(BF16)
    tri = _tril_ones(chunk).astype(BF16)

    def proj(col, width):
        return jnp.dot(u, w_ref[:, col:col + width], preferred_element_type=F32)

    def decay_factors(g, c):
        b = _cumsum_rows(g[c * chunk:(c + 1) * chunk], tri)
        b_last = b[chunk - 1:, :]
        return jnp.exp(b), jnp.exp(-b), jnp.exp(b_last - b), jnp.exp(b_last)

    t0 = lbt_ref[0:1, :]
    t1 = lbt_ref[1:2, :]
    m = jnp.maximum(t0, t1)
    e0 = jnp.exp(t0 - m)
    lb = e0 / (e0 + jnp.exp(t1 - m))

    f = lb + (1.0 - lb) * jax.nn.sigmoid(proj(C_HF, HGRN_WIDTH))
    g = jnp.log(f)
    k = 1.0 - f
    q = _silu(proj(C_HQ, HGRN_WIDTH))
    for c in range(n_chunks):
        rows = slice(c * chunk, (c + 1) * chunk)
        e_b, e_nb, e_end, dec = decay_factors(g, c)
        ops_ref[rows, O_HQD:O_HQD + HGRN_WIDTH] = (q[rows] * e_b).astype(BF16)
        ops_ref[rows, O_HKD:O_HKD + HGRN_WIDTH] = (k[rows] * e_nb).astype(BF16)
        ops_ref[rows, O_HKS:O_HKS + HGRN_WIDTH] = (k[rows] * e_end).astype(BF16)
        dec_ref[c:c + 1, 0:HGRN_WIDTH] = dec
    ops_ref[:, O_HV:O_HV + HGRN_WIDTH] = proj(C_HI, HGRN_WIDTH).astype(BF16)
    gate_ref[:, 0:HGRN_WIDTH] = _silu(proj(C_HG, HGRN_WIDTH))

    logits = _dot(proj(C_GA, GATE_RANK), wup_ref[...]) + bup_ref[...]
    g = _log_sigmoid(logits) / GATE_NORMALIZER
    q = proj(C_GQ, GLA_QK_WIDTH) * (GLA_K ** -0.5)
    k = proj(C_GK, GLA_QK_WIDTH)
    for c in range(n_chunks):
        rows = slice(c * chunk, (c + 1) * chunk)
        e_b, e_nb, e_end, dec = decay_factors(g, c)
        q_d = q[rows] * e_b
        k_s = k[rows] * e_end
        ops_ref[rows, O_GKD:O_GKD + GLA_QK_WIDTH] = (k[rows] * e_nb).astype(BF16)
        for h in range(HEADS):
            pair = slice(LANES * (h // 2), LANES * (h // 2 + 1))
            ops_ref[rows, O_GQD + LANES * h:O_GQD + LANES * (h + 1)] = (
                _own_half(q_d[:, pair], h).astype(BF16))
            ops_ref[rows, O_GKS + LANES * h:O_GKS + LANES * (h + 1)] = (
                _own_half(k_s[:, pair], h).astype(BF16))
        dec_ref[c:c + 1, HGRN_WIDTH:DEC_COLS] = dec
    ops_ref[:, O_GV:O_GV + GLA_WIDTH] = proj(C_GV, GLA_WIDTH).astype(BF16)
    gate_ref[:, HGRN_WIDTH:MIX_WIDTH] = _silu(proj(C_GG, GLA_WIDTH))


def _inproj(x2d, g, w_in, w_up, b_up, lb_table, rows, chunk):
    n = x2d.shape[0]
    chunks = rows // chunk
    const = lambda i: (0, 0)
    tile = lambda i: (i, 0)
    return pl.pallas_call(
        functools.partial(_inproj_kernel, chunk=chunk),
        grid=(n // rows,),
        in_specs=[
            pl.BlockSpec((rows, D_MODEL), tile),
            pl.BlockSpec((1, D_MODEL), const),
            pl.BlockSpec((D_MODEL, IN_COLS), const, pipeline_mode=pl.Buffered(1)),
            pl.BlockSpec((GATE_RANK, GLA_QK_WIDTH), const),
            pl.BlockSpec((1, GLA_QK_WIDTH), const),
            pl.BlockSpec(lb_table.shape, const),
        ],
        out_specs=[
            pl.BlockSpec((rows, OPS_COLS), tile),
            pl.BlockSpec((rows, MIX_WIDTH), tile),
            pl.BlockSpec((chunks, DEC_COLS), tile),
        ],
        out_shape=[
            jax.ShapeDtypeStruct((n, OPS_COLS), BF16),
            jax.ShapeDtypeStruct((n, MIX_WIDTH), F32),
            jax.ShapeDtypeStruct((n // chunk, DEC_COLS), F32),
        ],
        compiler_params=pltpu.CompilerParams(
            dimension_semantics=("arbitrary",), vmem_limit_bytes=V7X_VMEM_LIMIT),
        name="inproj",
    )(x2d, g, w_in, w_up, b_up, lb_table)


def _mixer_kernel(ops_ref, gate_ref, dec_ref, mops_ref, hnorm_ref, gnorm_ref, o_ref, st_ref):
    @pl.when(pl.program_id(1) == 0)
    def _():
        for i, (_, _, ks_col, v_col, _) in enumerate(HEAD_TABLE):
            st_ref[i] = _dot_tn(mops_ref[:, v_col:v_col + LANES],
                                mops_ref[:, ks_col:ks_col + LANES])

    causal = _tril_ones(CHUNK)
    norms = (hnorm_ref[...],) * HEADS + (gnorm_ref[...],) * HEADS

    def chunk_body(c, carry):
        rows = pl.ds(pl.multiple_of(c * CHUNK, CHUNK), CHUNK)
        dec = dec_ref[pl.ds(c, 1), :]
        scores, outs = [], []
        for q_col, k_col, _, _, _ in HEAD_TABLE:
            scores.append(jnp.where(causal, _dot_nt(ops_ref[rows, q_col:q_col + LANES],
                                                    ops_ref[rows, k_col:k_col + LANES]), 0.0))
        for i, (q_col, _, _, v_col, _) in enumerate(HEAD_TABLE):
            outs.append(_dot(scores[i], ops_ref[rows, v_col:v_col + LANES])
                        + _dot_nt(ops_ref[rows, q_col:q_col + LANES], st_ref[i]))
        for i, (_, _, ks_col, v_col, dec_col) in enumerate(HEAD_TABLE):
            st_ref[i] = (st_ref[i] * dec[:, dec_col:dec_col + LANES]
                         + _dot_tn(ops_ref[rows, v_col:v_col + LANES],
                                   ops_ref[rows, ks_col:ks_col + LANES]))
        for i in range(N_STATES):
            o = _rmsnorm(outs[i], norms[i]) * gate_ref[rows, LANES * i:LANES * (i + 1)]
            o_ref[rows, LANES * i:LANES * (i + 1)] = o.astype(o_ref.dtype)
        return carry

    lax.fori_loop(0, ops_ref.shape[0] // CHUNK, chunk_body, 0, unroll=True)


def _mixer(ops, gates, dec, meta_ops, hnorm, gnorm, batch, seq, rows):
    steps = seq // rows
    chunks = rows // CHUNK
    const = lambda b, t: (0, 0)
    tile = lambda b, t: (b * steps + t, 0)
    return pl.pallas_call(
        _mixer_kernel,
        grid=(batch, steps),
        in_specs=[
            pl.BlockSpec((rows, OPS_COLS), tile),
            pl.BlockSpec((rows, MIX_WIDTH), tile),
            pl.BlockSpec((chunks, DEC_COLS), tile),
            pl.BlockSpec((N_META, OPS_COLS), const),
            pl.BlockSpec((1, HEAD_V), const),
            pl.BlockSpec((1, HEAD_V), const),
        ],
        out_specs=pl.BlockSpec((rows, MIX_WIDTH), tile),
        out_shape=jax.ShapeDtypeStruct((batch * seq, MIX_WIDTH), BF16),
        scratch_shapes=[pltpu.VMEM((N_STATES, HEAD_V, LANES), F32)],
        compiler_params=pltpu.CompilerParams(
            dimension_semantics=("arbitrary", "arbitrary"), vmem_limit_bytes=V7X_VMEM_LIMIT),
        name="mixer",
    )(ops, gates, dec, meta_ops, hnorm, gnorm)


def _ffn_kernel(x_ref, o_ref, wout_ref, g2_ref, w1_ref, w2_ref, g3_ref, y_ref):
    h = x_ref[...] + jnp.dot(o_ref[...], wout_ref[...], preferred_element_type=F32)
    u = _rmsnorm(h, g2_ref[...]).astype(BF16)
    gate = jnp.dot(u, w1_ref[:, :FFN_HIDDEN], preferred_element_type=F32)
    up = jnp.dot(u, w1_ref[:, FFN_HIDDEN:], preferred_element_type=F32)
    h = h + _dot(_silu(gate) * up, w2_ref[...])
    y_ref[...] = _rmsnorm(h, g3_ref[...])


def _ffn(x2d, o, w_out, g2, w1, w2, g3, rows):
    n = x2d.shape[0]
    const = lambda i: (0, 0)
    tile = lambda i: (i, 0)
    resident = functools.partial(pl.BlockSpec, index_map=const, pipeline_mode=pl.Buffered(1))
    return pl.pallas_call(
        _ffn_kernel,
        grid=(n // rows,),
        in_specs=[
            pl.BlockSpec((rows, D_MODEL), tile),
            pl.BlockSpec((rows, D_MODEL), tile),
            resident((D_MODEL, D_MODEL)),
            pl.BlockSpec((1, D_MODEL), const),
            resident((D_MODEL, 2 * FFN_HIDDEN)),
            resident((FFN_HIDDEN, D_MODEL)),
            pl.BlockSpec((1, D_MODEL), const),
        ],
        out_specs=pl.BlockSpec((rows, D_MODEL), tile),
        out_shape=jax.ShapeDtypeStruct((n, D_MODEL), F32),
        compiler_params=pltpu.CompilerParams(
            dimension_semantics=("arbitrary",), vmem_limit_bytes=V7X_VMEM_LIMIT),
        name="outproj_ffn",
    )(x2d, o, w_out, g2, w1, w2, g3)


def kernel(x, meta_tokens, lb_table, norm_mix_g, w_in, w_gla_gate_up, b_gla_gate, hgrn_norm_g,
           gla_norm_g, w_out, norm_ffn_g, w_ffn_in, w_ffn_out, norm_final_g):
    batch, seq, _ = x.shape
    assert w_in.shape[0] == 1 and lb_table.shape[0] == 2, "single-layer block"
    x2d = x.reshape(batch * seq, D_MODEL)
    g_mix = norm_mix_g[0][None, :]
    w_in_b = w_in[0].astype(BF16)
    w_up_b = w_gla_gate_up[0].astype(BF16)
    b_up = b_gla_gate[0][None, :]

    ops, gates, dec = _inproj(x2d, g_mix, w_in_b, w_up_b, b_up, lb_table, ROWS_INPROJ, CHUNK)
    meta_ops, _, _ = _inproj(meta_tokens, g_mix, w_in_b, w_up_b, b_up, lb_table, N_META, N_META)
    o = _mixer(ops, gates, dec, meta_ops, hgrn_norm_g[0][None, :], gla_norm_g[0][None, :],
               batch, seq, ROWS_MIXER)
    y = _ffn(x2d, o, w_out[0].astype(BF16), norm_ffn_g[0][None, :], w_ffn_in[0].astype(BF16),
             w_ffn_out[0].astype(BF16), norm_final_g[None, :], ROWS_FFN)
    return y.reshape(batch, seq, D_MODEL)
```

```python
import functools

import jax
import jax.numpy as jnp
from jax import lax
from jax.experimental import pallas as pl
from jax.experimental.pallas import tpu as pltpu

F32 = jnp.float32
BF16 = jnp.bfloat16

D_MODEL = 1024
N_META = 16
CHUNK = 64
HEADS = 4
HEAD_V = 128
HGRN_WIDTH = HEADS * HEAD_V
GLA_K = 64
GLA_QK_WIDTH = HEADS * GLA_K
GLA_WIDTH = HEADS * HEAD_V
MIX_WIDTH = HGRN_WIDTH + GLA_WIDTH
GATE_RANK = 16
GATE_NORMALIZER = 16.0
FFN_HIDDEN = 2816
NORM_EPS = 1e-6
LANES = 128

C_HQ, C_HF, C_HI, C_HG = 0, 512, 1024, 1536
C_GQ, C_GK, C_GV, C_GG = 2048, 2304, 2560, 3072
C_GA = 3584
IN_COLS = C_GA + GATE_RANK

O_HQD, O_HKD, O_HKS, O_HV = 0, 512, 1024, 1536
O_GQD, O_GKD, O_GKS, O_GV = 2048, 2560, 2816, 3328
OPS_COLS = O_GV + GLA_WIDTH
DEC_COLS = HGRN_WIDTH + GLA_QK_WIDTH
N_STATES = 2 * HEADS

HEAD_TABLE = tuple(
    [(O_HQD + LANES * h, O_HKD + LANES * h, O_HKS + LANES * h, O_HV + LANES * h, LANES * h)
     for h in range(HEADS)]
    + [(O_GQD + LANES * h, O_GKD + LANES * (h // 2), O_GKS + LANES * h, O_GV + LANES * h,
        HGRN_WIDTH + LANES * (h // 2)) for h in range(HEADS)])

V7X_VMEM_LIMIT = 56 * 1024 * 1024

ROWS_INPROJ = 512
ROWS_MIXER = 1024
ROWS_FFN = 512


def _rmsnorm(x, g):
    return x * lax.rsqrt(jnp.mean(x * x, axis=-1, keepdims=True) + NORM_EPS) * g


def _dot(a, b):
    return jnp.dot(a.astype(BF16), b.astype(BF16), preferred_element_type=F32)


def _dot_nt(a, b):
    return lax.dot_general(a.astype(BF16), b.astype(BF16), (((1,), (1,)), ((), ())),
                           preferred_element_type=F32)


def _dot_tn(a, b):
    return lax.dot_general(a.astype(BF16), b.astype(BF16), (((0,), (0,)), ((), ())),
                           preferred_element_type=F32)


def _silu(x):
    return x * jax.nn.sigmoid(x)


def _log_sigmoid(x):
    return jnp.minimum(x, 0.0) - jnp.log1p(jnp.exp(-jnp.abs(x)))


def _tril_ones(n):
    r = lax.broadcasted_iota(jnp.int32, (n, n), 0)
    c = lax.broadcasted_iota(jnp.int32, (n, n), 1)
    return r >= c


def _cumsum_rows(g, tri_bf16):
    g_hi = g.astype(BF16)
    g_lo = (g - g_hi.astype(F32)).astype(BF16)
    return (jnp.dot(tri_bf16, g_hi, preferred_element_type=F32)
            + jnp.dot(tri_bf16, g_lo, preferred_element_type=F32))


def _own_half(x, h):
    lane = lax.broadcasted_iota(jnp.int32, x.shape, 1)
    return jnp.where((lane >= GLA_K) == bool(h % 2), x, 0.0)


def _project(x_ref, g_ref, w_ref, wup_ref, bup_ref, lbt_ref, ops_ref, gate_ref, dec_ref, chunk):
    n_chunks = x_ref.shape[0] // chunk
    u = _rmsnorm(x_ref[...], g_ref[...]).astype(BF16)
    tri = _tril_ones(chunk).astype(BF16)

    def proj(col, width):
        return jnp.dot(u, w_ref[:, col:col + width], preferred_element_type=F32)

    def decay_factors(g, c):
        b = _cumsum_rows(g[c * chunk:(c + 1) * chunk], tri)
        b_last = b[chunk - 1:, :]
        return jnp.exp(b), jnp.exp(-b), jnp.exp(b_last - b), jnp.exp(b_last)

    t0 = lbt_ref[0:1, :]
    t1 = lbt_ref[1:2, :]
    m = jnp.maximum(t0, t1)
    e0 = jnp.exp(t0 - m)
    lb = e0 / (e0 + jnp.exp(t1 - m))

    f = lb + (1.0 - lb) * jax.nn.sigmoid(proj(C_HF, HGRN_WIDTH))
    g = jnp.log(f)
    k = 1.0 - f
    q = _silu(proj(C_HQ, HGRN_WIDTH))
    for c in range(n_chunks):
        rows = slice(c * chunk, (c + 1) * chunk)
        e_b, e_nb, e_end, dec = decay_factors(g, c)
        ops_ref[rows, O_HQD:O_HQD + HGRN_WIDTH] = (q[rows] * e_b).astype(BF16)
        ops_ref[rows, O_HKD:O_HKD + HGRN_WIDTH] = (k[rows] * e_nb).astype(BF16)
        ops_ref[rows, O_HKS:O_HKS + HGRN_WIDTH] = (k[rows] * e_end).astype(BF16)
        dec_ref[c:c + 1, 0:HGRN_WIDTH] = dec
    ops_ref[:, O_HV:O_HV + HGRN_WIDTH] = proj(C_HI, HGRN_WIDTH).astype(BF16)
    gate_ref[:, 0:HGRN_WIDTH] = _silu(proj(C_HG, HGRN_WIDTH))

    logits = _dot(proj(C_GA, GATE_RANK), wup_ref[...]) + bup_ref[...]
    g = _log_sigmoid(logits) / GATE_NORMALIZER
    q = proj(C_GQ, GLA_QK_WIDTH) * (GLA_K ** -0.5)
    k = proj(C_GK, GLA_QK_WIDTH)
    for c in range(n_chunks):
        rows = slice(c * chunk, (c + 1) * chunk)
        e_b, e_nb, e_end, dec = decay_factors(g, c)
        q_d = q[rows] * e_b
        k_s = k[rows] * e_end
        ops_ref[rows, O_GKD:O_GKD + GLA_QK_WIDTH] = (k[rows] * e_nb).astype(BF16)
        for h in range(HEADS):
            pair = slice(LANES * (h // 2), LANES * (h // 2 + 1))
            ops_ref[rows, O_GQD + LANES * h:O_GQD + LANES * (h + 1)] = (
                _own_half(q_d[:, pair], h).astype(BF16))
            ops_ref[rows, O_GKS + LANES * h:O_GKS + LANES * (h + 1)] = (
                _own_half(k_s[:, pair], h).astype(BF16))
        dec_ref[c:c + 1, HGRN_WIDTH:DEC_COLS] = dec
    ops_ref[:, O_GV:O_GV + GLA_WIDTH] = proj(C_GV, GLA_WIDTH).astype(BF16)
    gate_ref[:, HGRN_WIDTH:MIX_WIDTH] = _silu(proj(C_GG, GLA_WIDTH))


def _meta_kernel(x_ref, g_ref, w32_ref, wup_ref, bup_ref, lbt_ref,
                 w_ref, ops_ref, gate_ref, dec_ref):
    w_ref[...] = w32_ref[...].astype(BF16)
    _project(x_ref, g_ref, w_ref, wup_ref, bup_ref, lbt_ref, ops_ref, gate_ref, dec_ref, N_META)


def _meta(meta_tokens, g, w_in, w_up, b_up, lb_table):
    full = lambda a, **kw: pl.BlockSpec(a.shape, lambda i: (0,) * len(a.shape), **kw)
    args = (meta_tokens, g, w_in, w_up, b_up, lb_table)
    in_specs = [full(a) for a in args]
    in_specs[2] = full(w_in, pipeline_mode=pl.Buffered(1))
    out_shapes = [jax.ShapeDtypeStruct(w_in.shape, BF16),
                  jax.ShapeDtypeStruct((N_META, OPS_COLS), BF16),
                  jax.ShapeDtypeStruct((N_META, MIX_WIDTH), F32),
                  jax.ShapeDtypeStruct((1, DEC_COLS), F32)]
    w_in_b, meta_ops, _, _ = pl.pallas_call(
        _meta_kernel,
        grid=(1,),
        in_specs=in_specs,
        out_specs=[full(s) for s in out_shapes],
        out_shape=out_shapes,
        compiler_params=pltpu.CompilerParams(
            dimension_semantics=("arbitrary",), vmem_limit_bytes=V7X_VMEM_LIMIT),
        name="meta",
    )(*args)
    return w_in_b, meta_ops


def _inproj_kernel(x_ref, g_ref, w_ref, wup_ref, bup_ref, lbt_ref, wout32_ref, w1_32_ref,
                   w2_32_ref, ops_ref, gate_ref, dec_ref, wout_ref, w1_ref, w2_ref):
    _project(x_ref, g_ref, w_ref, wup_ref, bup_ref, lbt_ref, ops_ref, gate_ref, dec_ref, CHUNK)
    wout_ref[...] = wout32_ref[...].astype(BF16)
    w1_ref[...] = w1_32_ref[...].astype(BF16)
    w2_ref[...] = w2_32_ref[...].astype(BF16)


def _inproj(x2d, g, w_in, w_up, b_up, lb_table, w_out, w1, w2, rows):
    n = x2d.shape[0]
    steps = n // rows
    chunks = rows // CHUNK
    wout_rows = w_out.shape[0] // steps
    w1_rows = w1.shape[0] // steps
    w2_rows = 2 * w2.shape[0] // steps
    assert wout_rows * steps == w_out.shape[0] and w1_rows * steps == w1.shape[0]
    assert w2_rows * steps == 2 * w2.shape[0]
    const = lambda i: (0, 0)
    tile = lambda i: (i, 0)
    half_pace = lambda i: (i // 2, 0)
    w_specs = [pl.BlockSpec((wout_rows, w_out.shape[1]), tile),
               pl.BlockSpec((w1_rows, w1.shape[1]), tile),
               pl.BlockSpec((w2_rows, w2.shape[1]), half_pace)]
    return pl.pallas_call(
        _inproj_kernel,
        grid=(steps,),
        in_specs=[
            pl.BlockSpec((rows, D_MODEL), tile),
            pl.BlockSpec((1, D_MODEL), const),
            pl.BlockSpec((D_MODEL, IN_COLS), const, pipeline_mode=pl.Buffered(1)),
            pl.BlockSpec((GATE_RANK, GLA_QK_WIDTH), const),
            pl.BlockSpec((1, GLA_QK_WIDTH), const),
            pl.BlockSpec(lb_table.shape, const),
        ] + w_specs,
        out_specs=[
            pl.BlockSpec((rows, OPS_COLS), tile),
            pl.BlockSpec((rows, MIX_WIDTH), tile),
            pl.BlockSpec((chunks, DEC_COLS), tile),
        ] + w_specs,
        out_shape=[
            jax.ShapeDtypeStruct((n, OPS_COLS), BF16),
            jax.ShapeDtypeStruct((n, MIX_WIDTH), F32),
            jax.ShapeDtypeStruct((n // CHUNK, DEC_COLS), F32),
            jax.ShapeDtypeStruct(w_out.shape, BF16),
            jax.ShapeDtypeStruct(w1.shape, BF16),
            jax.ShapeDtypeStruct(w2.shape, BF16),
        ],
        compiler_params=pltpu.CompilerParams(
            dimension_semantics=("arbitrary",), vmem_limit_bytes=V7X_VMEM_LIMIT),
        name="inproj",
    )(x2d, g, w_in, w_up, b_up, lb_table, w_out, w1, w2)


def _mixer_kernel(ops_ref, gate_ref, dec_ref, mops_ref, hnorm_ref, gnorm_ref, o_ref, st_ref,
                  *, steps_per_seq):
    @pl.when(lax.rem(pl.program_id(0), steps_per_seq) == 0)
    def _():
        for i, (_, _, ks_col, v_col, _) in enumerate(HEAD_TABLE):
            st_ref[i] = _dot_tn(mops_ref[:, v_col:v_col + LANES],
                                mops_ref[:, ks_col:ks_col + LANES])

    causal = _tril_ones(CHUNK)
    norms = (hnorm_ref[...],) * HEADS + (gnorm_ref[...],) * HEADS

    for c in range(ops_ref.shape[0] // CHUNK):
        rows = slice(c * CHUNK, (c + 1) * CHUNK)
        dec = dec_ref[c:c + 1, :]
        scores, outs = [], []
        for q_col, k_col, _, _, _ in HEAD_TABLE:
            scores.append(jnp.where(causal, _dot_nt(ops_ref[rows, q_col:q_col + LANES],
                                                    ops_ref[rows, k_col:k_col + LANES]), 0.0))
        for i, (q_col, _, _, v_col, _) in enumerate(HEAD_TABLE):
            outs.append(_dot(scores[i], ops_ref[rows, v_col:v_col + LANES])
                        + _dot_nt(ops_ref[rows, q_col:q_col + LANES], st_ref[i]))
        for i, (_, _, ks_col, v_col, dec_col) in enumerate(HEAD_TABLE):
            st_ref[i] = (st_ref[i] * dec[:, dec_col:dec_col + LANES]
                         + _dot_tn(ops_ref[rows, v_col:v_col + LANES],
                                   ops_ref[rows, ks_col:ks_col + LANES]))
        for i in range(N_STATES):
            o = _rmsnorm(outs[i], norms[i]) * gate_ref[rows, LANES * i:LANES * (i + 1)]
            o_ref[rows, LANES * i:LANES * (i + 1)] = o.astype(o_ref.dtype)


def _mixer(ops, gates, dec, meta_ops, hnorm, gnorm, seq, rows):
    n = ops.shape[0]
    chunks = rows // CHUNK
    const = lambda s: (0, 0)
    tile = lambda s: (s, 0)
    return pl.pallas_call(
        functools.partial(_mixer_kernel, steps_per_seq=seq // rows),
        grid=(n // rows,),
        in_specs=[
            pl.BlockSpec((rows, OPS_COLS), tile),
            pl.BlockSpec((rows, MIX_WIDTH), tile),
            pl.BlockSpec((chunks, DEC_COLS), tile),
            pl.BlockSpec((N_META, OPS_COLS), const),
            pl.BlockSpec((1, HEAD_V), const),
            pl.BlockSpec((1, HEAD_V), const),
        ],
        out_specs=pl.BlockSpec((rows, MIX_WIDTH), tile),
        out_shape=jax.ShapeDtypeStruct((n, MIX_WIDTH), BF16),
        scratch_shapes=[pltpu.VMEM((N_STATES, HEAD_V, LANES), F32)],
        compiler_params=pltpu.CompilerParams(
            dimension_semantics=("arbitrary",), vmem_limit_bytes=V7X_VMEM_LIMIT),
        name="mixer",
    )(ops, gates, dec, meta_ops, hnorm, gnorm)


def _ffn_kernel(x_ref, o_ref, wout_ref, g2_ref, w1_ref, w2_ref, g3_ref, y_ref):
    h = x_ref[...] + jnp.dot(o_ref[...], wout_ref[...], preferred_element_type=F32)
    u = _rmsnorm(h, g2_ref[...]).astype(BF16)
    gate = jnp.dot(u, w1_ref[:, :FFN_HIDDEN], preferred_element_type=F32)
    up = jnp.dot(u, w1_ref[:, FFN_HIDDEN:], preferred_element_type=F32)
    h = h + _dot(_silu(gate) * up, w2_ref[...])
    y_ref[...] = _rmsnorm(h, g3_ref[...])


def _ffn(x2d, o, w_out, g2, w1, w2, g3, rows):
    n = x2d.shape[0]
    const = lambda i: (0, 0)
    tile = lambda i: (i, 0)
    resident = functools.partial(pl.BlockSpec, index_map=const, pipeline_mode=pl.Buffered(1))
    return pl.pallas_call(
        _ffn_kernel,
        grid=(n // rows,),
        in_specs=[
            pl.BlockSpec((rows, D_MODEL), tile),
            pl.BlockSpec((rows, D_MODEL), tile),
            resident((D_MODEL, D_MODEL)),
            pl.BlockSpec((1, D_MODEL), const),
            resident((D_MODEL, 2 * FFN_HIDDEN)),
            resident((FFN_HIDDEN, D_MODEL)),
            pl.BlockSpec((1, D_MODEL), const),
        ],
        out_specs=pl.BlockSpec((rows, D_MODEL), tile),
        out_shape=jax.ShapeDtypeStruct((n, D_MODEL), F32),
        compiler_params=pltpu.CompilerParams(
            dimension_semantics=("arbitrary",), vmem_limit_bytes=V7X_VMEM_LIMIT),
        name="outproj_ffn",
    )(x2d, o, w_out, g2, w1, w2, g3)


def kernel(x, meta_tokens, lb_table, norm_mix_g, w_in, w_gla_gate_up, b_gla_gate, hgrn_norm_g,
           gla_norm_g, w_out, norm_ffn_g, w_ffn_in, w_ffn_out, norm_final_g):
    batch, seq, _ = x.shape
    assert w_in.shape[0] == 1 and lb_table.shape[0] == 2, "single-layer block"
    assert seq % ROWS_MIXER == 0
    x2d = x.reshape(batch * seq, D_MODEL)
    g_mix = norm_mix_g[0][None, :]
    w_up = w_gla_gate_up[0]
    b_up = b_gla_gate[0][None, :]

    w_in_b, meta_ops = _meta(meta_tokens, g_mix, w_in[0], w_up, b_up, lb_table)
    ops, gates, dec, w_out_b, w1_b, w2_b = _inproj(
        x2d, g_mix, w_in_b, w_up, b_up, lb_table, w_out[0], w_ffn_in[0], w_ffn_out[0],
        ROWS_INPROJ)
    o = _mixer(ops, gates, dec, meta_ops, hgrn_norm_g[0][None, :], gla_norm_g[0][None, :],
               seq, ROWS_MIXER)
    y = _ffn(x2d, o, w_out_b, norm_ffn_g[0][None, :], w1_b, w2_b, norm_final_g[None, :],
             ROWS_FFN)
    return y.reshape(batch, seq, D_MODEL)
```

```python
import functools

import jax
import jax.numpy as jnp
from jax import lax
from jax.experimental import pallas as pl
from jax.experimental.pallas import tpu as pltpu

F32 = jnp.float32
BF16 = jnp.bfloat16

D_MODEL = 1024
N_META = 16
CHUNK = 64
HEADS = 4
HEAD_V = 128
HGRN_WIDTH = HEADS * HEAD_V
GLA_K = 64
GLA_QK_WIDTH = HEADS * GLA_K
GLA_WIDTH = HEADS * HEAD_V
MIX_WIDTH = HGRN_WIDTH + GLA_WIDTH
GATE_RANK = 16
GATE_NORMALIZER = 16.0
FFN_HIDDEN = 2816
NORM_EPS = 1e-6
LANES = 128

C_HQ, C_HF, C_HI, C_HG = 0, 512, 1024, 1536
C_GQ, C_GK, C_GV, C_GG = 2048, 2304, 2560, 3072
C_GA = 3584
IN_COLS = C_GA + GATE_RANK
W_BLOCK = 512

O_HQD, O_HKD, O_HKS, O_HV = 0, 512, 1024, 1536
O_GQD, O_GKD, O_GKS, O_GV = 2048, 2560, 2816, 3328
OPS_COLS = O_GV + GLA_WIDTH
DEC_COLS = HGRN_WIDTH + GLA_QK_WIDTH
N_STATES = 2 * HEADS

HEAD_TABLE = tuple(
    [(O_HQD + LANES * h, O_HKD + LANES * h, O_HKS + LANES * h, O_HV + LANES * h, LANES * h)
     for h in range(HEADS)]
    + [(O_GQD + LANES * h, O_GKD + LANES * (h // 2), O_GKS + LANES * h, O_GV + LANES * h,
        HGRN_WIDTH + LANES * (h // 2)) for h in range(HEADS)])

V7X_VMEM_LIMIT = 56 * 1024 * 1024

ROWS_INPROJ = 512
ROWS_MIXER = 1024
ROWS_FFN = 512


def _rmsnorm(x, g):
    return x * lax.rsqrt(jnp.mean(x * x, axis=-1, keepdims=True) + NORM_EPS) * g


def _dot(a, b):
    return jnp.dot(a.astype(BF16), b.astype(BF16), preferred_element_type=F32)


def _dot_nt(a, b):
    return lax.dot_general(a.astype(BF16), b.astype(BF16), (((1,), (1,)), ((), ())),
                           preferred_element_type=F32)


def _dot_tn(a, b):
    return lax.dot_general(a.astype(BF16), b.astype(BF16), (((0,), (0,)), ((), ())),
                           preferred_element_type=F32)


def _silu(x):
    return x * jax.nn.sigmoid(x)


def _log_sigmoid(x):
    return jnp.minimum(x, 0.0) - jnp.log1p(jnp.exp(-jnp.abs(x)))


def _tril_ones(n):
    r = lax.broadcasted_iota(jnp.int32, (n, n), 0)
    c = lax.broadcasted_iota(jnp.int32, (n, n), 1)
    return r >= c


def _cumsum_rows(g, tri_bf16):
    g_hi = g.astype(BF16)
    g_lo = (g - g_hi.astype(F32)).astype(BF16)
    return (jnp.dot(tri_bf16, g_hi, preferred_element_type=F32)
            + jnp.dot(tri_bf16, g_lo, preferred_element_type=F32))


def _own_half(x, h):
    lane = lax.broadcasted_iota(jnp.int32, x.shape, 1)
    return jnp.where((lane >= GLA_K) == bool(h % 2), x, 0.0)


def _project(x_ref, g_ref, w_ref, wga_ref, wup_ref, bup_ref, lbt_ref, ops_ref, gate_ref, dec_ref,
             chunk):
    n_chunks = x_ref.shape[0] // chunk
    u = _rmsnorm(x_ref[...], g_ref[...]).astype(BF16)
    tri = _tril_ones(chunk).astype(BF16)

    def proj(col, width):
        blk, off = divmod(col, W_BLOCK)
        assert off + width <= W_BLOCK
        return jnp.dot(u, w_ref[blk, :, off:off + width], preferred_element_type=F32)

    def decay_factors(g, c):
        b = _cumsum_rows(g[c * chunk:(c + 1) * chunk], tri)
        b_last = b[chunk - 1:, :]
        return jnp.exp(b), jnp.exp(-b), jnp.exp(b_last - b), jnp.exp(b_last)

    t0 = lbt_ref[0:1, :]
    t1 = lbt_ref[1:2, :]
    m = jnp.maximum(t0, t1)
    e0 = jnp.exp(t0 - m)
    lb = e0 / (e0 + jnp.exp(t1 - m))

    f = lb + (1.0 - lb) * jax.nn.sigmoid(proj(C_HF, HGRN_WIDTH))
    g = jnp.log(f)
    k = 1.0 - f
    q = _silu(proj(C_HQ, HGRN_WIDTH))
    for c in range(n_chunks):
        rows = slice(c * chunk, (c + 1) * chunk)
        e_b, e_nb, e_end, dec = decay_factors(g, c)
        ops_ref[rows, O_HQD:O_HQD + HGRN_WIDTH] = (q[rows] * e_b).astype(BF16)
        ops_ref[rows, O_HKD:O_HKD + HGRN_WIDTH] = (k[rows] * e_nb).astype(BF16)
        ops_ref[rows, O_HKS:O_HKS + HGRN_WIDTH] = (k[rows] * e_end).astype(BF16)
        dec_ref[c:c + 1, 0:HGRN_WIDTH] = dec
    ops_ref[:, O_HV:O_HV + HGRN_WIDTH] = proj(C_HI, HGRN_WIDTH).astype(BF16)
    gate_ref[:, 0:HGRN_WIDTH] = _silu(proj(C_HG, HGRN_WIDTH))

    logits = _dot(_dot_nt(u, wga_ref[...]), wup_ref[...]) + bup_ref[...]
    g = _log_sigmoid(logits) / GATE_NORMALIZER
    q = proj(C_GQ, GLA_QK_WIDTH) * (GLA_K ** -0.5)
    k = proj(C_GK, GLA_QK_WIDTH)
    for c in range(n_chunks):
        rows = slice(c * chunk, (c + 1) * chunk)
        e_b, e_nb, e_end, dec = decay_factors(g, c)
        q_d = q[rows] * e_b
        k_s = k[rows] * e_end
        ops_ref[rows, O_GKD:O_GKD + GLA_QK_WIDTH] = (k[rows] * e_nb).astype(BF16)
        for h in range(HEADS):
            pair = slice(LANES * (h // 2), LANES * (h // 2 + 1))
            ops_ref[rows, O_GQD + LANES * h:O_GQD + LANES * (h + 1)] = (
                _own_half(q_d[:, pair], h).astype(BF16))
            ops_ref[rows, O_GKS + LANES * h:O_GKS + LANES * (h + 1)] = (
                _own_half(k_s[:, pair], h).astype(BF16))
        dec_ref[c:c + 1, HGRN_WIDTH:DEC_COLS] = dec
    ops_ref[:, O_GV:O_GV + GLA_WIDTH] = proj(C_GV, GLA_WIDTH).astype(BF16)
    gate_ref[:, HGRN_WIDTH:MIX_WIDTH] = _silu(proj(C_GG, GLA_WIDTH))


def _meta_kernel(x_ref, g_ref, wt_ref, wgat_ref, wup_ref, bup_ref, lbt_ref,
                 w_ref, wga_ref, ops_ref, gate_ref, dec_ref, w_scr):
    j = pl.program_id(0)
    n_blocks = w_scr.shape[0]

    @pl.when(j < n_blocks)
    def _():
        blk = wt_ref[...].T.astype(BF16)
        w_ref[0] = blk
        w_scr[j] = blk

    @pl.when(j == n_blocks)
    def _():
        wga_ref[...] = wgat_ref[...].astype(BF16)
        _project(x_ref, g_ref, w_scr, wga_ref, wup_ref, bup_ref, lbt_ref, ops_ref, gate_ref,
                 dec_ref, N_META)


def _meta(meta_tokens, g, w_in_t, w_up, b_up, lb_table):
    n_blocks = C_GA // W_BLOCK
    full = lambda a: pl.BlockSpec(a.shape, lambda j: (0,) * len(a.shape))
    last = n_blocks - 1
    out_shapes = [jax.ShapeDtypeStruct((n_blocks, D_MODEL, W_BLOCK), BF16),
                  jax.ShapeDtypeStruct((GATE_RANK, D_MODEL), BF16),
                  jax.ShapeDtypeStruct((N_META, OPS_COLS), BF16),
                  jax.ShapeDtypeStruct((N_META, MIX_WIDTH), F32),
                  jax.ShapeDtypeStruct((1, DEC_COLS), F32)]
    w_main, w_ga, meta_ops, _, _ = pl.pallas_call(
        _meta_kernel,
        grid=(n_blocks + 1,),
        in_specs=[
            full(meta_tokens), full(g),
            pl.BlockSpec((W_BLOCK, D_MODEL), lambda j: (jnp.minimum(j, last), 0)),
            pl.BlockSpec((GATE_RANK, D_MODEL), lambda j: (C_GA // GATE_RANK, 0)),
            full(w_up), full(b_up), full(lb_table),
        ],
        out_specs=[pl.BlockSpec((1, D_MODEL, W_BLOCK), lambda j: (jnp.minimum(j, last), 0, 0))]
        + [full(s) for s in out_shapes[1:]],
        out_shape=out_shapes,
        scratch_shapes=[pltpu.VMEM((n_blocks, D_MODEL, W_BLOCK), BF16)],
        compiler_params=pltpu.CompilerParams(
            dimension_semantics=("arbitrary",), vmem_limit_bytes=V7X_VMEM_LIMIT),
        name="meta",
    )(meta_tokens, g, w_in_t, w_in_t, w_up, b_up, lb_table)
    return w_main, w_ga, meta_ops


def _inproj_kernel(x_ref, g_ref, w_ref, wga_ref, wup_ref, bup_ref, lbt_ref, wout32_ref, w1_32_ref,
                   w2_32_ref, ops_ref, gate_ref, dec_ref, wout_ref, w1_ref, w2_ref):
    _project(x_ref, g_ref, w_ref, wga_ref, wup_ref, bup_ref, lbt_ref, ops_ref, gate_ref, dec_ref,
             CHUNK)
    wout_ref[...] = wout32_ref[...].astype(BF16)
    w1_ref[...] = w1_32_ref[...].astype(BF16)
    w2_ref[...] = w2_32_ref[...].astype(BF16)


def _inproj(x2d, g, w_main, w_ga, w_up, b_up, lb_table, w_out, w1, w2, rows):
    n = x2d.shape[0]
    steps = n // rows
    chunks = rows // CHUNK
    wout_rows = w_out.shape[0] // steps
    w1_rows = w1.shape[0] // steps
    w2_rows = 2 * w2.shape[0] // steps
    assert wout_rows * steps == w_out.shape[0] and w1_rows * steps == w1.shape[0]
    assert w2_rows * steps == 2 * w2.shape[0]
    const = lambda i: (0, 0)
    tile = lambda i: (i, 0)
    half_pace = lambda i: (i // 2, 0)
    w_specs = [pl.BlockSpec((wout_rows, w_out.shape[1]), tile),
               pl.BlockSpec((w1_rows, w1.shape[1]), tile),
               pl.BlockSpec((w2_rows, w2.shape[1]), half_pace)]
    return pl.pallas_call(
        _inproj_kernel,
        grid=(steps,),
        in_specs=[
            pl.BlockSpec((rows, D_MODEL), tile),
            pl.BlockSpec((1, D_MODEL), const),
            pl.BlockSpec(w_main.shape, lambda i: (0, 0, 0), pipeline_mode=pl.Buffered(1)),
            pl.BlockSpec(w_ga.shape, const),
            pl.BlockSpec((GATE_RANK, GLA_QK_WIDTH), const),
            pl.BlockSpec((1, GLA_QK_WIDTH), const),
            pl.BlockSpec(lb_table.shape, const),
        ] + w_specs,
        out_specs=[
            pl.BlockSpec((rows, OPS_COLS), tile),
            pl.BlockSpec((rows, MIX_WIDTH), tile),
            pl.BlockSpec((chunks, DEC_COLS), tile),
        ] + w_specs,
        out_shape=[
            jax.ShapeDtypeStruct((n, OPS_COLS), BF16),
            jax.ShapeDtypeStruct((n, MIX_WIDTH), F32),
            jax.ShapeDtypeStruct((n // CHUNK, DEC_COLS), F32),
            jax.ShapeDtypeStruct(w_out.shape, BF16),
            jax.ShapeDtypeStruct(w1.shape, BF16),
            jax.ShapeDtypeStruct(w2.shape, BF16),
        ],
        compiler_params=pltpu.CompilerParams(
            dimension_semantics=("arbitrary",), vmem_limit_bytes=V7X_VMEM_LIMIT),
        name="inproj",
    )(x2d, g, w_main, w_ga, w_up, b_up, lb_table, w_out, w1, w2)


def _mixer_kernel(ops_ref, gate_ref, dec_ref, mops_ref, hnorm_ref, gnorm_ref, o_ref, st_ref,
                  *, steps_per_seq):
    @pl.when(lax.rem(pl.program_id(0), steps_per_seq) == 0)
    def _():
        for i, (_, _, ks_col, v_col, _) in enumerate(HEAD_TABLE):
            st_ref[i] = _dot_tn(mops_ref[:, v_col:v_col + LANES],
                                mops_ref[:, ks_col:ks_col + LANES])

    causal = _tril_ones(CHUNK)
    norms = (hnorm_ref[...],) * HEADS + (gnorm_ref[...],) * HEADS

    for c in range(ops_ref.shape[0] // CHUNK):
        rows = slice(c * CHUNK, (c + 1) * CHUNK)
        dec = dec_ref[c:c + 1, :]
        scores, outs = [], []
        for q_col, k_col, _, _, _ in HEAD_TABLE:
            scores.append(jnp.where(causal, _dot_nt(ops_ref[rows, q_col:q_col + LANES],
                                                    ops_ref[rows, k_col:k_col + LANES]), 0.0))
        for i, (q_col, _, _, v_col, _) in enumerate(HEAD_TABLE):
            outs.append(_dot(scores[i], ops_ref[rows, v_col:v_col + LANES])
                        + _dot_nt(ops_ref[rows, q_col:q_col + LANES], st_ref[i]))
        for i, (_, _, ks_col, v_col, dec_col) in enumerate(HEAD_TABLE):
            st_ref[i] = (st_ref[i] * dec[:, dec_col:dec_col + LANES]
                         + _dot_tn(ops_ref[rows, v_col:v_col + LANES],
                                   ops_ref[rows, ks_col:ks_col + LANES]))
        for i in range(N_STATES):
            o = _rmsnorm(outs[i], norms[i]) * gate_ref[rows, LANES * i:LANES * (i + 1)]
            o_ref[rows, LANES * i:LANES * (i + 1)] = o.astype(o_ref.dtype)


def _mixer(ops, gates, dec, meta_ops, hnorm, gnorm, seq, rows):
    n = ops.shape[0]
    chunks = rows // CHUNK
    const = lambda s: (0, 0)
    tile = lambda s: (s, 0)
    return pl.pallas_call(
        functools.partial(_mixer_kernel, steps_per_seq=seq // rows),
        grid=(n // rows,),
        in_specs=[
            pl.BlockSpec((rows, OPS_COLS), tile),
            pl.BlockSpec((rows, MIX_WIDTH), tile),
            pl.BlockSpec((chunks, DEC_COLS), tile),
            pl.BlockSpec((N_META, OPS_COLS), const),
            pl.BlockSpec((1, HEAD_V), const),
            pl.BlockSpec((1, HEAD_V), const),
        ],
        out_specs=pl.BlockSpec((rows, MIX_WIDTH), tile),
        out_shape=jax.ShapeDtypeStruct((n, MIX_WIDTH), BF16),
        scratch_shapes=[pltpu.VMEM((N_STATES, HEAD_V, LANES), F32)],
        compiler_params=pltpu.CompilerParams(
            dimension_semantics=("arbitrary",), vmem_limit_bytes=V7X_VMEM_LIMIT),
        name="mixer",
    )(ops, gates, dec, meta_ops, hnorm, gnorm)


def _ffn_kernel(x_ref, o_ref, wout_ref, g2_ref, w1_ref, w2_ref, g3_ref, y_ref):
    h = x_ref[...] + jnp.dot(o_ref[...], wout_ref[...], preferred_element_type=F32)
    u = _rmsnorm(h, g2_ref[...]).astype(BF16)
    gate = jnp.dot(u, w1_ref[:, :FFN_HIDDEN], preferred_element_type=F32)
    up = jnp.dot(u, w1_ref[:, FFN_HIDDEN:], preferred_element_type=F32)
    h = h + _dot(_silu(gate) * up, w2_ref[...])
    y_ref[...] = _rmsnorm(h, g3_ref[...])


def _ffn(x2d, o, w_out, g2, w1, w2, g3, rows):
    n = x2d.shape[0]
    const = lambda i: (0, 0)
    tile = lambda i: (i, 0)
    resident = functools.partial(pl.BlockSpec, index_map=const, pipeline_mode=pl.Buffered(1))
    return pl.pallas_call(
        _ffn_kernel,
        grid=(n // rows,),
        in_specs=[
            pl.BlockSpec((rows, D_MODEL), tile),
            pl.BlockSpec((rows, D_MODEL), tile),
            resident((D_MODEL, D_MODEL)),
            pl.BlockSpec((1, D_MODEL), const),
            resident((D_MODEL, 2 * FFN_HIDDEN)),
            resident((FFN_HIDDEN, D_MODEL)),
            pl.BlockSpec((1, D_MODEL), const),
        ],
        out_specs=pl.BlockSpec((rows, D_MODEL), tile),
        out_shape=jax.ShapeDtypeStruct((n, D_MODEL), F32),
        compiler_params=pltpu.CompilerParams(
            dimension_semantics=("arbitrary",), vmem_limit_bytes=V7X_VMEM_LIMIT),
        name="outproj_ffn",
    )(x2d, o, w_out, g2, w1, w2, g3)


def kernel(x, meta_tokens, lb_table, norm_mix_g, w_in, w_gla_gate_up, b_gla_gate, hgrn_norm_g,
           gla_norm_g, w_out, norm_ffn_g, w_ffn_in, w_ffn_out, norm_final_g):
    batch, seq, _ = x.shape
    assert w_in.shape[0] == 1 and lb_table.shape[0] == 2, "single-layer block"
    assert seq % ROWS_MIXER == 0
    x2d = x.reshape(batch * seq, D_MODEL)
    g_mix = norm_mix_g[0][None, :]
    w_up = w_gla_gate_up[0]
    b_up = b_gla_gate[0][None, :]

    w_main, w_ga, meta_ops = _meta(meta_tokens, g_mix, w_in[0].T, w_up, b_up, lb_table)
    ops, gates, dec, w_out_b, w1_b, w2_b = _inproj(
        x2d, g_mix, w_main, w_ga, w_up, b_up, lb_table, w_out[0], w_ffn_in[0], w_ffn_out[0],
        ROWS_INPROJ)
    o = _mixer(ops, gates, dec, meta_ops, hgrn_norm_g[0][None, :], gla_norm_g[0][None, :],
               seq, ROWS_MIXER)
    y = _ffn(x2d, o, w_out_b, norm_ffn_g[0][None, :], w1_b, w2_b, norm_final_g[None, :],
             ROWS_FFN)
    return y.reshape(batch, seq, D_MODEL)
```

```python
import functools

import jax
import jax.numpy as jnp
from jax import lax
from jax.experimental import pallas as pl
from jax.experimental.pallas import tpu as pltpu

F32 = jnp.float32
BF16 = jnp.bfloat16

D_MODEL = 1024
N_META = 16
CHUNK = 128
HEADS = 4
HEAD_V = 128
HGRN_WIDTH = HEADS * HEAD_V
GLA_K = 64
GLA_QK_WIDTH = HEADS * GLA_K
GLA_WIDTH = HEADS * HEAD_V
MIX_WIDTH = HGRN_WIDTH + GLA_WIDTH
GATE_RANK = 16
GATE_NORMALIZER = 16.0
FFN_HIDDEN = 2816
NORM_EPS = 1e-6
LANES = 128
N_STATES = 2 * HEADS

C_HQ, C_HF, C_HI, C_HG = 0, 512, 1024, 1536
C_GQ, C_GK, C_GV, C_GG = 2048, 2304, 2560, 3072
C_GA = 3584
IN_COLS = C_GA + GATE_RANK
W_BLOCK = 512

KM_H, KM_G = 0, HGRN_WIDTH
KE_H, KE_G = HGRN_WIDTH + GLA_QK_WIDTH, 2 * HGRN_WIDTH + GLA_QK_WIDTH
KN_COLS = KE_G + GLA_WIDTH
DEC_COLS = HGRN_WIDTH + GLA_QK_WIDTH

HEAD_TABLE = tuple(
    [(KM_H + LANES * h, KE_H + LANES * h, LANES * h) for h in range(HEADS)]
    + [(KM_G + LANES * (h // 2), KE_G + LANES * h, HGRN_WIDTH + LANES * (h // 2))
       for h in range(HEADS)])

V7X_VMEM_LIMIT = 56 * 1024 * 1024

ROWS_INPROJ = 512
ROWS_MIXER = 1024
ROWS_FFN = 512


def _rmsnorm(x, g):
    return x * lax.rsqrt(jnp.mean(x * x, axis=-1, keepdims=True) + NORM_EPS) * g


def _dot(a, b):
    return jnp.dot(a.astype(BF16), b.astype(BF16), preferred_element_type=F32)


def _dot_nt(a, b):
    return lax.dot_general(a.astype(BF16), b.astype(BF16), (((1,), (1,)), ((), ())),
                           preferred_element_type=F32)


def _dot_tn(a, b):
    return lax.dot_general(a.astype(BF16), b.astype(BF16), (((0,), (0,)), ((), ())),
                           preferred_element_type=F32)


def _silu(x):
    return x * jax.nn.sigmoid(x)


def _log_sigmoid(x):
    return jnp.minimum(x, 0.0) - jnp.log1p(jnp.exp(-jnp.abs(x)))


def _iota2(n, axis):
    return lax.broadcasted_iota(jnp.int32, (n, n), axis)


def _cumsum_rows(g, tri_bf16):
    g_hi = g.astype(BF16)
    g_lo = (g - g_hi.astype(F32)).astype(BF16)
    return (jnp.dot(tri_bf16, g_hi, preferred_element_type=F32)
            + jnp.dot(tri_bf16, g_lo, preferred_element_type=F32))


def _own_half(x, h):
    lane = lax.broadcasted_iota(jnp.int32, x.shape, 1)
    return jnp.where((lane >= GLA_K) == bool(h % 2), x, 0.0)


def _project(x_ref, g_ref, w_ref, wga_ref, wup_ref, bup_ref, lbt_ref, kn_ref, qt_ref, vt_ref,
             gate_ref, dec_ref, chunk, transposed):
    n_chunks = x_ref.shape[0] // chunk
    u = _rmsnorm(x_ref[...], g_ref[...]).astype(BF16)
    tri = (_iota2(chunk, 0) >= _iota2(chunk, 1)).astype(BF16)
    mid = chunk // 2 - 1

    def proj(col, width):
        blk, off = divmod(col, W_BLOCK)
        assert off + width <= W_BLOCK
        return jnp.dot(u, w_ref[blk, :, off:off + width], preferred_element_type=F32)

    def decay_factors(g, c):
        b = _cumsum_rows(g[c * chunk:(c + 1) * chunk], tri)
        b_mid = b[mid:mid + 1, :]
        b_last = b[chunk - 1:, :]
        return (jnp.exp(b - b_mid), jnp.exp(b_mid - b), jnp.exp(b_last - b),
                jnp.exp(b_mid), jnp.exp(b_last))

    def put_tile(ref, c, i, tile):
        if transposed:
            ref[c, LANES * i:LANES * (i + 1), :] = tile.T
        else:
            ref[c, :, LANES * i:LANES * (i + 1)] = tile

    t0 = lbt_ref[0:1, :]
    t1 = lbt_ref[1:2, :]
    m = jnp.maximum(t0, t1)
    e0 = jnp.exp(t0 - m)
    lb = e0 / (e0 + jnp.exp(t1 - m))

    f = lb + (1.0 - lb) * jax.nn.sigmoid(proj(C_HF, HGRN_WIDTH))
    g = jnp.log(f)
    k = 1.0 - f
    q = _silu(proj(C_HQ, HGRN_WIDTH))
    for c in range(n_chunks):
        rows = slice(c * chunk, (c + 1) * chunk)
        e_q, e_k, e_end, e_mid, e_last = decay_factors(g, c)
        kn_ref[rows, KM_H:KM_H + HGRN_WIDTH] = (k[rows] * e_k).astype(BF16)
        kn_ref[rows, KE_H:KE_H + HGRN_WIDTH] = (k[rows] * e_end).astype(BF16)
        qm = (q[rows] * e_q).astype(BF16)
        for h in range(HEADS):
            put_tile(qt_ref, c, h, qm[:, LANES * h:LANES * (h + 1)])
        dec_ref[c, :, 0:HGRN_WIDTH] = e_mid
        dec_ref[c, :, DEC_COLS:DEC_COLS + HGRN_WIDTH] = e_last
    v = proj(C_HI, HGRN_WIDTH).astype(BF16)
    for c in range(n_chunks):
        for h in range(HEADS):
            put_tile(vt_ref, c, h, v[c * chunk:(c + 1) * chunk, LANES * h:LANES * (h + 1)])
    gate_ref[:, 0:HGRN_WIDTH] = _silu(proj(C_HG, HGRN_WIDTH))

    logits = _dot(_dot_nt(u, wga_ref[...]), wup_ref[...]) + bup_ref[...]
    g = _log_sigmoid(logits) / GATE_NORMALIZER
    q = proj(C_GQ, GLA_QK_WIDTH) * (GLA_K ** -0.5)
    k = proj(C_GK, GLA_QK_WIDTH)
    for c in range(n_chunks):
        rows = slice(c * chunk, (c + 1) * chunk)
        e_q, e_k, e_end, e_mid, e_last = decay_factors(g, c)
        qm = q[rows] * e_q
        ke = k[rows] * e_end
        kn_ref[rows, KM_G:KM_G + GLA_QK_WIDTH] = (k[rows] * e_k).astype(BF16)
        for h in range(HEADS):
            pair = slice(LANES * (h // 2), LANES * (h // 2 + 1))
            put_tile(qt_ref, c, HEADS + h, _own_half(qm[:, pair], h).astype(BF16))
            kn_ref[rows, KE_G + LANES * h:KE_G + LANES * (h + 1)] = (
                _own_half(ke[:, pair], h).astype(BF16))
        dec_ref[c, :, HGRN_WIDTH:DEC_COLS] = e_mid
        dec_ref[c, :, DEC_COLS + HGRN_WIDTH:2 * DEC_COLS] = e_last
    v = proj(C_GV, GLA_WIDTH).astype(BF16)
    for c in range(n_chunks):
        for h in range(HEADS):
            put_tile(vt_ref, c, HEADS + h,
                     v[c * chunk:(c + 1) * chunk, LANES * h:LANES * (h + 1)])
    gate_ref[:, HGRN_WIDTH:MIX_WIDTH] = _silu(proj(C_GG, GLA_WIDTH))


def _meta_kernel(x_ref, g_ref, wt_ref, wgat_ref, wup_ref, bup_ref, lbt_ref,
                 w_ref, wga_ref, kn_ref, q_ref, v_ref, gate_ref, dec_ref, w_scr):
    j = pl.program_id(0)
    n_blocks = w_scr.shape[0]

    @pl.when(j < n_blocks)
    def _():
        blk = wt_ref[...].T.astype(BF16)
        w_ref[0] = blk
        w_scr[j] = blk

    @pl.when(j == n_blocks)
    def _():
        wga_ref[...] = wgat_ref[...].astype(BF16)
        _project(x_ref, g_ref, w_scr, wga_ref, wup_ref, bup_ref, lbt_ref, kn_ref, q_ref, v_ref,
                 gate_ref, dec_ref, N_META, transposed=False)


def _meta(meta_tokens, g, w_in_t, w_up, b_up, lb_table):
    n_blocks = C_GA // W_BLOCK
    full = lambda a: pl.BlockSpec(a.shape, lambda j: (0,) * len(a.shape))
    last = n_blocks - 1
    out_shapes = [jax.ShapeDtypeStruct((n_blocks, D_MODEL, W_BLOCK), BF16),
                  jax.ShapeDtypeStruct((GATE_RANK, D_MODEL), BF16),
                  jax.ShapeDtypeStruct((N_META, KN_COLS), BF16),
                  jax.ShapeDtypeStruct((1, N_META, MIX_WIDTH), BF16),
                  jax.ShapeDtypeStruct((1, N_META, MIX_WIDTH), BF16),
                  jax.ShapeDtypeStruct((N_META, MIX_WIDTH), F32),
                  jax.ShapeDtypeStruct((1, 1, 2 * DEC_COLS), F32)]
    w_main, w_ga, kn, _, v, _, _ = pl.pallas_call(
        _meta_kernel,
        grid=(n_blocks + 1,),
        in_specs=[
            full(meta_tokens), full(g),
            pl.BlockSpec((W_BLOCK, D_MODEL), lambda j: (jnp.minimum(j, last), 0)),
            pl.BlockSpec((GATE_RANK, D_MODEL), lambda j: (C_GA // GATE_RANK, 0)),
            full(w_up), full(b_up), full(lb_table),
        ],
        out_specs=[pl.BlockSpec((1, D_MODEL, W_BLOCK), lambda j: (jnp.minimum(j, last), 0, 0))]
        + [full(s) for s in out_shapes[1:]],
        out_shape=out_shapes,
        scratch_shapes=[pltpu.VMEM((n_blocks, D_MODEL, W_BLOCK), BF16)],
        compiler_params=pltpu.CompilerParams(
            dimension_semantics=("arbitrary",), vmem_limit_bytes=V7X_VMEM_LIMIT),
        name="meta",
    )(meta_tokens, g, w_in_t, w_in_t, w_up, b_up, lb_table)
    return w_main, w_ga, kn, v


def _inproj_kernel(x_ref, g_ref, w_ref, wga_ref, wup_ref, bup_ref, lbt_ref, wout32_ref, w1_32_ref,
                   w2_32_ref, kn_ref, qt_ref, vt_ref, gate_ref, dec_ref, wout_ref, w1_ref, w2_ref):
    _project(x_ref, g_ref, w_ref, wga_ref, wup_ref, bup_ref, lbt_ref, kn_ref, qt_ref, vt_ref,
             gate_ref, dec_ref, CHUNK, transposed=True)
    wout_ref[...] = wout32_ref[...].astype(BF16)
    w1_ref[...] = w1_32_ref[...].astype(BF16)
    w2_ref[...] = w2_32_ref[...].astype(BF16)


def _inproj(x2d, g, w_main, w_ga, w_up, b_up, lb_table, w_out, w1, w2, rows):
    n = x2d.shape[0]
    steps = n // rows
    chunks = rows // CHUNK
    wout_rows = w_out.shape[0] // steps
    w1_rows = w1.shape[0] // steps
    w2_rows = 2 * w2.shape[0] // steps
    assert wout_rows * steps == w_out.shape[0] and w1_rows * steps == w1.shape[0]
    assert w2_rows * steps == 2 * w2.shape[0]
    const = lambda i: (0, 0)
    tile = lambda i: (i, 0)
    tile3 = lambda i: (i, 0, 0)
    half_pace = lambda i: (i // 2, 0)
    w_specs = [pl.BlockSpec((wout_rows, w_out.shape[1]), tile),
               pl.BlockSpec((w1_rows, w1.shape[1]), tile),
               pl.BlockSpec((w2_rows, w2.shape[1]), half_pace)]
    return pl.pallas_call(
        _inproj_kernel,
        grid=(steps,),
        in_specs=[
            pl.BlockSpec((rows, D_MODEL), tile),
            pl.BlockSpec((1, D_MODEL), const),
            pl.BlockSpec(w_main.shape, lambda i: (0, 0, 0), pipeline_mode=pl.Buffered(1)),
            pl.BlockSpec(w_ga.shape, const),
            pl.BlockSpec((GATE_RANK, GLA_QK_WIDTH), const),
            pl.BlockSpec((1, GLA_QK_WIDTH), const),
            pl.BlockSpec(lb_table.shape, const),
        ] + w_specs,
        out_specs=[
            pl.BlockSpec((rows, KN_COLS), tile),
            pl.BlockSpec((chunks, MIX_WIDTH, CHUNK), tile3),
            pl.BlockSpec((chunks, MIX_WIDTH, CHUNK), tile3),
            pl.BlockSpec((rows, MIX_WIDTH), tile),
            pl.BlockSpec((chunks, 1, 2 * DEC_COLS), tile3),
        ] + w_specs,
        out_shape=[
            jax.ShapeDtypeStruct((n, KN_COLS), BF16),
            jax.ShapeDtypeStruct((n // CHUNK, MIX_WIDTH, CHUNK), BF16),
            jax.ShapeDtypeStruct((n // CHUNK, MIX_WIDTH, CHUNK), BF16),
            jax.ShapeDtypeStruct((n, MIX_WIDTH), F32),
            jax.ShapeDtypeStruct((n // CHUNK, 1, 2 * DEC_COLS), F32),
            jax.ShapeDtypeStruct(w_out.shape, BF16),
            jax.ShapeDtypeStruct(w1.shape, BF16),
            jax.ShapeDtypeStruct(w2.shape, BF16),
        ],
        compiler_params=pltpu.CompilerParams(
            dimension_semantics=("arbitrary",), vmem_limit_bytes=V7X_VMEM_LIMIT),
        name="inproj",
    )(x2d, g, w_main, w_ga, w_up, b_up, lb_table, w_out, w1, w2)


def _mixer_kernel(kn_ref, qt_ref, vt_ref, gate_ref, dec_ref, mkn_ref, mv_ref, hnorm_ref,
                  gnorm_ref, o_ref, st_ref, *, steps_per_seq):
    @pl.when(lax.rem(pl.program_id(0), steps_per_seq) == 0)
    def _():
        for i, (_, ke_col, _) in enumerate(HEAD_TABLE):
            st_ref[i] = _dot_tn(mv_ref[0, :, LANES * i:LANES * (i + 1)],
                                mkn_ref[:, ke_col:ke_col + LANES])

    key_before_query = _iota2(CHUNK, 0) <= _iota2(CHUNK, 1)
    norms = (hnorm_ref[...],) * HEADS + (gnorm_ref[...],) * HEADS

    for c in range(kn_ref.shape[0] // CHUNK):
        rows = slice(c * CHUNK, (c + 1) * CHUNK)
        dec = dec_ref[c]
        heads = lambda ref: [ref[c, LANES * i:LANES * (i + 1), :] for i in range(N_STATES)]
        first, outs = [], []
        for i, (km_col, _, dec_col) in enumerate(HEAD_TABLE):
            s_mid = st_ref[i] * dec[:, dec_col:dec_col + LANES]
            lhs = jnp.concatenate([kn_ref[rows, km_col:km_col + LANES], s_mid.astype(BF16)],
                                  axis=0)
            first.append(jnp.dot(lhs, qt_ref[c, LANES * i:LANES * (i + 1), :],
                                 preferred_element_type=F32))
        for i in range(N_STATES):
            scores_t = jnp.where(key_before_query, first[i][:CHUNK], 0.0).astype(BF16)
            outs.append(first[i][CHUNK:] + jnp.dot(vt_ref[c, LANES * i:LANES * (i + 1), :],
                                                   scores_t, preferred_element_type=F32))
        for i, (_, ke_col, dec_col) in enumerate(HEAD_TABLE):
            st_ref[i] = (st_ref[i] * dec[:, DEC_COLS + dec_col:DEC_COLS + dec_col + LANES]
                         + jnp.dot(vt_ref[c, LANES * i:LANES * (i + 1), :],
                                   kn_ref[rows, ke_col:ke_col + LANES],
                                   preferred_element_type=F32))
        for i in range(N_STATES):
            o_t = outs[i]
            o_t = o_t * lax.rsqrt(jnp.mean(o_t * o_t, axis=0, keepdims=True) + NORM_EPS)
            o = o_t.T * norms[i] * gate_ref[rows, LANES * i:LANES * (i + 1)]
            o_ref[rows, LANES * i:LANES * (i + 1)] = o.astype(o_ref.dtype)


def _mixer(kn, qt, vt, gates, dec, meta_kn, meta_v, hnorm, gnorm, seq, rows):
    n = kn.shape[0]
    chunks = rows // CHUNK
    const = lambda s: (0, 0)
    tile = lambda s: (s, 0)
    tile3 = lambda s: (s, 0, 0)
    return pl.pallas_call(
        functools.partial(_mixer_kernel, steps_per_seq=seq // rows),
        grid=(n // rows,),
        in_specs=[
            pl.BlockSpec((rows, KN_COLS), tile),
            pl.BlockSpec((chunks, MIX_WIDTH, CHUNK), tile3),
            pl.BlockSpec((chunks, MIX_WIDTH, CHUNK), tile3),
            pl.BlockSpec((rows, MIX_WIDTH), tile),
            pl.BlockSpec((chunks, 1, 2 * DEC_COLS), tile3),
            pl.BlockSpec(meta_kn.shape, const),
            pl.BlockSpec(meta_v.shape, lambda s: (0, 0, 0)),
            pl.BlockSpec((1, HEAD_V), const),
            pl.BlockSpec((1, HEAD_V), const),
        ],
        out_specs=pl.BlockSpec((rows, MIX_WIDTH), tile),
        out_shape=jax.ShapeDtypeStruct((n, MIX_WIDTH), BF16),
        scratch_shapes=[pltpu.VMEM((N_STATES, HEAD_V, LANES), F32)],
        compiler_params=pltpu.CompilerParams(
            dimension_semantics=("arbitrary",), vmem_limit_bytes=V7X_VMEM_LIMIT),
        name="mixer",
    )(kn, qt, vt, gates, dec, meta_kn, meta_v, hnorm, gnorm)


def _ffn_kernel(x_ref, o_ref, wout_ref, g2_ref, w1_ref, w2_ref, g3_ref, y_ref):
    h = x_ref[...] + jnp.dot(o_ref[...], wout_ref[...], preferred_element_type=F32)
    u = _rmsnorm(h, g2_ref[...]).astype(BF16)
    gate = jnp.dot(u, w1_ref[:, :FFN_HIDDEN], preferred_element_type=F32)
    up = jnp.dot(u, w1_ref[:, FFN_HIDDEN:], preferred_element_type=F32)
    h = h + _dot(_silu(gate) * up, w2_ref[...])
    y_ref[...] = _rmsnorm(h, g3_ref[...])


def _ffn(x2d, o, w_out, g2, w1, w2, g3, rows):
    n = x2d.shape[0]
    const = lambda i: (0, 0)
    tile = lambda i: (i, 0)
    resident = functools.partial(pl.BlockSpec, index_map=const, pipeline_mode=pl.Buffered(1))
    return pl.pallas_call(
        _ffn_kernel,
        grid=(n // rows,),
        in_specs=[
            pl.BlockSpec((rows, D_MODEL), tile),
            pl.BlockSpec((rows, D_MODEL), tile),
            resident((D_MODEL, D_MODEL)),
            pl.BlockSpec((1, D_MODEL), const),
            resident((D_MODEL, 2 * FFN_HIDDEN)),
            resident((FFN_HIDDEN, D_MODEL)),
            pl.BlockSpec((1, D_MODEL), const),
        ],
        out_specs=pl.BlockSpec((rows, D_MODEL), tile),
        out_shape=jax.ShapeDtypeStruct((n, D_MODEL), F32),
        compiler_params=pltpu.CompilerParams(
            dimension_semantics=("arbitrary",), vmem_limit_bytes=V7X_VMEM_LIMIT),
        name="outproj_ffn",
    )(x2d, o, w_out, g2, w1, w2, g3)


def kernel(x, meta_tokens, lb_table, norm_mix_g, w_in, w_gla_gate_up, b_gla_gate, hgrn_norm_g,
           gla_norm_g, w_out, norm_ffn_g, w_ffn_in, w_ffn_out, norm_final_g):
    batch, seq, _ = x.shape
    assert w_in.shape[0] == 1 and lb_table.shape[0] == 2, "single-layer block"
    assert seq % ROWS_MIXER == 0
    x2d = x.reshape(batch * seq, D_MODEL)
    g_mix = norm_mix_g[0][None, :]
    w_up = w_gla_gate_up[0]
    b_up = b_gla_gate[0][None, :]

    w_main, w_ga, meta_kn, meta_v = _meta(meta_tokens, g_mix, w_in[0].T, w_up, b_up, lb_table)
    kn, qt, vt, gates, dec, w_out_b, w1_b, w2_b = _inproj(
        x2d, g_mix, w_main, w_ga, w_up, b_up, lb_table, w_out[0], w_ffn_in[0], w_ffn_out[0],
        ROWS_INPROJ)
    o = _mixer(kn, qt, vt, gates, dec, meta_kn, meta_v, hgrn_norm_g[0][None, :],
               gla_norm_g[0][None, :], seq, ROWS_MIXER)
    y = _ffn(x2d, o, w_out_b, norm_ffn_g[0][None, :], w1_b, w2_b, norm_final_g[None, :],
             ROWS_FFN)
    return y.reshape(batch, seq, D_MODEL)
```

```python
import functools

import jax
import jax.numpy as jnp
from jax import lax
from jax.experimental import pallas as pl
from jax.experimental.pallas import tpu as pltpu

F32 = jnp.float32
BF16 = jnp.bfloat16

D_MODEL = 1024
N_META = 16
CHUNK = 128
HEADS = 4
HEAD_V = 128
HGRN_WIDTH = HEADS * HEAD_V
GLA_K = 64
GLA_QK_WIDTH = HEADS * GLA_K
GLA_WIDTH = HEADS * HEAD_V
MIX_WIDTH = HGRN_WIDTH + GLA_WIDTH
GATE_RANK = 16
GATE_NORMALIZER = 16.0
FFN_HIDDEN = 2816
NORM_EPS = 1e-6
LANES = 128
N_STATES = 2 * HEADS

C_HQ, C_HF, C_HI, C_HG = 0, 512, 1024, 1536
C_GQ, C_GK, C_GV, C_GG = 2048, 2304, 2560, 3072
C_GA = 3584
IN_COLS = C_GA + GATE_RANK
W_BLOCK = 512

KM_H, KM_G = 0, HGRN_WIDTH
KE_H, KE_G = HGRN_WIDTH + GLA_QK_WIDTH, 2 * HGRN_WIDTH + GLA_QK_WIDTH
KN_COLS = KE_G + GLA_WIDTH
DEC_COLS = HGRN_WIDTH + GLA_QK_WIDTH

HEAD_TABLE = tuple(
    [(KM_H + LANES * h, KE_H + LANES * h, LANES * h) for h in range(HEADS)]
    + [(KM_G + LANES * (h // 2), KE_G + LANES * h, HGRN_WIDTH + LANES * (h // 2))
       for h in range(HEADS)])

V7X_VMEM_LIMIT = 56 * 1024 * 1024

ROWS_INPROJ = 512
ROWS_MIXER = 1024
ROWS_FFN = 512


def _rmsnorm(x, g):
    return x * lax.rsqrt(jnp.mean(x * x, axis=-1, keepdims=True) + NORM_EPS) * g


def _dot(a, b):
    return jnp.dot(a.astype(BF16), b.astype(BF16), preferred_element_type=F32)


def _dot_nt(a, b):
    return lax.dot_general(a.astype(BF16), b.astype(BF16), (((1,), (1,)), ((), ())),
                           preferred_element_type=F32)


def _dot_tn(a, b):
    return lax.dot_general(a.astype(BF16), b.astype(BF16), (((0,), (0,)), ((), ())),
                           preferred_element_type=F32)


def _silu(x):
    return x * jax.nn.sigmoid(x)


def _log_sigmoid(x):
    return jnp.minimum(x, 0.0) - jnp.log1p(jnp.exp(-jnp.abs(x)))


def _iota2(n, axis):
    return lax.broadcasted_iota(jnp.int32, (n, n), axis)


def _cumsum_rows(g, tri_bf16):
    g_hi = g.astype(BF16)
    g_lo = (g - g_hi.astype(F32)).astype(BF16)
    return (jnp.dot(tri_bf16, g_hi, preferred_element_type=F32)
            + jnp.dot(tri_bf16, g_lo, preferred_element_type=F32))


def _own_half(x, h):
    lane = lax.broadcasted_iota(jnp.int32, x.shape, 1)
    return jnp.where((lane >= GLA_K) == bool(h % 2), x, 0.0)


def _project(x_ref, g_ref, w_ref, wga_ref, wup_ref, bup_ref, lbt_ref, kn_ref, qt_ref, vt_ref,
             gate_ref, dec_ref, chunk, transposed):
    n_chunks = x_ref.shape[0] // chunk
    u = _rmsnorm(x_ref[...], g_ref[...]).astype(BF16)
    tri = (_iota2(chunk, 0) >= _iota2(chunk, 1)).astype(BF16)
    mid = chunk // 2 - 1

    def proj(col, width):
        blk, off = divmod(col, W_BLOCK)
        assert off + width <= W_BLOCK
        return jnp.dot(u, w_ref[blk, :, off:off + width], preferred_element_type=F32)

    def decay_factors(g, c):
        b = _cumsum_rows(g[c * chunk:(c + 1) * chunk], tri)
        b_mid = b[mid:mid + 1, :]
        b_last = b[chunk - 1:, :]
        return (jnp.exp(b - b_mid), jnp.exp(b_mid - b), jnp.exp(b_last - b),
                jnp.exp(b_mid), jnp.exp(b_last))

    def put_tile(ref, c, i, tile):
        if transposed:
            ref[c, LANES * i:LANES * (i + 1), :] = tile.T
        else:
            ref[c, :, LANES * i:LANES * (i + 1)] = tile

    t0 = lbt_ref[0:1, :]
    t1 = lbt_ref[1:2, :]
    m = jnp.maximum(t0, t1)
    e0 = jnp.exp(t0 - m)
    lb = e0 / (e0 + jnp.exp(t1 - m))

    f = lb + (1.0 - lb) * jax.nn.sigmoid(proj(C_HF, HGRN_WIDTH))
    g = jnp.log(f)
    k = 1.0 - f
    q = _silu(proj(C_HQ, HGRN_WIDTH))
    for c in range(n_chunks):
        rows = slice(c * chunk, (c + 1) * chunk)
        e_q, e_k, e_end, e_mid, e_last = decay_factors(g, c)
        kn_ref[rows, KM_H:KM_H + HGRN_WIDTH] = (k[rows] * e_k).astype(BF16)
        kn_ref[rows, KE_H:KE_H + HGRN_WIDTH] = (k[rows] * e_end).astype(BF16)
        qm = (q[rows] * e_q).astype(BF16)
        for h in range(HEADS):
            put_tile(qt_ref, c, h, qm[:, LANES * h:LANES * (h + 1)])
        dec_ref[c, :, 0:HGRN_WIDTH] = e_mid
        dec_ref[c, :, DEC_COLS:DEC_COLS + HGRN_WIDTH] = e_last
    v = proj(C_HI, HGRN_WIDTH).astype(BF16)
    for c in range(n_chunks):
        for h in range(HEADS):
            put_tile(vt_ref, c, h, v[c * chunk:(c + 1) * chunk, LANES * h:LANES * (h + 1)])
    gate_ref[:, 0:HGRN_WIDTH] = proj(C_HG, HGRN_WIDTH)

    logits = _dot(_dot_nt(u, wga_ref[...]), wup_ref[...]) + bup_ref[...]
    g = _log_sigmoid(logits) / GATE_NORMALIZER
    q = proj(C_GQ, GLA_QK_WIDTH) * (GLA_K ** -0.5)
    k = proj(C_GK, GLA_QK_WIDTH)
    for c in range(n_chunks):
        rows = slice(c * chunk, (c + 1) * chunk)
        e_q, e_k, e_end, e_mid, e_last = decay_factors(g, c)
        qm = q[rows] * e_q
        ke = k[rows] * e_end
        kn_ref[rows, KM_G:KM_G + GLA_QK_WIDTH] = (k[rows] * e_k).astype(BF16)
        for h in range(HEADS):
            pair = slice(LANES * (h // 2), LANES * (h // 2 + 1))
            put_tile(qt_ref, c, HEADS + h, _own_half(qm[:, pair], h).astype(BF16))
            kn_ref[rows, KE_G + LANES * h:KE_G + LANES * (h + 1)] = (
                _own_half(ke[:, pair], h).astype(BF16))
        dec_ref[c, :, HGRN_WIDTH:DEC_COLS] = e_mid
        dec_ref[c, :, DEC_COLS + HGRN_WIDTH:2 * DEC_COLS] = e_last
    v = proj(C_GV, GLA_WIDTH).astype(BF16)
    for c in range(n_chunks):
        for h in range(HEADS):
            put_tile(vt_ref, c, HEADS + h,
                     v[c * chunk:(c + 1) * chunk, LANES * h:LANES * (h + 1)])
    gate_ref[:, HGRN_WIDTH:MIX_WIDTH] = proj(C_GG, GLA_WIDTH)


def _meta_kernel(x_ref, g_ref, wt_ref, wgat_ref, wup_ref, bup_ref, lbt_ref,
                 w_ref, wga_ref, kn_ref, q_ref, v_ref, gate_ref, dec_ref, w_scr):
    j = pl.program_id(0)
    n_blocks = w_scr.shape[0]

    @pl.when(j < n_blocks)
    def _():
        blk = wt_ref[...].T.astype(BF16)
        w_ref[0] = blk
        w_scr[j] = blk

    @pl.when(j == n_blocks)
    def _():
        wga_ref[...] = wgat_ref[...].astype(BF16)
        _project(x_ref, g_ref, w_scr, wga_ref, wup_ref, bup_ref, lbt_ref, kn_ref, q_ref, v_ref,
                 gate_ref, dec_ref, N_META, transposed=False)


def _meta(meta_tokens, g, w_in_t, w_up, b_up, lb_table):
    n_blocks = C_GA // W_BLOCK
    full = lambda a: pl.BlockSpec(a.shape, lambda j: (0,) * len(a.shape))
    last = n_blocks - 1
    out_shapes = [jax.ShapeDtypeStruct((n_blocks, D_MODEL, W_BLOCK), BF16),
                  jax.ShapeDtypeStruct((GATE_RANK, D_MODEL), BF16),
                  jax.ShapeDtypeStruct((N_META, KN_COLS), BF16),
                  jax.ShapeDtypeStruct((1, N_META, MIX_WIDTH), BF16),
                  jax.ShapeDtypeStruct((1, N_META, MIX_WIDTH), BF16),
                  jax.ShapeDtypeStruct((N_META, MIX_WIDTH), F32),
                  jax.ShapeDtypeStruct((1, 1, 2 * DEC_COLS), F32)]
    w_main, w_ga, kn, _, v, _, _ = pl.pallas_call(
        _meta_kernel,
        grid=(n_blocks + 1,),
        in_specs=[
            full(meta_tokens), full(g),
            pl.BlockSpec((W_BLOCK, D_MODEL), lambda j: (jnp.minimum(j, last), 0)),
            pl.BlockSpec((GATE_RANK, D_MODEL), lambda j: (C_GA // GATE_RANK, 0)),
            full(w_up), full(b_up), full(lb_table),
        ],
        out_specs=[pl.BlockSpec((1, D_MODEL, W_BLOCK), lambda j: (jnp.minimum(j, last), 0, 0))]
        + [full(s) for s in out_shapes[1:]],
        out_shape=out_shapes,
        scratch_shapes=[pltpu.VMEM((n_blocks, D_MODEL, W_BLOCK), BF16)],
        compiler_params=pltpu.CompilerParams(
            dimension_semantics=("arbitrary",), vmem_limit_bytes=V7X_VMEM_LIMIT),
        name="meta",
    )(meta_tokens, g, w_in_t, w_in_t, w_up, b_up, lb_table)
    return w_main, w_ga, kn, v


def _inproj_kernel(x_ref, g_ref, w_ref, wga_ref, wup_ref, bup_ref, lbt_ref, wout32_ref, w1_32_ref,
                   w2_32_ref, kn_ref, qt_ref, vt_ref, gate_ref, dec_ref, wout_ref, w1_ref, w2_ref):
    _project(x_ref, g_ref, w_ref, wga_ref, wup_ref, bup_ref, lbt_ref, kn_ref, qt_ref, vt_ref,
             gate_ref, dec_ref, CHUNK, transposed=True)
    wout_ref[...] = wout32_ref[...].astype(BF16)
    w1_ref[...] = w1_32_ref[...].astype(BF16)
    w2_ref[...] = w2_32_ref[...].astype(BF16)


def _inproj(x2d, g, w_main, w_ga, w_up, b_up, lb_table, w_out, w1, w2, rows):
    n = x2d.shape[0]
    steps = n // rows
    chunks = rows // CHUNK
    wout_rows = w_out.shape[0] // steps
    w1_rows = w1.shape[0] // steps
    w2_rows = 2 * w2.shape[0] // steps
    assert wout_rows * steps == w_out.shape[0] and w1_rows * steps == w1.shape[0]
    assert w2_rows * steps == 2 * w2.shape[0]
    const = lambda i: (0, 0)
    tile = lambda i: (i, 0)
    tile3 = lambda i: (i, 0, 0)
    half_pace = lambda i: (i // 2, 0)
    w_specs = [pl.BlockSpec((wout_rows, w_out.shape[1]), tile),
               pl.BlockSpec((w1_rows, w1.shape[1]), tile),
               pl.BlockSpec((w2_rows, w2.shape[1]), half_pace)]
    return pl.pallas_call(
        _inproj_kernel,
        grid=(steps,),
        in_specs=[
            pl.BlockSpec((rows, D_MODEL), tile),
            pl.BlockSpec((1, D_MODEL), const),
            pl.BlockSpec(w_main.shape, lambda i: (0, 0, 0), pipeline_mode=pl.Buffered(1)),
            pl.BlockSpec(w_ga.shape, const),
            pl.BlockSpec((GATE_RANK, GLA_QK_WIDTH), const),
            pl.BlockSpec((1, GLA_QK_WIDTH), const),
            pl.BlockSpec(lb_table.shape, const),
        ] + w_specs,
        out_specs=[
            pl.BlockSpec((rows, KN_COLS), tile),
            pl.BlockSpec((chunks, MIX_WIDTH, CHUNK), tile3),
            pl.BlockSpec((chunks, MIX_WIDTH, CHUNK), tile3),
            pl.BlockSpec((rows, MIX_WIDTH), tile),
            pl.BlockSpec((chunks, 1, 2 * DEC_COLS), tile3),
        ] + w_specs,
        out_shape=[
            jax.ShapeDtypeStruct((n, KN_COLS), BF16),
            jax.ShapeDtypeStruct((n // CHUNK, MIX_WIDTH, CHUNK), BF16),
            jax.ShapeDtypeStruct((n // CHUNK, MIX_WIDTH, CHUNK), BF16),
            jax.ShapeDtypeStruct((n, MIX_WIDTH), F32),
            jax.ShapeDtypeStruct((n // CHUNK, 1, 2 * DEC_COLS), F32),
            jax.ShapeDtypeStruct(w_out.shape, BF16),
            jax.ShapeDtypeStruct(w1.shape, BF16),
            jax.ShapeDtypeStruct(w2.shape, BF16),
        ],
        compiler_params=pltpu.CompilerParams(
            dimension_semantics=("arbitrary",), vmem_limit_bytes=V7X_VMEM_LIMIT),
        name="inproj",
    )(x2d, g, w_main, w_ga, w_up, b_up, lb_table, w_out, w1, w2)


def _mixer_kernel(kn_ref, qt_ref, vt_ref, dec_ref, mkn_ref, mv_ref, hnorm_ref, gnorm_ref,
                  o_ref, st_ref, *, steps_per_seq):
    @pl.when(lax.rem(pl.program_id(0), steps_per_seq) == 0)
    def _():
        for i, (_, ke_col, _) in enumerate(HEAD_TABLE):
            st_ref[i] = _dot_tn(mv_ref[0, :, LANES * i:LANES * (i + 1)],
                                mkn_ref[:, ke_col:ke_col + LANES])

    key_before_query = _iota2(CHUNK, 0) <= _iota2(CHUNK, 1)
    norms = (hnorm_ref[...],) * HEADS + (gnorm_ref[...],) * HEADS

    for c in range(kn_ref.shape[0] // CHUNK):
        rows = slice(c * CHUNK, (c + 1) * CHUNK)
        dec = dec_ref[c]
        first, outs = [], []
        for i, (km_col, _, dec_col) in enumerate(HEAD_TABLE):
            s_mid = st_ref[i] * dec[:, dec_col:dec_col + LANES]
            lhs = jnp.concatenate([kn_ref[rows, km_col:km_col + LANES], s_mid.astype(BF16)],
                                  axis=0)
            first.append(jnp.dot(lhs, qt_ref[c, LANES * i:LANES * (i + 1), :],
                                 preferred_element_type=F32))
        for i in range(N_STATES):
            scores_t = jnp.where(key_before_query, first[i][:CHUNK], 0.0).astype(BF16)
            outs.append(first[i][CHUNK:] + jnp.dot(vt_ref[c, LANES * i:LANES * (i + 1), :],
                                                   scores_t, preferred_element_type=F32))
        for i, (_, ke_col, dec_col) in enumerate(HEAD_TABLE):
            st_ref[i] = (st_ref[i] * dec[:, DEC_COLS + dec_col:DEC_COLS + dec_col + LANES]
                         + jnp.dot(vt_ref[c, LANES * i:LANES * (i + 1), :],
                                   kn_ref[rows, ke_col:ke_col + LANES],
                                   preferred_element_type=F32))
        for i in range(N_STATES):
            o_t = outs[i]
            o_t = o_t * lax.rsqrt(jnp.mean(o_t * o_t, axis=0, keepdims=True) + NORM_EPS)
            o_ref[rows, LANES * i:LANES * (i + 1)] = (o_t.T * norms[i]).astype(o_ref.dtype)


def _mixer(kn, qt, vt, dec, meta_kn, meta_v, hnorm, gnorm, seq, rows):
    n = kn.shape[0]
    chunks = rows // CHUNK
    const = lambda s: (0, 0)
    tile = lambda s: (s, 0)
    tile3 = lambda s: (s, 0, 0)
    return pl.pallas_call(
        functools.partial(_mixer_kernel, steps_per_seq=seq // rows),
        grid=(n // rows,),
        in_specs=[
            pl.BlockSpec((rows, KN_COLS), tile),
            pl.BlockSpec((chunks, MIX_WIDTH, CHUNK), tile3),
            pl.BlockSpec((chunks, MIX_WIDTH, CHUNK), tile3),
            pl.BlockSpec((chunks, 1, 2 * DEC_COLS), tile3),
            pl.BlockSpec(meta_kn.shape, const),
            pl.BlockSpec(meta_v.shape, lambda s: (0, 0, 0)),
            pl.BlockSpec((1, HEAD_V), const),
            pl.BlockSpec((1, HEAD_V), const),
        ],
        out_specs=pl.BlockSpec((rows, MIX_WIDTH), tile),
        out_shape=jax.ShapeDtypeStruct((n, MIX_WIDTH), BF16),
        scratch_shapes=[pltpu.VMEM((N_STATES, HEAD_V, LANES), F32)],
        compiler_params=pltpu.CompilerParams(
            dimension_semantics=("arbitrary",), vmem_limit_bytes=V7X_VMEM_LIMIT),
        name="mixer",
    )(kn, qt, vt, dec, meta_kn, meta_v, hnorm, gnorm)


def _ffn_kernel(x_ref, o_ref, gate_ref, wout_ref, g2_ref, w1_ref, w2_ref, g3_ref, y_ref):
    o = o_ref[...].astype(F32) * _silu(gate_ref[...])
    h = x_ref[...] + _dot(o, wout_ref[...])
    u = _rmsnorm(h, g2_ref[...]).astype(BF16)
    gate = jnp.dot(u, w1_ref[:, :FFN_HIDDEN], preferred_element_type=F32)
    up = jnp.dot(u, w1_ref[:, FFN_HIDDEN:], preferred_element_type=F32)
    h = h + _dot(_silu(gate) * up, w2_ref[...])
    y_ref[...] = _rmsnorm(h, g3_ref[...])


def _ffn(x2d, o, gates, w_out, g2, w1, w2, g3, rows):
    n = x2d.shape[0]
    const = lambda i: (0, 0)
    tile = lambda i: (i, 0)
    resident = functools.partial(pl.BlockSpec, index_map=const, pipeline_mode=pl.Buffered(1))
    return pl.pallas_call(
        _ffn_kernel,
        grid=(n // rows,),
        in_specs=[
            pl.BlockSpec((rows, D_MODEL), tile),
            pl.BlockSpec((rows, MIX_WIDTH), tile),
            pl.BlockSpec((rows, MIX_WIDTH), tile),
            resident((MIX_WIDTH, D_MODEL)),
            pl.BlockSpec((1, D_MODEL), const),
            resident((D_MODEL, 2 * FFN_HIDDEN)),
            resident((FFN_HIDDEN, D_MODEL)),
            pl.BlockSpec((1, D_MODEL), const),
        ],
        out_specs=pl.BlockSpec((rows, D_MODEL), tile),
        out_shape=jax.ShapeDtypeStruct((n, D_MODEL), F32),
        compiler_params=pltpu.CompilerParams(
            dimension_semantics=("arbitrary",), vmem_limit_bytes=V7X_VMEM_LIMIT),
        name="outproj_ffn",
    )(x2d, o, gates, w_out, g2, w1, w2, g3)


def kernel(x, meta_tokens, lb_table, norm_mix_g, w_in, w_gla_gate_up, b_gla_gate, hgrn_norm_g,
           gla_norm_g, w_out, norm_ffn_g, w_ffn_in, w_ffn_out, norm_final_g):
    batch, seq, _ = x.shape
    assert w_in.shape[0] == 1 and lb_table.shape[0] == 2, "single-layer block"
    assert seq % ROWS_MIXER == 0
    x2d = x.reshape(batch * seq, D_MODEL)
    g_mix = norm_mix_g[0][None, :]
    w_up = w_gla_gate_up[0]
    b_up = b_gla_gate[0][None, :]

    w_main, w_ga, meta_kn, meta_v = _meta(meta_tokens, g_mix, w_in[0].T, w_up, b_up, lb_table)
    kn, qt, vt, gates, dec, w_out_b, w1_b, w2_b = _inproj(
        x2d, g_mix, w_main, w_ga, w_up, b_up, lb_table, w_out[0], w_ffn_in[0], w_ffn_out[0],
        ROWS_INPROJ)
    o = _mixer(kn, qt, vt, dec, meta_kn, meta_v, hgrn_norm_g[0][None, :],
               gla_norm_g[0][None, :], seq, ROWS_MIXER)
    y = _ffn(x2d, o, gates, w_out_b, norm_ffn_g[0][None, :], w1_b, w2_b, norm_final_g[None, :],
             ROWS_FFN)
    return y.reshape(batch, seq, D_MODEL)
```

```python
import functools

import jax
import jax.numpy as jnp
from jax import lax
from jax.experimental import pallas as pl
from jax.experimental.pallas import tpu as pltpu

F32 = jnp.float32
BF16 = jnp.bfloat16

D_MODEL = 1024
N_META = 16
CHUNK = 128
HEADS = 4
HEAD_V = 128
HGRN_WIDTH = HEADS * HEAD_V
GLA_K = 64
GLA_QK_WIDTH = HEADS * GLA_K
GLA_WIDTH = HEADS * HEAD_V
MIX_WIDTH = HGRN_WIDTH + GLA_WIDTH
GATE_RANK = 16
GATE_NORMALIZER = 16.0
FFN_HIDDEN = 2816
NORM_EPS = 1e-6
LANES = 128
N_STATES = 2 * HEADS

C_HQ, C_HF, C_HI, C_HG = 0, 512, 1024, 1536
C_GQ, C_GK, C_GV, C_GG = 2048, 2304, 2560, 3072
C_GA = 3584
IN_COLS = C_GA + GATE_RANK
W_BLOCK = 512

KM_H, KM_G = 0, HGRN_WIDTH
KE_H, KE_G = HGRN_WIDTH + GLA_QK_WIDTH, 2 * HGRN_WIDTH + GLA_QK_WIDTH
KN_COLS = KE_G + GLA_WIDTH
DEC_COLS = HGRN_WIDTH + GLA_QK_WIDTH

HEAD_TABLE = tuple(
    [(KM_H + LANES * h, KE_H + LANES * h, LANES * h) for h in range(HEADS)]
    + [(KM_G + LANES * (h // 2), KE_G + LANES * h, HGRN_WIDTH + LANES * (h // 2))
       for h in range(HEADS)])

V7X_VMEM_LIMIT = 56 * 1024 * 1024

ROWS_INPROJ = 512
ROWS_MIXER = 1024
ROWS_FFN = 512


def _rmsnorm(x, g):
    return x * lax.rsqrt(jnp.mean(x * x, axis=-1, keepdims=True) + NORM_EPS) * g


def _dot(a, b):
    return jnp.dot(a.astype(BF16), b.astype(BF16), preferred_element_type=F32)


def _dot_nt(a, b):
    return lax.dot_general(a.astype(BF16), b.astype(BF16), (((1,), (1,)), ((), ())),
                           preferred_element_type=F32)


def _dot_tn(a, b):
    return lax.dot_general(a.astype(BF16), b.astype(BF16), (((0,), (0,)), ((), ())),
                           preferred_element_type=F32)


def _silu(x):
    return x * jax.nn.sigmoid(x)


def _log_sigmoid(x):
    return jnp.minimum(x, 0.0) - jnp.log1p(jnp.exp(-jnp.abs(x)))


def _iota2(n, axis):
    return lax.broadcasted_iota(jnp.int32, (n, n), axis)


def _cumsum_rows(g, tri_bf16):
    g_hi = g.astype(BF16)
    g_lo = (g - g_hi.astype(F32)).astype(BF16)
    return (jnp.dot(tri_bf16, g_hi, preferred_element_type=F32)
            + jnp.dot(tri_bf16, g_lo, preferred_element_type=F32))


def _own_half(x, h):
    lane = lax.broadcasted_iota(jnp.int32, x.shape, 1)
    return jnp.where((lane >= GLA_K) == bool(h % 2), x, 0.0)


def _project(x_ref, g_ref, w_ref, wga_ref, wup_ref, bup_ref, lbt_ref, kn_ref, qt_ref, vt_ref,
             gate_ref, dec_ref, chunk, transposed):
    n_chunks = x_ref.shape[0] // chunk
    u = _rmsnorm(x_ref[...], g_ref[...]).astype(BF16)
    tri = (_iota2(chunk, 0) >= _iota2(chunk, 1)).astype(BF16)
    mid = chunk // 2 - 1

    def proj(col, width):
        blk, off = divmod(col, W_BLOCK)
        assert off + width <= W_BLOCK
        return jnp.dot(u, w_ref[blk, :, off:off + width], preferred_element_type=F32)

    def decay_factors(g, c):
        b = _cumsum_rows(g[c * chunk:(c + 1) * chunk], tri)
        b_mid = b[mid:mid + 1, :]
        b_last = b[chunk - 1:, :]
        return (jnp.exp(b - b_mid), jnp.exp(b_mid - b), jnp.exp(b_last - b),
                jnp.exp(b_mid), jnp.exp(b_last))

    def put_tile(ref, c, i, tile):
        if transposed:
            ref[c, LANES * i:LANES * (i + 1), :] = tile.T
        else:
            ref[c, :, LANES * i:LANES * (i + 1)] = tile

    t0 = lbt_ref[0:1, :]
    t1 = lbt_ref[1:2, :]
    m = jnp.maximum(t0, t1)
    e0 = jnp.exp(t0 - m)
    lb = e0 / (e0 + jnp.exp(t1 - m))

    def value_piece(col, first_head):
        def emit():
            v = proj(col, 2 * LANES).astype(BF16)
            for c in range(n_chunks):
                for h in range(2):
                    put_tile(vt_ref, c, first_head + h,
                             v[c * chunk:(c + 1) * chunk, LANES * h:LANES * (h + 1)])
        return emit

    def gate_piece(col, out_col):
        def emit():
            gate_ref[:, out_col:out_col + 2 * LANES] = proj(col, 2 * LANES)
        return emit

    light = [value_piece(C_HI, 0), value_piece(C_HI + 2 * LANES, 2),
             value_piece(C_GV, HEADS), value_piece(C_GV + 2 * LANES, HEADS + 2),
             gate_piece(C_HG, 0), gate_piece(C_HG + 2 * LANES, 2 * LANES),
             gate_piece(C_GG, HGRN_WIDTH), gate_piece(C_GG + 2 * LANES, HGRN_WIDTH + 2 * LANES)]

    def next_light():
        if light:
            light.pop(0)()

    hf = proj(C_HF, HGRN_WIDTH)
    hq = proj(C_HQ, HGRN_WIDTH)
    ga = _dot_nt(u, wga_ref[...])
    gq = proj(C_GQ, GLA_QK_WIDTH)
    gk = proj(C_GK, GLA_QK_WIDTH)

    f = lb + (1.0 - lb) * jax.nn.sigmoid(hf)
    g = jnp.log(f)
    k = 1.0 - f
    q = _silu(hq)
    for c in range(n_chunks):
        rows = slice(c * chunk, (c + 1) * chunk)
        e_q, e_k, e_end, e_mid, e_last = decay_factors(g, c)
        kn_ref[rows, KM_H:KM_H + HGRN_WIDTH] = (k[rows] * e_k).astype(BF16)
        kn_ref[rows, KE_H:KE_H + HGRN_WIDTH] = (k[rows] * e_end).astype(BF16)
        qm = (q[rows] * e_q).astype(BF16)
        for h in range(HEADS):
            put_tile(qt_ref, c, h, qm[:, LANES * h:LANES * (h + 1)])
        dec_ref[c, :, 0:HGRN_WIDTH] = e_mid
        dec_ref[c, :, DEC_COLS:DEC_COLS + HGRN_WIDTH] = e_last
        next_light()

    logits = _dot(ga, wup_ref[...]) + bup_ref[...]
    g = _log_sigmoid(logits) / GATE_NORMALIZER
    q = gq * (GLA_K ** -0.5)
    k = gk
    for c in range(n_chunks):
        rows = slice(c * chunk, (c + 1) * chunk)
        e_q, e_k, e_end, e_mid, e_last = decay_factors(g, c)
        qm = q[rows] * e_q
        ke = k[rows] * e_end
        kn_ref[rows, KM_G:KM_G + GLA_QK_WIDTH] = (k[rows] * e_k).astype(BF16)
        for h in range(HEADS):
            pair = slice(LANES * (h // 2), LANES * (h // 2 + 1))
            put_tile(qt_ref, c, HEADS + h, _own_half(qm[:, pair], h).astype(BF16))
            kn_ref[rows, KE_G + LANES * h:KE_G + LANES * (h + 1)] = (
                _own_half(ke[:, pair], h).astype(BF16))
        dec_ref[c, :, HGRN_WIDTH:DEC_COLS] = e_mid
        dec_ref[c, :, DEC_COLS + HGRN_WIDTH:2 * DEC_COLS] = e_last
        next_light()
    while light:
        next_light()


def _meta_kernel(x_ref, g_ref, wt_ref, wgat_ref, wup_ref, bup_ref, lbt_ref,
                 w_ref, wga_ref, kn_ref, q_ref, v_ref, gate_ref, dec_ref, w_scr):
    j = pl.program_id(0)
    n_blocks = w_scr.shape[0]

    @pl.when(j < n_blocks)
    def _():
        blk = wt_ref[...].T.astype(BF16)
        w_ref[0] = blk
        w_scr[j] = blk

    @pl.when(j == n_blocks)
    def _():
        wga_ref[...] = wgat_ref[...].astype(BF16)
        _project(x_ref, g_ref, w_scr, wga_ref, wup_ref, bup_ref, lbt_ref, kn_ref, q_ref, v_ref,
                 gate_ref, dec_ref, N_META, transposed=False)


def _meta(meta_tokens, g, w_in_t, w_up, b_up, lb_table):
    n_blocks = C_GA // W_BLOCK
    full = lambda a: pl.BlockSpec(a.shape, lambda j: (0,) * len(a.shape))
    last = n_blocks - 1
    out_shapes = [jax.ShapeDtypeStruct((n_blocks, D_MODEL, W_BLOCK), BF16),
                  jax.ShapeDtypeStruct((GATE_RANK, D_MODEL), BF16),
                  jax.ShapeDtypeStruct((N_META, KN_COLS), BF16),
                  jax.ShapeDtypeStruct((1, N_META, MIX_WIDTH), BF16),
                  jax.ShapeDtypeStruct((1, N_META, MIX_WIDTH), BF16),
                  jax.ShapeDtypeStruct((N_META, MIX_WIDTH), F32),
                  jax.ShapeDtypeStruct((1, 1, 2 * DEC_COLS), F32)]
    w_main, w_ga, kn, _, v, _, _ = pl.pallas_call(
        _meta_kernel,
        grid=(n_blocks + 1,),
        in_specs=[
            full(meta_tokens), full(g),
            pl.BlockSpec((W_BLOCK, D_MODEL), lambda j: (jnp.minimum(j, last), 0)),
            pl.BlockSpec((GATE_RANK, D_MODEL), lambda j: (C_GA // GATE_RANK, 0)),
            full(w_up), full(b_up), full(lb_table),
        ],
        out_specs=[pl.BlockSpec((1, D_MODEL, W_BLOCK), lambda j: (jnp.minimum(j, last), 0, 0))]
        + [full(s) for s in out_shapes[1:]],
        out_shape=out_shapes,
        scratch_shapes=[pltpu.VMEM((n_blocks, D_MODEL, W_BLOCK), BF16)],
        compiler_params=pltpu.CompilerParams(
            dimension_semantics=("arbitrary",), vmem_limit_bytes=V7X_VMEM_LIMIT),
        name="meta",
    )(meta_tokens, g, w_in_t, w_in_t, w_up, b_up, lb_table)
    return w_main, w_ga, kn, v


def _inproj_kernel(x_ref, g_ref, w_ref, wga_ref, wup_ref, bup_ref, lbt_ref, wout32_ref, w1_32_ref,
                   w2_32_ref, kn_ref, qt_ref, vt_ref, gate_ref, dec_ref, wout_ref, w1_ref, w2_ref):
    _project(x_ref, g_ref, w_ref, wga_ref, wup_ref, bup_ref, lbt_ref, kn_ref, qt_ref, vt_ref,
             gate_ref, dec_ref, CHUNK, transposed=True)
    wout_ref[...] = wout32_ref[...].astype(BF16)
    w1_ref[...] = w1_32_ref[...].astype(BF16)
    w2_ref[...] = w2_32_ref[...].astype(BF16)


def _inproj(x2d, g, w_main, w_ga, w_up, b_up, lb_table, w_out, w1, w2, rows):
    n = x2d.shape[0]
    steps = n // rows
    chunks = rows // CHUNK
    wout_rows = w_out.shape[0] // steps
    w1_rows = w1.shape[0] // steps
    w2_rows = 2 * w2.shape[0] // steps
    assert wout_rows * steps == w_out.shape[0] and w1_rows * steps == w1.shape[0]
    assert w2_rows * steps == 2 * w2.shape[0]
    const = lambda i: (0, 0)
    tile = lambda i: (i, 0)
    tile3 = lambda i: (i, 0, 0)
    half_pace = lambda i: (i // 2, 0)
    w_specs = [pl.BlockSpec((wout_rows, w_out.shape[1]), tile),
               pl.BlockSpec((w1_rows, w1.shape[1]), tile),
               pl.BlockSpec((w2_rows, w2.shape[1]), half_pace)]
    return pl.pallas_call(
        _inproj_kernel,
        grid=(steps,),
        in_specs=[
            pl.BlockSpec((rows, D_MODEL), tile),
            pl.BlockSpec((1, D_MODEL), const),
            pl.BlockSpec(w_main.shape, lambda i: (0, 0, 0), pipeline_mode=pl.Buffered(1)),
            pl.BlockSpec(w_ga.shape, const),
            pl.BlockSpec((GATE_RANK, GLA_QK_WIDTH), const),
            pl.BlockSpec((1, GLA_QK_WIDTH), const),
            pl.BlockSpec(lb_table.shape, const),
        ] + w_specs,
        out_specs=[
            pl.BlockSpec((rows, KN_COLS), tile),
            pl.BlockSpec((chunks, MIX_WIDTH, CHUNK), tile3),
            pl.BlockSpec((chunks, MIX_WIDTH, CHUNK), tile3),
            pl.BlockSpec((rows, MIX_WIDTH), tile),
            pl.BlockSpec((chunks, 1, 2 * DEC_COLS), tile3),
        ] + w_specs,
        out_shape=[
            jax.ShapeDtypeStruct((n, KN_COLS), BF16),
            jax.ShapeDtypeStruct((n // CHUNK, MIX_WIDTH, CHUNK), BF16),
            jax.ShapeDtypeStruct((n // CHUNK, MIX_WIDTH, CHUNK), BF16),
            jax.ShapeDtypeStruct((n, MIX_WIDTH), F32),
            jax.ShapeDtypeStruct((n // CHUNK, 1, 2 * DEC_COLS), F32),
            jax.ShapeDtypeStruct(w_out.shape, BF16),
            jax.ShapeDtypeStruct(w1.shape, BF16),
            jax.ShapeDtypeStruct(w2.shape, BF16),
        ],
        compiler_params=pltpu.CompilerParams(
            dimension_semantics=("arbitrary",), vmem_limit_bytes=V7X_VMEM_LIMIT),
        name="inproj",
    )(x2d, g, w_main, w_ga, w_up, b_up, lb_table, w_out, w1, w2)


def _mixer_kernel(kn_ref, qt_ref, vt_ref, dec_ref, mkn_ref, mv_ref, hnorm_ref, gnorm_ref,
                  o_ref, st_ref, *, steps_per_seq):
    @pl.when(lax.rem(pl.program_id(0), steps_per_seq) == 0)
    def _():
        for i, (_, ke_col, _) in enumerate(HEAD_TABLE):
            st_ref[i] = _dot_tn(mv_ref[0, :, LANES * i:LANES * (i + 1)],
                                mkn_ref[:, ke_col:ke_col + LANES])

    key_before_query = _iota2(CHUNK, 0) <= _iota2(CHUNK, 1)
    norms = (hnorm_ref[...],) * HEADS + (gnorm_ref[...],) * HEADS

    for c in range(kn_ref.shape[0] // CHUNK):
        rows = slice(c * CHUNK, (c + 1) * CHUNK)
        dec = dec_ref[c]
        first, outs = [], []
        for i, (km_col, _, dec_col) in enumerate(HEAD_TABLE):
            s_mid = st_ref[i] * dec[:, dec_col:dec_col + LANES]
            lhs = jnp.concatenate([kn_ref[rows, km_col:km_col + LANES], s_mid.astype(BF16)],
                                  axis=0)
            first.append(jnp.dot(lhs, qt_ref[c, LANES * i:LANES * (i + 1), :],
                                 preferred_element_type=F32))
        for i in range(N_STATES):
            scores_t = jnp.where(key_before_query, first[i][:CHUNK], 0.0).astype(BF16)
            outs.append(first[i][CHUNK:] + jnp.dot(vt_ref[c, LANES * i:LANES * (i + 1), :],
                                                   scores_t, preferred_element_type=F32))
        for i, (_, ke_col, dec_col) in enumerate(HEAD_TABLE):
            st_ref[i] = (st_ref[i] * dec[:, DEC_COLS + dec_col:DEC_COLS + dec_col + LANES]
                         + jnp.dot(vt_ref[c, LANES * i:LANES * (i + 1), :],
                                   kn_ref[rows, ke_col:ke_col + LANES],
                                   preferred_element_type=F32))
        for i in range(N_STATES):
            o_t = outs[i]
            o_t = o_t * lax.rsqrt(jnp.mean(o_t * o_t, axis=0, keepdims=True) + NORM_EPS)
            o_ref[rows, LANES * i:LANES * (i + 1)] = (o_t.T * norms[i]).astype(o_ref.dtype)


def _mixer(kn, qt, vt, dec, meta_kn, meta_v, hnorm, gnorm, seq, rows):
    n = kn.shape[0]
    chunks = rows // CHUNK
    const = lambda s: (0, 0)
    tile = lambda s: (s, 0)
    tile3 = lambda s: (s, 0, 0)
    return pl.pallas_call(
        functools.partial(_mixer_kernel, steps_per_seq=seq // rows),
        grid=(n // rows,),
        in_specs=[
            pl.BlockSpec((rows, KN_COLS), tile),
            pl.BlockSpec((chunks, MIX_WIDTH, CHUNK), tile3),
            pl.BlockSpec((chunks, MIX_WIDTH, CHUNK), tile3),
            pl.BlockSpec((chunks, 1, 2 * DEC_COLS), tile3),
            pl.BlockSpec(meta_kn.shape, const),
            pl.BlockSpec(meta_v.shape, lambda s: (0, 0, 0)),
            pl.BlockSpec((1, HEAD_V), const),
            pl.BlockSpec((1, HEAD_V), const),
        ],
        out_specs=pl.BlockSpec((rows, MIX_WIDTH), tile),
        out_shape=jax.ShapeDtypeStruct((n, MIX_WIDTH), BF16),
        scratch_shapes=[pltpu.VMEM((N_STATES, HEAD_V, LANES), F32)],
        compiler_params=pltpu.CompilerParams(
            dimension_semantics=("arbitrary",), vmem_limit_bytes=V7X_VMEM_LIMIT),
        name="mixer",
    )(kn, qt, vt, dec, meta_kn, meta_v, hnorm, gnorm)


def _ffn_kernel(x_ref, o_ref, gate_ref, wout_ref, g2_ref, w1_ref, w2_ref, g3_ref, y_ref):
    o = o_ref[...].astype(F32) * _silu(gate_ref[...])
    h = x_ref[...] + _dot(o, wout_ref[...])
    u = _rmsnorm(h, g2_ref[...]).astype(BF16)
    gate = jnp.dot(u, w1_ref[:, :FFN_HIDDEN], preferred_element_type=F32)
    up = jnp.dot(u, w1_ref[:, FFN_HIDDEN:], preferred_element_type=F32)
    h = h + _dot(_silu(gate) * up, w2_ref[...])
    y_ref[...] = _rmsnorm(h, g3_ref[...])


def _ffn(x2d, o, gates, w_out, g2, w1, w2, g3, rows):
    n = x2d.shape[0]
    const = lambda i: (0, 0)
    tile = lambda i: (i, 0)
    resident = functools.partial(pl.BlockSpec, index_map=const, pipeline_mode=pl.Buffered(1))
    return pl.pallas_call(
        _ffn_kernel,
        grid=(n // rows,),
        in_specs=[
            pl.BlockSpec((rows, D_MODEL), tile),
            pl.BlockSpec((rows, MIX_WIDTH), tile),
            pl.BlockSpec((rows, MIX_WIDTH), tile),
            resident((MIX_WIDTH, D_MODEL)),
            pl.BlockSpec((1, D_MODEL), const),
            resident((D_MODEL, 2 * FFN_HIDDEN)),
            resident((FFN_HIDDEN, D_MODEL)),
            pl.BlockSpec((1, D_MODEL), const),
        ],
        out_specs=pl.BlockSpec((rows, D_MODEL), tile),
        out_shape=jax.ShapeDtypeStruct((n, D_MODEL), F32),
        compiler_params=pltpu.CompilerParams(
            dimension_semantics=("arbitrary",), vmem_limit_bytes=V7X_VMEM_LIMIT),
        name="outproj_ffn",
    )(x2d, o, gates, w_out, g2, w1, w2, g3)


def kernel(x, meta_tokens, lb_table, norm_mix_g, w_in, w_gla_gate_up, b_gla_gate, hgrn_norm_g,
           gla_norm_g, w_out, norm_ffn_g, w_ffn_in, w_ffn_out, norm_final_g):
    batch, seq, _ = x.shape
    assert w_in.shape[0] == 1 and lb_table.shape[0] == 2, "single-layer block"
    assert seq % ROWS_MIXER == 0
    x2d = x.reshape(batch * seq, D_MODEL)
    g_mix = norm_mix_g[0][None, :]
    w_up = w_gla_gate_up[0]
    b_up = b_gla_gate[0][None, :]

    w_main, w_ga, meta_kn, meta_v = _meta(meta_tokens, g_mix, w_in[0].T, w_up, b_up, lb_table)
    kn, qt, vt, gates, dec, w_out_b, w1_b, w2_b = _inproj(
        x2d, g_mix, w_main, w_ga, w_up, b_up, lb_table, w_out[0], w_ffn_in[0], w_ffn_out[0],
        ROWS_INPROJ)
    o = _mixer(kn, qt, vt, dec, meta_kn, meta_v, hgrn_norm_g[0][None, :],
               gla_norm_g[0][None, :], seq, ROWS_MIXER)
    y = _ffn(x2d, o, gates, w_out_b, norm_ffn_g[0][None, :], w1_b, w2_b, norm_final_g[None, :],
             ROWS_FFN)
    return y.reshape(batch, seq, D_MODEL)
```

```python
import functools

import jax
import jax.numpy as jnp
from jax import lax
from jax.experimental import pallas as pl
from jax.experimental.pallas import tpu as pltpu

F32 = jnp.float32
BF16 = jnp.bfloat16

D_MODEL = 1024
N_META = 16
CHUNK = 128
HEADS = 4
HEAD_V = 128
HGRN_WIDTH = HEADS * HEAD_V
GLA_K = 64
GLA_QK_WIDTH = HEADS * GLA_K
GLA_WIDTH = HEADS * HEAD_V
MIX_WIDTH = HGRN_WIDTH + GLA_WIDTH
GATE_RANK = 16
GATE_NORMALIZER = 16.0
LOG2_E = 1.4426950408889634
FFN_HIDDEN = 2816
NORM_EPS = 1e-6
LANES = 128
N_STATES = 2 * HEADS

C_HQ, C_HF, C_HI, C_HG = 0, 512, 1024, 1536
C_GQ, C_GK, C_GV, C_GG = 2048, 2304, 2560, 3072
C_GA = 3584
IN_COLS = C_GA + GATE_RANK
W_BLOCK = 512

KM_H, KM_G = 0, HGRN_WIDTH
KE_H, KE_G = HGRN_WIDTH + GLA_QK_WIDTH, 2 * HGRN_WIDTH + GLA_QK_WIDTH
KN_COLS = KE_G + GLA_WIDTH
DEC_COLS = HGRN_WIDTH + GLA_QK_WIDTH

HEAD_TABLE = tuple(
    [(KM_H + LANES * h, KE_H + LANES * h, LANES * h) for h in range(HEADS)]
    + [(KM_G + LANES * (h // 2), KE_G + LANES * h, HGRN_WIDTH + LANES * (h // 2))
       for h in range(HEADS)])

V7X_VMEM_LIMIT = 56 * 1024 * 1024

ROWS_INPROJ = 512
ROWS_MIXER = 1024
ROWS_FFN = 512


def _rmsnorm(x, g):
    return x * lax.rsqrt(jnp.mean(x * x, axis=-1, keepdims=True) + NORM_EPS) * g


def _dot(a, b):
    return jnp.dot(a.astype(BF16), b.astype(BF16), preferred_element_type=F32)


def _dot_nt(a, b):
    return lax.dot_general(a.astype(BF16), b.astype(BF16), (((1,), (1,)), ((), ())),
                           preferred_element_type=F32)


def _dot_tn(a, b):
    return lax.dot_general(a.astype(BF16), b.astype(BF16), (((0,), (0,)), ((), ())),
                           preferred_element_type=F32)


def _sigmoid(x):
    return 0.5 * jnp.tanh(0.5 * x) + 0.5


def _silu(x):
    return x * _sigmoid(x)


def _log_sigmoid(x):
    return jnp.minimum(x, 0.0) - jnp.log1p(jnp.exp(-jnp.abs(x)))


def _iota2(n, axis):
    return lax.broadcasted_iota(jnp.int32, (n, n), axis)


def _cumsum_rows(g, tri_bf16):
    g_hi = g.astype(BF16)
    g_lo = (g - g_hi.astype(F32)).astype(BF16)
    return (jnp.dot(tri_bf16, g_hi, preferred_element_type=F32)
            + jnp.dot(tri_bf16, g_lo, preferred_element_type=F32))


def _own_half(x, h):
    lane = lax.broadcasted_iota(jnp.int32, x.shape, 1)
    return jnp.where((lane >= GLA_K) == bool(h % 2), x, 0.0)


def _project(x_ref, g_ref, w_ref, wga_ref, wup_ref, bup_ref, lbt_ref, kn_ref, qt_ref, vt_ref,
             gate_ref, dec_ref, chunk, transposed):
    n_chunks = x_ref.shape[0] // chunk
    u = _rmsnorm(x_ref[...], g_ref[...]).astype(BF16)
    tri = (_iota2(chunk, 0) >= _iota2(chunk, 1)).astype(BF16)
    mid = chunk // 2 - 1

    def proj(col, width):
        blk, off = divmod(col, W_BLOCK)
        assert off + width <= W_BLOCK
        return jnp.dot(u, w_ref[blk, :, off:off + width], preferred_element_type=F32)

    def decay_factors(g, c):
        b = _cumsum_rows(g[c * chunk:(c + 1) * chunk], tri)
        b_mid = b[mid:mid + 1, :]
        b_last = b[chunk - 1:, :]
        return (jnp.exp2(b - b_mid), jnp.exp2(b_mid - b), jnp.exp2(b_last - b),
                jnp.exp2(b_mid), jnp.exp2(b_last))

    def put_tile(ref, c, i, tile):
        if transposed:
            ref[c, LANES * i:LANES * (i + 1), :] = tile.T
        else:
            ref[c, :, LANES * i:LANES * (i + 1)] = tile

    t0 = lbt_ref[0:1, :]
    t1 = lbt_ref[1:2, :]
    m = jnp.maximum(t0, t1)
    e0 = jnp.exp(t0 - m)
    lb = e0 / (e0 + jnp.exp(t1 - m))

    def value_piece(col, first_head):
        def emit():
            v = proj(col, 2 * LANES).astype(BF16)
            for c in range(n_chunks):
                for h in range(2):
                    put_tile(vt_ref, c, first_head + h,
                             v[c * chunk:(c + 1) * chunk, LANES * h:LANES * (h + 1)])
        return emit

    def gate_piece(col, out_col):
        def emit():
            gate_ref[:, out_col:out_col + 2 * LANES] = proj(col, 2 * LANES)
        return emit

    light = [value_piece(C_HI, 0), value_piece(C_HI + 2 * LANES, 2),
             value_piece(C_GV, HEADS), value_piece(C_GV + 2 * LANES, HEADS + 2),
             gate_piece(C_HG, 0), gate_piece(C_HG + 2 * LANES, 2 * LANES),
             gate_piece(C_GG, HGRN_WIDTH), gate_piece(C_GG + 2 * LANES, HGRN_WIDTH + 2 * LANES)]

    def next_light():
        if light:
            light.pop(0)()

    hf = proj(C_HF, HGRN_WIDTH)
    hq = proj(C_HQ, HGRN_WIDTH)
    ga = _dot_nt(u, wga_ref[...])
    gq = proj(C_GQ, GLA_QK_WIDTH)
    gk = proj(C_GK, GLA_QK_WIDTH)

    f = 0.5 * (1.0 + lb) + (0.5 * (1.0 - lb)) * jnp.tanh(0.5 * hf)
    g = jnp.log2(f)
    k = 1.0 - f
    q = _silu(hq)
    for c in range(n_chunks):
        rows = slice(c * chunk, (c + 1) * chunk)
        e_q, e_k, e_end, e_mid, e_last = decay_factors(g, c)
        kn_ref[rows, KM_H:KM_H + HGRN_WIDTH] = (k[rows] * e_k).astype(BF16)
        kn_ref[rows, KE_H:KE_H + HGRN_WIDTH] = (k[rows] * e_end).astype(BF16)
        qm = (q[rows] * e_q).astype(BF16)
        for h in range(HEADS):
            put_tile(qt_ref, c, h, qm[:, LANES * h:LANES * (h + 1)])
        dec_ref[c, :, 0:HGRN_WIDTH] = e_mid
        dec_ref[c, :, DEC_COLS:DEC_COLS + HGRN_WIDTH] = e_last
        next_light()

    logits = _dot(ga, wup_ref[...]) + bup_ref[...]
    g = _log_sigmoid(logits) * (LOG2_E / GATE_NORMALIZER)
    q = gq * (GLA_K ** -0.5)
    k = gk
    for c in range(n_chunks):
        rows = slice(c * chunk, (c + 1) * chunk)
        e_q, e_k, e_end, e_mid, e_last = decay_factors(g, c)
        qm = q[rows] * e_q
        ke = k[rows] * e_end
        kn_ref[rows, KM_G:KM_G + GLA_QK_WIDTH] = (k[rows] * e_k).astype(BF16)
        for h in range(HEADS):
            pair = slice(LANES * (h // 2), LANES * (h // 2 + 1))
            put_tile(qt_ref, c, HEADS + h, _own_half(qm[:, pair], h).astype(BF16))
            kn_ref[rows, KE_G + LANES * h:KE_G + LANES * (h + 1)] = (
                _own_half(ke[:, pair], h).astype(BF16))
        dec_ref[c, :, HGRN_WIDTH:DEC_COLS] = e_mid
        dec_ref[c, :, DEC_COLS + HGRN_WIDTH:2 * DEC_COLS] = e_last
        next_light()
    while light:
        next_light()


def _meta_kernel(x_ref, g_ref, wt_ref, wgat_ref, wup_ref, bup_ref, lbt_ref,
                 w_ref, wga_ref, kn_ref, q_ref, v_ref, gate_ref, dec_ref, w_scr):
    j = pl.program_id(0)
    n_blocks = w_scr.shape[0]

    @pl.when(j < n_blocks)
    def _():
        blk = wt_ref[...].T.astype(BF16)
        w_ref[0] = blk
        w_scr[j] = blk

    @pl.when(j == n_blocks)
    def _():
        wga_ref[...] = wgat_ref[...].astype(BF16)
        _project(x_ref, g_ref, w_scr, wga_ref, wup_ref, bup_ref, lbt_ref, kn_ref, q_ref, v_ref,
                 gate_ref, dec_ref, N_META, transposed=False)


def _meta(meta_tokens, g, w_in_t, w_up, b_up, lb_table):
    n_blocks = C_GA // W_BLOCK
    full = lambda a: pl.BlockSpec(a.shape, lambda j: (0,) * len(a.shape))
    last = n_blocks - 1
    out_shapes = [jax.ShapeDtypeStruct((n_blocks, D_MODEL, W_BLOCK), BF16),
                  jax.ShapeDtypeStruct((GATE_RANK, D_MODEL), BF16),
                  jax.ShapeDtypeStruct((N_META, KN_COLS), BF16),
                  jax.ShapeDtypeStruct((1, N_META, MIX_WIDTH), BF16),
                  jax.ShapeDtypeStruct((1, N_META, MIX_WIDTH), BF16),
                  jax.ShapeDtypeStruct((N_META, MIX_WIDTH), F32),
                  jax.ShapeDtypeStruct((1, 1, 2 * DEC_COLS), F32)]
    w_main, w_ga, kn, _, v, _, _ = pl.pallas_call(
        _meta_kernel,
        grid=(n_blocks + 1,),
        in_specs=[
            full(meta_tokens), full(g),
            pl.BlockSpec((W_BLOCK, D_MODEL), lambda j: (jnp.minimum(j, last), 0)),
            pl.BlockSpec((GATE_RANK, D_MODEL), lambda j: (C_GA // GATE_RANK, 0)),
            full(w_up), full(b_up), full(lb_table),
        ],
        out_specs=[pl.BlockSpec((1, D_MODEL, W_BLOCK), lambda j: (jnp.minimum(j, last), 0, 0))]
        + [full(s) for s in out_shapes[1:]],
        out_shape=out_shapes,
        scratch_shapes=[pltpu.VMEM((n_blocks, D_MODEL, W_BLOCK), BF16)],
        compiler_params=pltpu.CompilerParams(
            dimension_semantics=("arbitrary",), vmem_limit_bytes=V7X_VMEM_LIMIT),
        name="meta",
    )(meta_tokens, g, w_in_t, w_in_t, w_up, b_up, lb_table)
    return w_main, w_ga, kn, v


def _inproj_kernel(x_ref, g_ref, w_ref, wga_ref, wup_ref, bup_ref, lbt_ref, wout32_ref, w1_32_ref,
                   w2_32_ref, kn_ref, qt_ref, vt_ref, gate_ref, dec_ref, wout_ref, w1_ref, w2_ref):
    _project(x_ref, g_ref, w_ref, wga_ref, wup_ref, bup_ref, lbt_ref, kn_ref, qt_ref, vt_ref,
             gate_ref, dec_ref, CHUNK, transposed=True)
    wout_ref[...] = wout32_ref[...].astype(BF16)
    w1_ref[...] = w1_32_ref[...].astype(BF16)
    w2_ref[...] = w2_32_ref[...].astype(BF16)


def _inproj(x2d, g, w_main, w_ga, w_up, b_up, lb_table, w_out, w1, w2, rows):
    n = x2d.shape[0]
    steps = n // rows
    chunks = rows // CHUNK
    wout_rows = w_out.shape[0] // steps
    w1_rows = w1.shape[0] // steps
    w2_rows = 2 * w2.shape[0] // steps
    assert wout_rows * steps == w_out.shape[0] and w1_rows * steps == w1.shape[0]
    assert w2_rows * steps == 2 * w2.shape[0]
    const = lambda i: (0, 0)
    tile = lambda i: (i, 0)
    tile3 = lambda i: (i, 0, 0)
    half_pace = lambda i: (i // 2, 0)
    w_specs = [pl.BlockSpec((wout_rows, w_out.shape[1]), tile),
               pl.BlockSpec((w1_rows, w1.shape[1]), tile),
               pl.BlockSpec((w2_rows, w2.shape[1]), half_pace)]
    return pl.pallas_call(
        _inproj_kernel,
        grid=(steps,),
        in_specs=[
            pl.BlockSpec((rows, D_MODEL), tile),
            pl.BlockSpec((1, D_MODEL), const),
            pl.BlockSpec(w_main.shape, lambda i: (0, 0, 0), pipeline_mode=pl.Buffered(1)),
            pl.BlockSpec(w_ga.shape, const),
            pl.BlockSpec((GATE_RANK, GLA_QK_WIDTH), const),
            pl.BlockSpec((1, GLA_QK_WIDTH), const),
            pl.BlockSpec(lb_table.shape, const),
        ] + w_specs,
        out_specs=[
            pl.BlockSpec((rows, KN_COLS), tile),
            pl.BlockSpec((chunks, MIX_WIDTH, CHUNK), tile3),
            pl.BlockSpec((chunks, MIX_WIDTH, CHUNK), tile3),
            pl.BlockSpec((rows, MIX_WIDTH), tile),
            pl.BlockSpec((chunks, 1, 2 * DEC_COLS), tile3),
        ] + w_specs,
        out_shape=[
            jax.ShapeDtypeStruct((n, KN_COLS), BF16),
            jax.ShapeDtypeStruct((n // CHUNK, MIX_WIDTH, CHUNK), BF16),
            jax.ShapeDtypeStruct((n // CHUNK, MIX_WIDTH, CHUNK), BF16),
            jax.ShapeDtypeStruct((n, MIX_WIDTH), F32),
            jax.ShapeDtypeStruct((n // CHUNK, 1, 2 * DEC_COLS), F32),
            jax.ShapeDtypeStruct(w_out.shape, BF16),
            jax.ShapeDtypeStruct(w1.shape, BF16),
            jax.ShapeDtypeStruct(w2.shape, BF16),
        ],
        compiler_params=pltpu.CompilerParams(
            dimension_semantics=("arbitrary",), vmem_limit_bytes=V7X_VMEM_LIMIT),
        name="inproj",
    )(x2d, g, w_main, w_ga, w_up, b_up, lb_table, w_out, w1, w2)


def _mixer_kernel(kn_ref, qt_ref, vt_ref, dec_ref, mkn_ref, mv_ref, hnorm_ref, gnorm_ref,
                  o_ref, st_ref, *, steps_per_seq):
    @pl.when(lax.rem(pl.program_id(0), steps_per_seq) == 0)
    def _():
        for i, (_, ke_col, _) in enumerate(HEAD_TABLE):
            st_ref[i] = _dot_tn(mv_ref[0, :, LANES * i:LANES * (i + 1)],
                                mkn_ref[:, ke_col:ke_col + LANES])

    key_before_query = _iota2(CHUNK, 0) <= _iota2(CHUNK, 1)
    norms = (hnorm_ref[...],) * HEADS + (gnorm_ref[...],) * HEADS

    for c in range(kn_ref.shape[0] // CHUNK):
        rows = slice(c * CHUNK, (c + 1) * CHUNK)
        dec = dec_ref[c]
        first, outs = [], []
        for i, (km_col, _, dec_col) in enumerate(HEAD_TABLE):
            s_mid = st_ref[i] * dec[:, dec_col:dec_col + LANES]
            lhs = jnp.concatenate([kn_ref[rows, km_col:km_col + LANES], s_mid.astype(BF16)],
                                  axis=0)
            first.append(jnp.dot(lhs, qt_ref[c, LANES * i:LANES * (i + 1), :],
                                 preferred_element_type=F32))
        for i in range(N_STATES):
            scores_t = jnp.where(key_before_query, first[i][:CHUNK], 0.0).astype(BF16)
            outs.append(first[i][CHUNK:] + jnp.dot(vt_ref[c, LANES * i:LANES * (i + 1), :],
                                                   scores_t, preferred_element_type=F32))
        for i, (_, ke_col, dec_col) in enumerate(HEAD_TABLE):
            st_ref[i] = (st_ref[i] * dec[:, DEC_COLS + dec_col:DEC_COLS + dec_col + LANES]
                         + jnp.dot(vt_ref[c, LANES * i:LANES * (i + 1), :],
                                   kn_ref[rows, ke_col:ke_col + LANES],
                                   preferred_element_type=F32))
        for i in range(N_STATES):
            o_t = outs[i]
            o_t = o_t * lax.rsqrt(jnp.mean(o_t * o_t, axis=0, keepdims=True) + NORM_EPS)
            o_ref[rows, LANES * i:LANES * (i + 1)] = (o_t.T * norms[i]).astype(o_ref.dtype)


def _mixer(kn, qt, vt, dec, meta_kn, meta_v, hnorm, gnorm, seq, rows):
    n = kn.shape[0]
    chunks = rows // CHUNK
    const = lambda s: (0, 0)
    tile = lambda s: (s, 0)
    tile3 = lambda s: (s, 0, 0)
    return pl.pallas_call(
        functools.partial(_mixer_kernel, steps_per_seq=seq // rows),
        grid=(n // rows,),
        in_specs=[
            pl.BlockSpec((rows, KN_COLS), tile),
            pl.BlockSpec((chunks, MIX_WIDTH, CHUNK), tile3),
            pl.BlockSpec((chunks, MIX_WIDTH, CHUNK), tile3),
            pl.BlockSpec((chunks, 1, 2 * DEC_COLS), tile3),
            pl.BlockSpec(meta_kn.shape, const),
            pl.BlockSpec(meta_v.shape, lambda s: (0, 0, 0)),
            pl.BlockSpec((1, HEAD_V), const),
            pl.BlockSpec((1, HEAD_V), const),
        ],
        out_specs=pl.BlockSpec((rows, MIX_WIDTH), tile),
        out_shape=jax.ShapeDtypeStruct((n, MIX_WIDTH), BF16),
        scratch_shapes=[pltpu.VMEM((N_STATES, HEAD_V, LANES), F32)],
        compiler_params=pltpu.CompilerParams(
            dimension_semantics=("arbitrary",), vmem_limit_bytes=V7X_VMEM_LIMIT),
        name="mixer",
    )(kn, qt, vt, dec, meta_kn, meta_v, hnorm, gnorm)


def _ffn_kernel(x_ref, o_ref, gate_ref, wout_ref, g2_ref, w1_ref, w2_ref, g3_ref, y_ref):
    o = o_ref[...].astype(F32) * _silu(gate_ref[...])
    h = x_ref[...] + _dot(o, wout_ref[...])
    u = _rmsnorm(h, g2_ref[...]).astype(BF16)
    gate = jnp.dot(u, w1_ref[:, :FFN_HIDDEN], preferred_element_type=F32)
    up = jnp.dot(u, w1_ref[:, FFN_HIDDEN:], preferred_element_type=F32)
    h = h + _dot(_silu(gate) * up, w2_ref[...])
    y_ref[...] = _rmsnorm(h, g3_ref[...])


def _ffn(x2d, o, gates, w_out, g2, w1, w2, g3, rows):
    n = x2d.shape[0]
    const = lambda i: (0, 0)
    tile = lambda i: (i, 0)
    resident = functools.partial(pl.BlockSpec, index_map=const, pipeline_mode=pl.Buffered(1))
    return pl.pallas_call(
        _ffn_kernel,
        grid=(n // rows,),
        in_specs=[
            pl.BlockSpec((rows, D_MODEL), tile),
            pl.BlockSpec((rows, MIX_WIDTH), tile),
            pl.BlockSpec((rows, MIX_WIDTH), tile),
            resident((MIX_WIDTH, D_MODEL)),
            pl.BlockSpec((1, D_MODEL), const),
            resident((D_MODEL, 2 * FFN_HIDDEN)),
            resident((FFN_HIDDEN, D_MODEL)),
            pl.BlockSpec((1, D_MODEL), const),
        ],
        out_specs=pl.BlockSpec((rows, D_MODEL), tile),
        out_shape=jax.ShapeDtypeStruct((n, D_MODEL), F32),
        compiler_params=pltpu.CompilerParams(
            dimension_semantics=("arbitrary",), vmem_limit_bytes=V7X_VMEM_LIMIT),
        name="outproj_ffn",
    )(x2d, o, gates, w_out, g2, w1, w2, g3)


def kernel(x, meta_tokens, lb_table, norm_mix_g, w_in, w_gla_gate_up, b_gla_gate, hgrn_norm_g,
           gla_norm_g, w_out, norm_ffn_g, w_ffn_in, w_ffn_out, norm_final_g):
    batch, seq, _ = x.shape
    assert w_in.shape[0] == 1 and lb_table.shape[0] == 2, "single-layer block"
    assert seq % ROWS_MIXER == 0
    x2d = x.reshape(batch * seq, D_MODEL)
    g_mix = norm_mix_g[0][None, :]
    w_up = w_gla_gate_up[0]
    b_up = b_gla_gate[0][None, :]

    w_main, w_ga, meta_kn, meta_v = _meta(meta_tokens, g_mix, w_in[0].T, w_up, b_up, lb_table)
    kn, qt, vt, gates, dec, w_out_b, w1_b, w2_b = _inproj(
        x2d, g_mix, w_main, w_ga, w_up, b_up, lb_table, w_out[0], w_ffn_in[0], w_ffn_out[0],
        ROWS_INPROJ)
    o = _mixer(kn, qt, vt, dec, meta_kn, meta_v, hgrn_norm_g[0][None, :],
               gla_norm_g[0][None, :], seq, ROWS_MIXER)
    y = _ffn(x2d, o, gates, w_out_b, norm_ffn_g[0][None, :], w1_b, w2_b, norm_final_g[None, :],
             ROWS_FFN)
    return y.reshape(batch, seq, D_MODEL)
```

```python
import functools

import jax
import jax.numpy as jnp
from jax import lax
from jax.experimental import pallas as pl
from jax.experimental.pallas import tpu as pltpu

F32 = jnp.float32
BF16 = jnp.bfloat16

D_MODEL = 1024
N_META = 16
CHUNK = 128
HEADS = 4
HEAD_V = 128
HGRN_WIDTH = HEADS * HEAD_V
GLA_K = 64
GLA_QK_WIDTH = HEADS * GLA_K
GLA_WIDTH = HEADS * HEAD_V
MIX_WIDTH = HGRN_WIDTH + GLA_WIDTH
GATE_RANK = 16
GATE_NORMALIZER = 16.0
LOG2_E = 1.4426950408889634
FFN_HIDDEN = 2816
NORM_EPS = 1e-6
LANES = 128
N_STATES = 2 * HEADS

C_HQ, C_HF, C_HI, C_HG = 0, 512, 1024, 1536
C_GQ, C_GK, C_GV, C_GG = 2048, 2304, 2560, 3072
C_GA = 3584
IN_COLS = C_GA + GATE_RANK
W_BLOCK = 512

KM_H, KM_G = 0, HGRN_WIDTH
KE_H, KE_G = HGRN_WIDTH + GLA_QK_WIDTH, 2 * HGRN_WIDTH + GLA_QK_WIDTH
KN_COLS = KE_G + GLA_WIDTH
DEC_COLS = HGRN_WIDTH + GLA_QK_WIDTH

HEAD_TABLE = tuple(
    [(KM_H + LANES * h, KE_H + LANES * h, LANES * h) for h in range(HEADS)]
    + [(KM_G + LANES * (h // 2), KE_G + LANES * h, HGRN_WIDTH + LANES * (h // 2))
       for h in range(HEADS)])

V7X_VMEM_LIMIT = 56 * 1024 * 1024

ROWS_INPROJ = 512
ROWS_MIXER = 1024
ROWS_FFN = 512


def _rmsnorm(x, g):
    return x * lax.rsqrt(jnp.mean(x * x, axis=-1, keepdims=True) + NORM_EPS) * g


def _dot(a, b):
    return jnp.dot(a.astype(BF16), b.astype(BF16), preferred_element_type=F32)


def _dot_nt(a, b):
    return lax.dot_general(a.astype(BF16), b.astype(BF16), (((1,), (1,)), ((), ())),
                           preferred_element_type=F32)


def _dot_tn(a, b):
    return lax.dot_general(a.astype(BF16), b.astype(BF16), (((0,), (0,)), ((), ())),
                           preferred_element_type=F32)


def _sigmoid(x):
    return 0.5 * jnp.tanh(0.5 * x) + 0.5


def _silu(x):
    return x * _sigmoid(x)


def _log_sigmoid(x):
    return jnp.minimum(x, 0.0) - jnp.log1p(jnp.exp(-jnp.abs(x)))


def _iota2(n, axis):
    return lax.broadcasted_iota(jnp.int32, (n, n), axis)


def _cumsum_rows(g, tri2_bf16):
    g_hi = g.astype(BF16)
    g_lo = (g - g_hi.astype(F32)).astype(BF16)
    return jnp.dot(tri2_bf16, jnp.concatenate([g_hi, g_lo], axis=0),
                   preferred_element_type=F32)


def _own_half(x, h):
    lane = lax.broadcasted_iota(jnp.int32, x.shape, 1)
    return jnp.where((lane >= GLA_K) == bool(h % 2), x, 0.0)


def _project(x_ref, g_ref, w_ref, wga_ref, wup_ref, bup_ref, lbt_ref, kn_ref, qt_ref, vt_ref,
             gate_ref, dec_ref, chunk, transposed):
    n_chunks = x_ref.shape[0] // chunk
    u = _rmsnorm(x_ref[...], g_ref[...]).astype(BF16)
    tri = (_iota2(chunk, 0) >= _iota2(chunk, 1)).astype(BF16)
    tri = jnp.concatenate([tri, tri], axis=1)
    mid = chunk // 2 - 1

    def proj(col, width):
        blk, off = divmod(col, W_BLOCK)
        assert off + width <= W_BLOCK
        return jnp.dot(u, w_ref[blk, :, off:off + width], preferred_element_type=F32)

    def decay_factors(g, c):
        b = _cumsum_rows(g[c * chunk:(c + 1) * chunk], tri)
        b_mid = b[mid:mid + 1, :]
        b_last = b[chunk - 1:, :]
        return (jnp.exp2(b - b_mid), jnp.exp2(b_mid - b), jnp.exp2(b_last - b),
                jnp.exp2(b_mid), jnp.exp2(b_last))

    def put_tile(ref, c, i, tile):
        if transposed:
            ref[c, LANES * i:LANES * (i + 1), :] = tile.T
        else:
            ref[c, :, LANES * i:LANES * (i + 1)] = tile

    t0 = lbt_ref[0:1, :]
    t1 = lbt_ref[1:2, :]
    m = jnp.maximum(t0, t1)
    e0 = jnp.exp(t0 - m)
    lb = e0 / (e0 + jnp.exp(t1 - m))

    def value_piece(col, first_head):
        def emit():
            v = proj(col, 2 * LANES).astype(BF16)
            for c in range(n_chunks):
                for h in range(2):
                    put_tile(vt_ref, c, first_head + h,
                             v[c * chunk:(c + 1) * chunk, LANES * h:LANES * (h + 1)])
        return emit

    def gate_piece(col, out_col):
        def emit():
            gate_ref[:, out_col:out_col + 2 * LANES] = proj(col, 2 * LANES)
        return emit

    light = [value_piece(C_HI, 0), value_piece(C_HI + 2 * LANES, 2),
             value_piece(C_GV, HEADS), value_piece(C_GV + 2 * LANES, HEADS + 2),
             gate_piece(C_HG, 0), gate_piece(C_HG + 2 * LANES, 2 * LANES),
             gate_piece(C_GG, HGRN_WIDTH), gate_piece(C_GG + 2 * LANES, HGRN_WIDTH + 2 * LANES)]

    def next_light():
        if light:
            light.pop(0)()

    hf = proj(C_HF, HGRN_WIDTH)
    hq = proj(C_HQ, HGRN_WIDTH)
    ga_t = _dot_nt(wga_ref[...], u)
    gq = proj(C_GQ, GLA_QK_WIDTH)
    gk = proj(C_GK, GLA_QK_WIDTH)

    f = 0.5 * (1.0 + lb) + (0.5 * (1.0 - lb)) * jnp.tanh(0.5 * hf)
    g = jnp.log2(f)
    k = 1.0 - f
    q = _silu(hq)
    for c in range(n_chunks):
        rows = slice(c * chunk, (c + 1) * chunk)
        e_q, e_k, e_end, e_mid, e_last = decay_factors(g, c)
        kn_ref[rows, KM_H:KM_H + HGRN_WIDTH] = (k[rows] * e_k).astype(BF16)
        kn_ref[rows, KE_H:KE_H + HGRN_WIDTH] = (k[rows] * e_end).astype(BF16)
        qm = (q[rows] * e_q).astype(BF16)
        for h in range(HEADS):
            put_tile(qt_ref, c, h, qm[:, LANES * h:LANES * (h + 1)])
        dec_ref[c, :, 0:HGRN_WIDTH] = e_mid
        dec_ref[c, :, DEC_COLS:DEC_COLS + HGRN_WIDTH] = e_last
        next_light()

    logits = _dot_tn(ga_t, wup_ref[...]) + bup_ref[...]
    g = _log_sigmoid(logits) * (LOG2_E / GATE_NORMALIZER)
    q = gq * (GLA_K ** -0.5)
    k = gk
    for c in range(n_chunks):
        rows = slice(c * chunk, (c + 1) * chunk)
        e_q, e_k, e_end, e_mid, e_last = decay_factors(g, c)
        qm = q[rows] * e_q
        ke = k[rows] * e_end
        kn_ref[rows, KM_G:KM_G + GLA_QK_WIDTH] = (k[rows] * e_k).astype(BF16)
        for h in range(HEADS):
            pair = slice(LANES * (h // 2), LANES * (h // 2 + 1))
            put_tile(qt_ref, c, HEADS + h, _own_half(qm[:, pair], h).astype(BF16))
            kn_ref[rows, KE_G + LANES * h:KE_G + LANES * (h + 1)] = (
                _own_half(ke[:, pair], h).astype(BF16))
        dec_ref[c, :, HGRN_WIDTH:DEC_COLS] = e_mid
        dec_ref[c, :, DEC_COLS + HGRN_WIDTH:2 * DEC_COLS] = e_last
        next_light()
    while light:
        next_light()


def _meta_kernel(x_ref, g_ref, wt_ref, wgat_ref, wup_ref, bup_ref, lbt_ref,
                 w_ref, wga_ref, kn_ref, q_ref, v_ref, gate_ref, dec_ref, w_scr):
    j = pl.program_id(0)
    n_blocks = w_scr.shape[0]

    @pl.when(j < n_blocks)
    def _():
        blk = wt_ref[...].T.astype(BF16)
        w_ref[0] = blk
        w_scr[j] = blk

    @pl.when(j == n_blocks)
    def _():
        wga_ref[...] = wgat_ref[...].astype(BF16)
        _project(x_ref, g_ref, w_scr, wga_ref, wup_ref, bup_ref, lbt_ref, kn_ref, q_ref, v_ref,
                 gate_ref, dec_ref, N_META, transposed=False)


def _meta(meta_tokens, g, w_in_t, w_up, b_up, lb_table):
    n_blocks = C_GA // W_BLOCK
    full = lambda a: pl.BlockSpec(a.shape, lambda j: (0,) * len(a.shape))
    last = n_blocks - 1
    out_shapes = [jax.ShapeDtypeStruct((n_blocks, D_MODEL, W_BLOCK), BF16),
                  jax.ShapeDtypeStruct((GATE_RANK, D_MODEL), BF16),
                  jax.ShapeDtypeStruct((N_META, KN_COLS), BF16),
                  jax.ShapeDtypeStruct((1, N_META, MIX_WIDTH), BF16),
                  jax.ShapeDtypeStruct((1, N_META, MIX_WIDTH), BF16),
                  jax.ShapeDtypeStruct((N_META, MIX_WIDTH), F32),
                  jax.ShapeDtypeStruct((1, 1, 2 * DEC_COLS), F32)]
    w_main, w_ga, kn, _, v, _, _ = pl.pallas_call(
        _meta_kernel,
        grid=(n_blocks + 1,),
        in_specs=[
            full(meta_tokens), full(g),
            pl.BlockSpec((W_BLOCK, D_MODEL), lambda j: (jnp.minimum(j, last), 0)),
            pl.BlockSpec((GATE_RANK, D_MODEL), lambda j: (C_GA // GATE_RANK, 0)),
            full(w_up), full(b_up), full(lb_table),
        ],
        out_specs=[pl.BlockSpec((1, D_MODEL, W_BLOCK), lambda j: (jnp.minimum(j, last), 0, 0))]
        + [full(s) for s in out_shapes[1:]],
        out_shape=out_shapes,
        scratch_shapes=[pltpu.VMEM((n_blocks, D_MODEL, W_BLOCK), BF16)],
        compiler_params=pltpu.CompilerParams(
            dimension_semantics=("arbitrary",), vmem_limit_bytes=V7X_VMEM_LIMIT),
        name="meta",
    )(meta_tokens, g, w_in_t, w_in_t, w_up, b_up, lb_table)
    return w_main, w_ga, kn, v


def _inproj_kernel(x_ref, g_ref, w_ref, wga_ref, wup_ref, bup_ref, lbt_ref, wout32_ref, w1_32_ref,
                   w2_32_ref, kn_ref, qt_ref, vt_ref, gate_ref, dec_ref, wout_ref, w1_ref, w2_ref):
    _project(x_ref, g_ref, w_ref, wga_ref, wup_ref, bup_ref, lbt_ref, kn_ref, qt_ref, vt_ref,
             gate_ref, dec_ref, CHUNK, transposed=True)
    wout_ref[...] = wout32_ref[...].astype(BF16)
    w1_ref[...] = w1_32_ref[...].astype(BF16)
    w2_ref[...] = w2_32_ref[...].astype(BF16)


def _inproj(x2d, g, w_main, w_ga, w_up, b_up, lb_table, w_out, w1, w2, rows):
    n = x2d.shape[0]
    steps = n // rows
    chunks = rows // CHUNK
    wout_rows = w_out.shape[0] // steps
    w1_rows = w1.shape[0] // steps
    w2_rows = 2 * w2.shape[0] // steps
    assert wout_rows * steps == w_out.shape[0] and w1_rows * steps == w1.shape[0]
    assert w2_rows * steps == 2 * w2.shape[0]
    const = lambda i: (0, 0)
    tile = lambda i: (i, 0)
    tile3 = lambda i: (i, 0, 0)
    half_pace = lambda i: (i // 2, 0)
    w_specs = [pl.BlockSpec((wout_rows, w_out.shape[1]), tile),
               pl.BlockSpec((w1_rows, w1.shape[1]), tile),
               pl.BlockSpec((w2_rows, w2.shape[1]), half_pace)]
    return pl.pallas_call(
        _inproj_kernel,
        grid=(steps,),
        in_specs=[
            pl.BlockSpec((rows, D_MODEL), tile),
            pl.BlockSpec((1, D_MODEL), const),
            pl.BlockSpec(w_main.shape, lambda i: (0, 0, 0), pipeline_mode=pl.Buffered(1)),
            pl.BlockSpec(w_ga.shape, const),
            pl.BlockSpec((GATE_RANK, GLA_QK_WIDTH), const),
            pl.BlockSpec((1, GLA_QK_WIDTH), const),
            pl.BlockSpec(lb_table.shape, const),
        ] + w_specs,
        out_specs=[
            pl.BlockSpec((rows, KN_COLS), tile),
            pl.BlockSpec((chunks, MIX_WIDTH, CHUNK), tile3),
            pl.BlockSpec((chunks, MIX_WIDTH, CHUNK), tile3),
            pl.BlockSpec((rows, MIX_WIDTH), tile),
            pl.BlockSpec((chunks, 1, 2 * DEC_COLS), tile3),
        ] + w_specs,
        out_shape=[
            jax.ShapeDtypeStruct((n, KN_COLS), BF16),
            jax.ShapeDtypeStruct((n // CHUNK, MIX_WIDTH, CHUNK), BF16),
            jax.ShapeDtypeStruct((n // CHUNK, MIX_WIDTH, CHUNK), BF16),
            jax.ShapeDtypeStruct((n, MIX_WIDTH), F32),
            jax.ShapeDtypeStruct((n // CHUNK, 1, 2 * DEC_COLS), F32),
            jax.ShapeDtypeStruct(w_out.shape, BF16),
            jax.ShapeDtypeStruct(w1.shape, BF16),
            jax.ShapeDtypeStruct(w2.shape, BF16),
        ],
        compiler_params=pltpu.CompilerParams(
            dimension_semantics=("arbitrary",), vmem_limit_bytes=V7X_VMEM_LIMIT),
        name="inproj",
    )(x2d, g, w_main, w_ga, w_up, b_up, lb_table, w_out, w1, w2)


def _mixer_kernel(kn_ref, qt_ref, vt_ref, dec_ref, mkn_ref, mv_ref, hnorm_ref, gnorm_ref,
                  o_ref, st_ref, *, steps_per_seq):
    @pl.when(lax.rem(pl.program_id(0), steps_per_seq) == 0)
    def _():
        for i, (_, ke_col, _) in enumerate(HEAD_TABLE):
            st_ref[i] = _dot_tn(mv_ref[0, :, LANES * i:LANES * (i + 1)],
                                mkn_ref[:, ke_col:ke_col + LANES])

    key_before_query = _iota2(CHUNK, 0) <= _iota2(CHUNK, 1)
    norms = (hnorm_ref[...],) * HEADS + (gnorm_ref[...],) * HEADS

    for c in range(kn_ref.shape[0] // CHUNK):
        rows = slice(c * CHUNK, (c + 1) * CHUNK)
        dec = dec_ref[c]
        first, outs = [], []
        for i, (km_col, _, dec_col) in enumerate(HEAD_TABLE):
            s_mid = st_ref[i] * dec[:, dec_col:dec_col + LANES]
            lhs = jnp.concatenate([kn_ref[rows, km_col:km_col + LANES], s_mid.astype(BF16)],
                                  axis=0)
            first.append(jnp.dot(lhs, qt_ref[c, LANES * i:LANES * (i + 1), :],
                                 preferred_element_type=F32))
        for i in range(N_STATES):
            scores_t = jnp.where(key_before_query, first[i][:CHUNK], 0.0).astype(BF16)
            outs.append(first[i][CHUNK:] + jnp.dot(vt_ref[c, LANES * i:LANES * (i + 1), :],
                                                   scores_t, preferred_element_type=F32))
        for i, (_, ke_col, dec_col) in enumerate(HEAD_TABLE):
            st_ref[i] = (st_ref[i] * dec[:, DEC_COLS + dec_col:DEC_COLS + dec_col + LANES]
                         + jnp.dot(vt_ref[c, LANES * i:LANES * (i + 1), :],
                                   kn_ref[rows, ke_col:ke_col + LANES],
                                   preferred_element_type=F32))
        for i in range(N_STATES):
            o_t = outs[i]
            o_t = o_t * lax.rsqrt(jnp.mean(o_t * o_t, axis=0, keepdims=True) + NORM_EPS)
            o_ref[rows, LANES * i:LANES * (i + 1)] = (o_t.T * norms[i]).astype(o_ref.dtype)


def _mixer(kn, qt, vt, dec, meta_kn, meta_v, hnorm, gnorm, seq, rows):
    n = kn.shape[0]
    chunks = rows // CHUNK
    const = lambda s: (0, 0)
    tile = lambda s: (s, 0)
    tile3 = lambda s: (s, 0, 0)
    return pl.pallas_call(
        functools.partial(_mixer_kernel, steps_per_seq=seq // rows),
        grid=(n // rows,),
        in_specs=[
            pl.BlockSpec((rows, KN_COLS), tile),
            pl.BlockSpec((chunks, MIX_WIDTH, CHUNK), tile3),
            pl.BlockSpec((chunks, MIX_WIDTH, CHUNK), tile3),
            pl.BlockSpec((chunks, 1, 2 * DEC_COLS), tile3),
            pl.BlockSpec(meta_kn.shape, const),
            pl.BlockSpec(meta_v.shape, lambda s: (0, 0, 0)),
            pl.BlockSpec((1, HEAD_V), const),
            pl.BlockSpec((1, HEAD_V), const),
        ],
        out_specs=pl.BlockSpec((rows, MIX_WIDTH), tile),
        out_shape=jax.ShapeDtypeStruct((n, MIX_WIDTH), BF16),
        scratch_shapes=[pltpu.VMEM((N_STATES, HEAD_V, LANES), F32)],
        compiler_params=pltpu.CompilerParams(
            dimension_semantics=("arbitrary",), vmem_limit_bytes=V7X_VMEM_LIMIT),
        name="mixer",
    )(kn, qt, vt, dec, meta_kn, meta_v, hnorm, gnorm)


def _ffn_kernel(x_ref, o_ref, gate_ref, wout_ref, g2_ref, w1_ref, w2_ref, g3_ref, y_ref):
    o = o_ref[...].astype(F32) * _silu(gate_ref[...])
    h = x_ref[...] + _dot(o, wout_ref[...])
    u = _rmsnorm(h, g2_ref[...]).astype(BF16)
    gate = jnp.dot(u, w1_ref[:, :FFN_HIDDEN], preferred_element_type=F32)
    up = jnp.dot(u, w1_ref[:, FFN_HIDDEN:], preferred_element_type=F32)
    h = h + _dot(_silu(gate) * up, w2_ref[...])
    y_ref[...] = _rmsnorm(h, g3_ref[...])


def _ffn(x2d, o, gates, w_out, g2, w1, w2, g3, rows):
    n = x2d.shape[0]
    const = lambda i: (0, 0)
    tile = lambda i: (i, 0)
    resident = functools.partial(pl.BlockSpec, index_map=const, pipeline_mode=pl.Buffered(1))
    return pl.pallas_call(
        _ffn_kernel,
        grid=(n // rows,),
        in_specs=[
            pl.BlockSpec((rows, D_MODEL), tile),
            pl.BlockSpec((rows, MIX_WIDTH), tile),
            pl.BlockSpec((rows, MIX_WIDTH), tile),
            resident((MIX_WIDTH, D_MODEL)),
            pl.BlockSpec((1, D_MODEL), const),
            resident((D_MODEL, 2 * FFN_HIDDEN)),
            resident((FFN_HIDDEN, D_MODEL)),
            pl.BlockSpec((1, D_MODEL), const),
        ],
        out_specs=pl.BlockSpec((rows, D_MODEL), tile),
        out_shape=jax.ShapeDtypeStruct((n, D_MODEL), F32),
        compiler_params=pltpu.CompilerParams(
            dimension_semantics=("arbitrary",), vmem_limit_bytes=V7X_VMEM_LIMIT),
        name="outproj_ffn",
    )(x2d, o, gates, w_out, g2, w1, w2, g3)


def kernel(x, meta_tokens, lb_table, norm_mix_g, w_in, w_gla_gate_up, b_gla_gate, hgrn_norm_g,
           gla_norm_g, w_out, norm_ffn_g, w_ffn_in, w_ffn_out, norm_final_g):
    batch, seq, _ = x.shape
    assert w_in.shape[0] == 1 and lb_table.shape[0] == 2, "single-layer block"
    assert seq % ROWS_MIXER == 0
    x2d = x.reshape(batch * seq, D_MODEL)
    g_mix = norm_mix_g[0][None, :]
    w_up = w_gla_gate_up[0]
    b_up = b_gla_gate[0][None, :]

    w_main, w_ga, meta_kn, meta_v = _meta(meta_tokens, g_mix, w_in[0].T, w_up, b_up, lb_table)
    kn, qt, vt, gates, dec, w_out_b, w1_b, w2_b = _inproj(
        x2d, g_mix, w_main, w_ga, w_up, b_up, lb_table, w_out[0], w_ffn_in[0], w_ffn_out[0],
        ROWS_INPROJ)
    o = _mixer(kn, qt, vt, dec, meta_kn, meta_v, hgrn_norm_g[0][None, :],
               gla_norm_g[0][None, :], seq, ROWS_MIXER)
    y = _ffn(x2d, o, gates, w_out_b, norm_ffn_g[0][None, :], w1_b, w2_b, norm_final_g[None, :],
             ROWS_FFN)
    return y.reshape(batch, seq, D_MODEL)
```

```python
import functools

import jax
import jax.numpy as jnp
from jax import lax
from jax.experimental import pallas as pl
from jax.experimental.pallas import tpu as pltpu

F32 = jnp.float32
BF16 = jnp.bfloat16

D_MODEL = 1024
N_META = 16
CHUNK = 128
HEADS = 4
HEAD_V = 128
HGRN_WIDTH = HEADS * HEAD_V
GLA_K = 64
GLA_QK_WIDTH = HEADS * GLA_K
GLA_WIDTH = HEADS * HEAD_V
MIX_WIDTH = HGRN_WIDTH + GLA_WIDTH
GATE_RANK = 16
GATE_NORMALIZER = 16.0
LOG2_E = 1.4426950408889634
FFN_HIDDEN = 2816
NORM_EPS = 1e-6
LANES = 128
N_STATES = 2 * HEADS

C_HQ, C_HF, C_HI, C_HG = 0, 512, 1024, 1536
C_GQ, C_GK, C_GV, C_GG = 2048, 2304, 2560, 3072
C_GA = 3584
IN_COLS = C_GA + GATE_RANK
W_BLOCK = 512

KM_H, KM_G = 0, HGRN_WIDTH
KE_H, KE_G = HGRN_WIDTH + GLA_QK_WIDTH, 2 * HGRN_WIDTH + GLA_QK_WIDTH
KN_COLS = KE_G + GLA_WIDTH
DEC_COLS = HGRN_WIDTH + GLA_QK_WIDTH

HEAD_TABLE = tuple(
    [(KM_H + LANES * h, KE_H + LANES * h, LANES * h) for h in range(HEADS)]
    + [(KM_G + LANES * (h // 2), KE_G + LANES * h, HGRN_WIDTH + LANES * (h // 2))
       for h in range(HEADS)])

V7X_VMEM_LIMIT = 56 * 1024 * 1024

ROWS_INPROJ = 512
ROWS_MIXER = 1024
ROWS_FFN = 512


def _rmsnorm(x, g):
    return x * lax.rsqrt(jnp.mean(x * x, axis=-1, keepdims=True) + NORM_EPS) * g


def _dot(a, b):
    return jnp.dot(a.astype(BF16), b.astype(BF16), preferred_element_type=F32)


def _dot_nt(a, b):
    return lax.dot_general(a.astype(BF16), b.astype(BF16), (((1,), (1,)), ((), ())),
                           preferred_element_type=F32)


def _dot_tn(a, b):
    return lax.dot_general(a.astype(BF16), b.astype(BF16), (((0,), (0,)), ((), ())),
                           preferred_element_type=F32)


def _sigmoid(x):
    return 0.5 * jnp.tanh(0.5 * x) + 0.5


def _silu(x):
    return x * _sigmoid(x)


def _log_sigmoid(x):
    return jnp.minimum(x, 0.0) - jnp.log1p(jnp.exp(-jnp.abs(x)))


def _iota2(n, axis):
    return lax.broadcasted_iota(jnp.int32, (n, n), axis)


def _cumsum_rows(g, tri2_bf16):
    g_hi = g.astype(BF16)
    g_lo = (g - g_hi.astype(F32)).astype(BF16)
    return jnp.dot(tri2_bf16, jnp.concatenate([g_hi, g_lo], axis=0),
                   preferred_element_type=F32)


def _own_half(x, h):
    lane = lax.broadcasted_iota(jnp.int32, x.shape, 1)
    return jnp.where((lane >= GLA_K) == bool(h % 2), x, 0.0)


def _project(x_ref, g_ref, w_ref, wga_ref, wup_ref, bup_ref, lbt_ref, kn_ref, qt_ref, vt_ref,
             gate_ref, dec_ref, chunk, transposed):
    n_chunks = x_ref.shape[0] // chunk
    u = _rmsnorm(x_ref[...], g_ref[...]).astype(BF16)
    tri = (_iota2(chunk, 0) >= _iota2(chunk, 1)).astype(BF16)
    tri = jnp.concatenate([tri, tri], axis=1)
    mid = chunk // 2 - 1

    def proj(col, width):
        blk, off = divmod(col, W_BLOCK)
        assert off + width <= W_BLOCK
        return jnp.dot(u, w_ref[blk, :, off:off + width], preferred_element_type=F32)

    def decay_factors(g, c):
        b = _cumsum_rows(g[c * chunk:(c + 1) * chunk], tri)
        b_mid = b[mid:mid + 1, :]
        b_last = b[chunk - 1:, :]
        return (jnp.exp2(b - b_mid), jnp.exp2(b_mid - b), jnp.exp2(b_last - b),
                jnp.exp2(b_mid), jnp.exp2(b_last))

    def put_tile(ref, c, i, tile):
        if transposed:
            ref[c, LANES * i:LANES * (i + 1), :] = tile.T
        else:
            ref[c, :, LANES * i:LANES * (i + 1)] = tile

    t0 = lbt_ref[0:1, :]
    t1 = lbt_ref[1:2, :]
    m = jnp.maximum(t0, t1)
    e0 = jnp.exp(t0 - m)
    lb = e0 / (e0 + jnp.exp(t1 - m))

    def value_piece(col, first_head):
        def emit():
            v = proj(col, 2 * LANES).astype(BF16)
            for c in range(n_chunks):
                for h in range(2):
                    put_tile(vt_ref, c, first_head + h,
                             v[c * chunk:(c + 1) * chunk, LANES * h:LANES * (h + 1)])
        return emit

    def gate_piece(col, out_col):
        def emit():
            gate_ref[:, out_col:out_col + 2 * LANES] = proj(col, 2 * LANES)
        return emit

    light = [value_piece(C_HI, 0), value_piece(C_HI + 2 * LANES, 2),
             value_piece(C_GV, HEADS), value_piece(C_GV + 2 * LANES, HEADS + 2),
             gate_piece(C_HG, 0), gate_piece(C_HG + 2 * LANES, 2 * LANES),
             gate_piece(C_GG, HGRN_WIDTH), gate_piece(C_GG + 2 * LANES, HGRN_WIDTH + 2 * LANES)]

    def next_light():
        if light:
            light.pop(0)()

    def ga_part(k):
        cols = slice(k * (D_MODEL // 4), (k + 1) * (D_MODEL // 4))
        return _dot_nt(wga_ref[:, cols], u[:, cols])

    ga_t = ga_part(0)
    hf = proj(C_HF, HGRN_WIDTH)
    ga_t += ga_part(1)
    hq = proj(C_HQ, HGRN_WIDTH)
    ga_t += ga_part(2)
    gq = proj(C_GQ, GLA_QK_WIDTH)
    ga_t += ga_part(3)
    gk = proj(C_GK, GLA_QK_WIDTH)

    f = 0.5 * (1.0 + lb) + (0.5 * (1.0 - lb)) * jnp.tanh(0.5 * hf)
    g = jnp.log2(f)
    k = 1.0 - f
    q = _silu(hq)
    for c in range(n_chunks):
        rows = slice(c * chunk, (c + 1) * chunk)
        e_q, e_k, e_end, e_mid, e_last = decay_factors(g, c)
        kn_ref[rows, KM_H:KM_H + HGRN_WIDTH] = (k[rows] * e_k).astype(BF16)
        kn_ref[rows, KE_H:KE_H + HGRN_WIDTH] = (k[rows] * e_end).astype(BF16)
        qm = (q[rows] * e_q).astype(BF16)
        for h in range(HEADS):
            put_tile(qt_ref, c, h, qm[:, LANES * h:LANES * (h + 1)])
        dec_ref[c, :, 0:HGRN_WIDTH] = e_mid
        dec_ref[c, :, DEC_COLS:DEC_COLS + HGRN_WIDTH] = e_last
        next_light()

    next_light()
    logits = _dot_tn(ga_t, wup_ref[...]) + bup_ref[...]
    g = _log_sigmoid(logits) * (LOG2_E / GATE_NORMALIZER)
    q = gq * (GLA_K ** -0.5)
    k = gk
    for c in range(n_chunks):
        rows = slice(c * chunk, (c + 1) * chunk)
        e_q, e_k, e_end, e_mid, e_last = decay_factors(g, c)
        qm = q[rows] * e_q
        ke = k[rows] * e_end
        kn_ref[rows, KM_G:KM_G + GLA_QK_WIDTH] = (k[rows] * e_k).astype(BF16)
        for h in range(HEADS):
            pair = slice(LANES * (h // 2), LANES * (h // 2 + 1))
            put_tile(qt_ref, c, HEADS + h, _own_half(qm[:, pair], h).astype(BF16))
            kn_ref[rows, KE_G + LANES * h:KE_G + LANES * (h + 1)] = (
                _own_half(ke[:, pair], h).astype(BF16))
        dec_ref[c, :, HGRN_WIDTH:DEC_COLS] = e_mid
        dec_ref[c, :, DEC_COLS + HGRN_WIDTH:2 * DEC_COLS] = e_last
        next_light()
    while light:
        next_light()


def _meta_kernel(x_ref, g_ref, wt_ref, wgat_ref, wup_ref, bup_ref, lbt_ref,
                 w_ref, wga_ref, kn_ref, q_ref, v_ref, gate_ref, dec_ref, w_scr):
    j = pl.program_id(0)
    n_blocks = w_scr.shape[0]

    @pl.when(j < n_blocks)
    def _():
        blk = wt_ref[...].T.astype(BF16)
        w_ref[0] = blk
        w_scr[j] = blk

    @pl.when(j == n_blocks)
    def _():
        wga_ref[...] = wgat_ref[...].astype(BF16)
        _project(x_ref, g_ref, w_scr, wga_ref, wup_ref, bup_ref, lbt_ref, kn_ref, q_ref, v_ref,
                 gate_ref, dec_ref, N_META, transposed=False)


def _meta(meta_tokens, g, w_in_t, w_up, b_up, lb_table):
    n_blocks = C_GA // W_BLOCK
    full = lambda a: pl.BlockSpec(a.shape, lambda j: (0,) * len(a.shape))
    last = n_blocks - 1
    out_shapes = [jax.ShapeDtypeStruct((n_blocks, D_MODEL, W_BLOCK), BF16),
                  jax.ShapeDtypeStruct((GATE_RANK, D_MODEL), BF16),
                  jax.ShapeDtypeStruct((N_META, KN_COLS), BF16),
                  jax.ShapeDtypeStruct((1, N_META, MIX_WIDTH), BF16),
                  jax.ShapeDtypeStruct((1, N_META, MIX_WIDTH), BF16),
                  jax.ShapeDtypeStruct((N_META, MIX_WIDTH), F32),
                  jax.ShapeDtypeStruct((1, 1, 2 * DEC_COLS), F32)]
    w_main, w_ga, kn, _, v, _, _ = pl.pallas_call(
        _meta_kernel,
        grid=(n_blocks + 1,),
        in_specs=[
            full(meta_tokens), full(g),
            pl.BlockSpec((W_BLOCK, D_MODEL), lambda j: (jnp.minimum(j, last), 0)),
            pl.BlockSpec((GATE_RANK, D_MODEL), lambda j: (C_GA // GATE_RANK, 0)),
            full(w_up), full(b_up), full(lb_table),
        ],
        out_specs=[pl.BlockSpec((1, D_MODEL, W_BLOCK), lambda j: (jnp.minimum(j, last), 0, 0))]
        + [full(s) for s in out_shapes[1:]],
        out_shape=out_shapes,
        scratch_shapes=[pltpu.VMEM((n_blocks, D_MODEL, W_BLOCK), BF16)],
        compiler_params=pltpu.CompilerParams(
            dimension_semantics=("arbitrary",), vmem_limit_bytes=V7X_VMEM_LIMIT),
        name="meta",
    )(meta_tokens, g, w_in_t, w_in_t, w_up, b_up, lb_table)
    return w_main, w_ga, kn, v


def _inproj_kernel(x_ref, g_ref, w_ref, wga_ref, wup_ref, bup_ref, lbt_ref, wout32_ref, w1_32_ref,
                   w2_32_ref, kn_ref, qt_ref, vt_ref, gate_ref, dec_ref, wout_ref, w1_ref, w2_ref):
    _project(x_ref, g_ref, w_ref, wga_ref, wup_ref, bup_ref, lbt_ref, kn_ref, qt_ref, vt_ref,
             gate_ref, dec_ref, CHUNK, transposed=True)
    wout_ref[...] = wout32_ref[...].astype(BF16)
    w1_ref[...] = w1_32_ref[...].astype(BF16)
    w2_ref[...] = w2_32_ref[...].astype(BF16)


def _inproj(x2d, g, w_main, w_ga, w_up, b_up, lb_table, w_out, w1, w2, rows):
    n = x2d.shape[0]
    steps = n // rows
    chunks = rows // CHUNK
    wout_rows = w_out.shape[0] // steps
    w1_rows = w1.shape[0] // steps
    w2_rows = 2 * w2.shape[0] // steps
    assert wout_rows * steps == w_out.shape[0] and w1_rows * steps == w1.shape[0]
    assert w2_rows * steps == 2 * w2.shape[0]
    const = lambda i: (0, 0)
    tile = lambda i: (i, 0)
    tile3 = lambda i: (i, 0, 0)
    half_pace = lambda i: (i // 2, 0)
    w_specs = [pl.BlockSpec((wout_rows, w_out.shape[1]), tile),
               pl.BlockSpec((w1_rows, w1.shape[1]), tile),
               pl.BlockSpec((w2_rows, w2.shape[1]), half_pace)]
    return pl.pallas_call(
        _inproj_kernel,
        grid=(steps,),
        in_specs=[
            pl.BlockSpec((rows, D_MODEL), tile),
            pl.BlockSpec((1, D_MODEL), const),
            pl.BlockSpec(w_main.shape, lambda i: (0, 0, 0), pipeline_mode=pl.Buffered(1)),
            pl.BlockSpec(w_ga.shape, const),
            pl.BlockSpec((GATE_RANK, GLA_QK_WIDTH), const),
            pl.BlockSpec((1, GLA_QK_WIDTH), const),
            pl.BlockSpec(lb_table.shape, const),
        ] + w_specs,
        out_specs=[
            pl.BlockSpec((rows, KN_COLS), tile),
            pl.BlockSpec((chunks, MIX_WIDTH, CHUNK), tile3),
            pl.BlockSpec((chunks, MIX_WIDTH, CHUNK), tile3),
            pl.BlockSpec((rows, MIX_WIDTH), tile),
            pl.BlockSpec((chunks, 1, 2 * DEC_COLS), tile3),
        ] + w_specs,
        out_shape=[
            jax.ShapeDtypeStruct((n, KN_COLS), BF16),
            jax.ShapeDtypeStruct((n // CHUNK, MIX_WIDTH, CHUNK), BF16),
            jax.ShapeDtypeStruct((n // CHUNK, MIX_WIDTH, CHUNK), BF16),
            jax.ShapeDtypeStruct((n, MIX_WIDTH), F32),
            jax.ShapeDtypeStruct((n // CHUNK, 1, 2 * DEC_COLS), F32),
            jax.ShapeDtypeStruct(w_out.shape, BF16),
            jax.ShapeDtypeStruct(w1.shape, BF16),
            jax.ShapeDtypeStruct(w2.shape, BF16),
        ],
        compiler_params=pltpu.CompilerParams(
            dimension_semantics=("arbitrary",), vmem_limit_bytes=V7X_VMEM_LIMIT),
        name="inproj",
    )(x2d, g, w_main, w_ga, w_up, b_up, lb_table, w_out, w1, w2)


def _mixer_kernel(kn_ref, qt_ref, vt_ref, dec_ref, mkn_ref, mv_ref, hnorm_ref, gnorm_ref,
                  o_ref, st_ref, *, steps_per_seq):
    @pl.when(lax.rem(pl.program_id(0), steps_per_seq) == 0)
    def _():
        for i, (_, ke_col, _) in enumerate(HEAD_TABLE):
            st_ref[i] = _dot_tn(mv_ref[0, :, LANES * i:LANES * (i + 1)],
                                mkn_ref[:, ke_col:ke_col + LANES])

    key_before_query = _iota2(CHUNK, 0) <= _iota2(CHUNK, 1)
    norms = (hnorm_ref[...],) * HEADS + (gnorm_ref[...],) * HEADS

    for c in range(kn_ref.shape[0] // CHUNK):
        rows = slice(c * CHUNK, (c + 1) * CHUNK)
        dec = dec_ref[c]
        first, outs = [], []
        for i, (km_col, _, dec_col) in enumerate(HEAD_TABLE):
            s_mid = st_ref[i] * dec[:, dec_col:dec_col + LANES]
            lhs = jnp.concatenate([kn_ref[rows, km_col:km_col + LANES], s_mid.astype(BF16)],
                                  axis=0)
            first.append(jnp.dot(lhs, qt_ref[c, LANES * i:LANES * (i + 1), :],
                                 preferred_element_type=F32))
        for i in range(N_STATES):
            scores_t = jnp.where(key_before_query, first[i][:CHUNK], 0.0).astype(BF16)
            outs.append(first[i][CHUNK:] + jnp.dot(vt_ref[c, LANES * i:LANES * (i + 1), :],
                                                   scores_t, preferred_element_type=F32))
        for i, (_, ke_col, dec_col) in enumerate(HEAD_TABLE):
            st_ref[i] = (st_ref[i] * dec[:, DEC_COLS + dec_col:DEC_COLS + dec_col + LANES]
                         + jnp.dot(vt_ref[c, LANES * i:LANES * (i + 1), :],
                                   kn_ref[rows, ke_col:ke_col + LANES],
                                   preferred_element_type=F32))
        for i in range(N_STATES):
            o_t = outs[i]
            o_t = o_t * lax.rsqrt(jnp.mean(o_t * o_t, axis=0, keepdims=True) + NORM_EPS)
            o_ref[rows, LANES * i:LANES * (i + 1)] = (o_t.T * norms[i]).astype(o_ref.dtype)


def _mixer(kn, qt, vt, dec, meta_kn, meta_v, hnorm, gnorm, seq, rows):
    n = kn.shape[0]
    chunks = rows // CHUNK
    const = lambda s: (0, 0)
    tile = lambda s: (s, 0)
    tile3 = lambda s: (s, 0, 0)
    return pl.pallas_call(
        functools.partial(_mixer_kernel, steps_per_seq=seq // rows),
        grid=(n // rows,),
        in_specs=[
            pl.BlockSpec((rows, KN_COLS), tile),
            pl.BlockSpec((chunks, MIX_WIDTH, CHUNK), tile3),
            pl.BlockSpec((chunks, MIX_WIDTH, CHUNK), tile3),
            pl.BlockSpec((chunks, 1, 2 * DEC_COLS), tile3),
            pl.BlockSpec(meta_kn.shape, const),
            pl.BlockSpec(meta_v.shape, lambda s: (0, 0, 0)),
            pl.BlockSpec((1, HEAD_V), const),
            pl.BlockSpec((1, HEAD_V), const),
        ],
        out_specs=pl.BlockSpec((rows, MIX_WIDTH), tile),
        out_shape=jax.ShapeDtypeStruct((n, MIX_WIDTH), BF16),
        scratch_shapes=[pltpu.VMEM((N_STATES, HEAD_V, LANES), F32)],
        compiler_params=pltpu.CompilerParams(
            dimension_semantics=("arbitrary",), vmem_limit_bytes=V7X_VMEM_LIMIT),
        name="mixer",
    )(kn, qt, vt, dec, meta_kn, meta_v, hnorm, gnorm)


def _ffn_kernel(x_ref, o_ref, gate_ref, wout_ref, g2_ref, w1_ref, w2_ref, g3_ref, y_ref):
    o = o_ref[...].astype(F32) * _silu(gate_ref[...])
    h = x_ref[...] + _dot(o, wout_ref[...])
    u = _rmsnorm(h, g2_ref[...]).astype(BF16)
    gate = jnp.dot(u, w1_ref[:, :FFN_HIDDEN], preferred_element_type=F32)
    up = jnp.dot(u, w1_ref[:, FFN_HIDDEN:], preferred_element_type=F32)
    h = h + _dot(_silu(gate) * up, w2_ref[...])
    y_ref[...] = _rmsnorm(h, g3_ref[...])


def _ffn(x2d, o, gates, w_out, g2, w1, w2, g3, rows):
    n = x2d.shape[0]
    const = lambda i: (0, 0)
    tile = lambda i: (i, 0)
    resident = functools.partial(pl.BlockSpec, index_map=const, pipeline_mode=pl.Buffered(1))
    return pl.pallas_call(
        _ffn_kernel,
        grid=(n // rows,),
        in_specs=[
            pl.BlockSpec((rows, D_MODEL), tile),
            pl.BlockSpec((rows, MIX_WIDTH), tile),
            pl.BlockSpec((rows, MIX_WIDTH), tile),
            resident((MIX_WIDTH, D_MODEL)),
            pl.BlockSpec((1, D_MODEL), const),
            resident((D_MODEL, 2 * FFN_HIDDEN)),
            resident((FFN_HIDDEN, D_MODEL)),
            pl.BlockSpec((1, D_MODEL), const),
        ],
        out_specs=pl.BlockSpec((rows, D_MODEL), tile),
        out_shape=jax.ShapeDtypeStruct((n, D_MODEL), F32),
        compiler_params=pltpu.CompilerParams(
            dimension_semantics=("arbitrary",), vmem_limit_bytes=V7X_VMEM_LIMIT),
        name="outproj_ffn",
    )(x2d, o, gates, w_out, g2, w1, w2, g3)


def kernel(x, meta_tokens, lb_table, norm_mix_g, w_in, w_gla_gate_up, b_gla_gate, hgrn_norm_g,
           gla_norm_g, w_out, norm_ffn_g, w_ffn_in, w_ffn_out, norm_final_g):
    batch, seq, _ = x.shape
    assert w_in.shape[0] == 1 and lb_table.shape[0] == 2, "single-layer block"
    assert seq % ROWS_MIXER == 0
    x2d = x.reshape(batch * seq, D_MODEL)
    g_mix = norm_mix_g[0][None, :]
    w_up = w_gla_gate_up[0]
    b_up = b_gla_gate[0][None, :]

    w_main, w_ga, meta_kn, meta_v = _meta(meta_tokens, g_mix, w_in[0].T, w_up, b_up, lb_table)
    kn, qt, vt, gates, dec, w_out_b, w1_b, w2_b = _inproj(
        x2d, g_mix, w_main, w_ga, w_up, b_up, lb_table, w_out[0], w_ffn_in[0], w_ffn_out[0],
        ROWS_INPROJ)
    o = _mixer(kn, qt, vt, dec, meta_kn, meta_v, hgrn_norm_g[0][None, :],
               gla_norm_g[0][None, :], seq, ROWS_MIXER)
    y = _ffn(x2d, o, gates, w_out_b, norm_ffn_g[0][None, :], w1_b, w2_b, norm_final_g[None, :],
             ROWS_FFN)
    return y.reshape(batch, seq, D_MODEL)
```

```python
import functools

import jax
import jax.numpy as jnp
from jax import lax
from jax.experimental import pallas as pl
from jax.experimental.pallas import tpu as pltpu

F32 = jnp.float32
BF16 = jnp.bfloat16

D_MODEL = 1024
N_META = 16
CHUNK = 128
HEADS = 4
HEAD_V = 128
HGRN_WIDTH = HEADS * HEAD_V
GLA_K = 64
GLA_QK_WIDTH = HEADS * GLA_K
GLA_WIDTH = HEADS * HEAD_V
MIX_WIDTH = HGRN_WIDTH + GLA_WIDTH
GATE_RANK = 16
GATE_NORMALIZER = 16.0
LOG2_E = 1.4426950408889634
FFN_HIDDEN = 2816
NORM_EPS = 1e-6
LANES = 128
N_STATES = 2 * HEADS

C_HQ, C_HF, C_HI, C_HG = 0, 512, 1024, 1536
C_GQ, C_GK, C_GV, C_GG = 2048, 2304, 2560, 3072
C_GA = 3584
IN_COLS = C_GA + GATE_RANK
W_BLOCK = 512

KM_H, KM_G = 0, HGRN_WIDTH
KE_H, KE_G = HGRN_WIDTH + GLA_QK_WIDTH, 2 * HGRN_WIDTH + GLA_QK_WIDTH
KN_COLS = KE_G + GLA_WIDTH
DEC_COLS = HGRN_WIDTH + GLA_QK_WIDTH

HEAD_TABLE = tuple(
    [(KM_H + LANES * h, KE_H + LANES * h, LANES * h) for h in range(HEADS)]
    + [(KM_G + LANES * (h // 2), KE_G + LANES * h, HGRN_WIDTH + LANES * (h // 2))
       for h in range(HEADS)])

V7X_VMEM_LIMIT = 56 * 1024 * 1024

ROWS_MIXER = 512
ROWS_FFN = 512


def _rmsnorm(x, g):
    return x * lax.rsqrt(jnp.mean(x * x, axis=-1, keepdims=True) + NORM_EPS) * g


def _dot(a, b):
    return jnp.dot(a.astype(BF16), b.astype(BF16), preferred_element_type=F32)


def _dot_nt(a, b):
    return lax.dot_general(a.astype(BF16), b.astype(BF16), (((1,), (1,)), ((), ())),
                           preferred_element_type=F32)


def _dot_tn(a, b):
    return lax.dot_general(a.astype(BF16), b.astype(BF16), (((0,), (0,)), ((), ())),
                           preferred_element_type=F32)


def _sigmoid(x):
    return 0.5 * jnp.tanh(0.5 * x) + 0.5


def _silu(x):
    return x * _sigmoid(x)


def _log_sigmoid(x):
    return jnp.minimum(x, 0.0) - jnp.log1p(jnp.exp(-jnp.abs(x)))


def _iota2(n, axis):
    return lax.broadcasted_iota(jnp.int32, (n, n), axis)


def _cumsum_rows(g, tri2_bf16):
    g_hi = g.astype(BF16)
    g_lo = (g - g_hi.astype(F32)).astype(BF16)
    return jnp.dot(tri2_bf16, jnp.concatenate([g_hi, g_lo], axis=0),
                   preferred_element_type=F32)


def _own_half(x, h):
    lane = lax.broadcasted_iota(jnp.int32, x.shape, 1)
    return jnp.where((lane >= GLA_K) == bool(h % 2), x, 0.0)


def _project(x_ref, g_ref, w_ref, wga_ref, wup_ref, bup_ref, lbt_ref, kn_ref, qt_ref, vt_ref,
             gate_ref, dec_ref, chunk, transposed, between=None):
    n_chunks = x_ref.shape[0] // chunk
    u = _rmsnorm(x_ref[...], g_ref[...]).astype(BF16)
    tri = (_iota2(chunk, 0) >= _iota2(chunk, 1)).astype(BF16)
    tri = jnp.concatenate([tri, tri], axis=1)
    mid = chunk // 2 - 1

    def proj(col, width):
        blk, off = divmod(col, W_BLOCK)
        assert off + width <= W_BLOCK
        return jnp.dot(u, w_ref[blk, :, off:off + width], preferred_element_type=F32)

    def decay_factors(g, c):
        b = _cumsum_rows(g[c * chunk:(c + 1) * chunk], tri)
        b_mid = b[mid:mid + 1, :]
        b_last = b[chunk - 1:, :]
        return (jnp.exp2(b - b_mid), jnp.exp2(b_mid - b), jnp.exp2(b_last - b),
                jnp.exp2(b_mid), jnp.exp2(b_last))

    def put_tile(ref, c, i, tile):
        if transposed:
            ref[c, LANES * i:LANES * (i + 1), :] = tile.T
        else:
            ref[c, :, LANES * i:LANES * (i + 1)] = tile

    t0 = lbt_ref[0:1, :]
    t1 = lbt_ref[1:2, :]
    m = jnp.maximum(t0, t1)
    e0 = jnp.exp(t0 - m)
    lb = e0 / (e0 + jnp.exp(t1 - m))

    def value_piece(col, first_head):
        def emit():
            v = proj(col, 2 * LANES).astype(BF16)
            for c in range(n_chunks):
                for h in range(2):
                    put_tile(vt_ref, c, first_head + h,
                             v[c * chunk:(c + 1) * chunk, LANES * h:LANES * (h + 1)])
        return emit

    def gate_piece(col, out_col):
        def emit():
            gate_ref[:, out_col:out_col + 2 * LANES] = proj(col, 2 * LANES)
        return emit

    light = [value_piece(C_HI, 0), value_piece(C_HI + 2 * LANES, 2),
             value_piece(C_GV, HEADS), value_piece(C_GV + 2 * LANES, HEADS + 2),
             gate_piece(C_HG, 0), gate_piece(C_HG + 2 * LANES, 2 * LANES),
             gate_piece(C_GG, HGRN_WIDTH), gate_piece(C_GG + 2 * LANES, HGRN_WIDTH + 2 * LANES)]

    def pause():
        if between is not None:
            between()

    def next_light():
        if light:
            light.pop(0)()
        pause()

    def ga_part(k):
        cols = slice(k * (D_MODEL // 4), (k + 1) * (D_MODEL // 4))
        return _dot_nt(wga_ref[:, cols], u[:, cols])

    ga_t = ga_part(0)
    hf = proj(C_HF, HGRN_WIDTH)
    ga_t += ga_part(1)
    hq = proj(C_HQ, HGRN_WIDTH)
    ga_t += ga_part(2)
    gq = proj(C_GQ, GLA_QK_WIDTH)
    ga_t += ga_part(3)
    gk = proj(C_GK, GLA_QK_WIDTH)
    pause()

    f = 0.5 * (1.0 + lb) + (0.5 * (1.0 - lb)) * jnp.tanh(0.5 * hf)
    g = jnp.log2(f)
    k = 1.0 - f
    q = _silu(hq)
    pause()
    for c in range(n_chunks):
        rows = slice(c * chunk, (c + 1) * chunk)
        e_q, e_k, e_end, e_mid, e_last = decay_factors(g, c)
        kn_ref[rows, KM_H:KM_H + HGRN_WIDTH] = (k[rows] * e_k).astype(BF16)
        kn_ref[rows, KE_H:KE_H + HGRN_WIDTH] = (k[rows] * e_end).astype(BF16)
        qm = (q[rows] * e_q).astype(BF16)
        for h in range(HEADS):
            put_tile(qt_ref, c, h, qm[:, LANES * h:LANES * (h + 1)])
        dec_ref[c, :, 0:HGRN_WIDTH] = e_mid
        dec_ref[c, :, DEC_COLS:DEC_COLS + HGRN_WIDTH] = e_last
        next_light()

    next_light()
    logits = _dot_tn(ga_t, wup_ref[...]) + bup_ref[...]
    g = _log_sigmoid(logits) * (LOG2_E / GATE_NORMALIZER)
    q = gq * (GLA_K ** -0.5)
    k = gk
    for c in range(n_chunks):
        rows = slice(c * chunk, (c + 1) * chunk)
        e_q, e_k, e_end, e_mid, e_last = decay_factors(g, c)
        qm = q[rows] * e_q
        ke = k[rows] * e_end
        kn_ref[rows, KM_G:KM_G + GLA_QK_WIDTH] = (k[rows] * e_k).astype(BF16)
        for h in range(HEADS):
            pair = slice(LANES * (h // 2), LANES * (h // 2 + 1))
            put_tile(qt_ref, c, HEADS + h, _own_half(qm[:, pair], h).astype(BF16))
            kn_ref[rows, KE_G + LANES * h:KE_G + LANES * (h + 1)] = (
                _own_half(ke[:, pair], h).astype(BF16))
        dec_ref[c, :, HGRN_WIDTH:DEC_COLS] = e_mid
        dec_ref[c, :, DEC_COLS + HGRN_WIDTH:2 * DEC_COLS] = e_last
        next_light()
    while light:
        next_light()


def _meta_kernel(x_ref, g_ref, wt_ref, wgat_ref, wup_ref, bup_ref, lbt_ref,
                 w_ref, wga_ref, kn_ref, q_ref, v_ref, gate_ref, dec_ref, w_scr):
    j = pl.program_id(0)
    n_blocks = w_scr.shape[0]

    @pl.when(j < n_blocks)
    def _():
        blk = wt_ref[...].T.astype(BF16)
        w_ref[0] = blk
        w_scr[j] = blk

    @pl.when(j == n_blocks)
    def _():
        wga_ref[...] = wgat_ref[...].astype(BF16)
        _project(x_ref, g_ref, w_scr, wga_ref, wup_ref, bup_ref, lbt_ref, kn_ref, q_ref, v_ref,
                 gate_ref, dec_ref, N_META, transposed=False)


def _meta(meta_tokens, g, w_in_t, w_up, b_up, lb_table):
    n_blocks = C_GA // W_BLOCK
    full = lambda a: pl.BlockSpec(a.shape, lambda j: (0,) * len(a.shape))
    last = n_blocks - 1
    out_shapes = [jax.ShapeDtypeStruct((n_blocks, D_MODEL, W_BLOCK), BF16),
                  jax.ShapeDtypeStruct((GATE_RANK, D_MODEL), BF16),
                  jax.ShapeDtypeStruct((N_META, KN_COLS), BF16),
                  jax.ShapeDtypeStruct((1, N_META, MIX_WIDTH), BF16),
                  jax.ShapeDtypeStruct((1, N_META, MIX_WIDTH), BF16),
                  jax.ShapeDtypeStruct((N_META, MIX_WIDTH), F32),
                  jax.ShapeDtypeStruct((1, 1, 2 * DEC_COLS), F32)]
    w_main, w_ga, kn, _, v, _, _ = pl.pallas_call(
        _meta_kernel,
        grid=(n_blocks + 1,),
        in_specs=[
            full(meta_tokens), full(g),
            pl.BlockSpec((W_BLOCK, D_MODEL), lambda j: (jnp.minimum(j, last), 0)),
            pl.BlockSpec((GATE_RANK, D_MODEL), lambda j: (C_GA // GATE_RANK, 0)),
            full(w_up), full(b_up), full(lb_table),
        ],
        out_specs=[pl.BlockSpec((1, D_MODEL, W_BLOCK), lambda j: (jnp.minimum(j, last), 0, 0))]
        + [full(s) for s in out_shapes[1:]],
        out_shape=out_shapes,
        scratch_shapes=[pltpu.VMEM((n_blocks, D_MODEL, W_BLOCK), BF16)],
        compiler_params=pltpu.CompilerParams(
            dimension_semantics=("arbitrary",), vmem_limit_bytes=V7X_VMEM_LIMIT),
        name="meta",
    )(meta_tokens, g, w_in_t, w_in_t, w_up, b_up, lb_table)
    return w_main, w_ga, kn, v


def _seed_state(mkn_ref, mv_ref, st_ref):
    for i, (_, ke_col, _) in enumerate(HEAD_TABLE):
        st_ref[i] = _dot_tn(mv_ref[0, :, LANES * i:LANES * (i + 1)],
                            mkn_ref[:, ke_col:ke_col + LANES])


def _recurrence_steps(kn_ref, qt_ref, vt_ref, dec_ref, hnorm_ref, gnorm_ref, o_ref, st_ref):
    key_before_query = _iota2(CHUNK, 0) <= _iota2(CHUNK, 1)
    norms = (hnorm_ref[...],) * HEADS + (gnorm_ref[...],) * HEADS
    for c in range(kn_ref.shape[0] // CHUNK):
        rows = slice(c * CHUNK, (c + 1) * CHUNK)
        dec = dec_ref[c]
        first, outs = [], []
        for i, (km_col, _, dec_col) in enumerate(HEAD_TABLE):
            s_mid = st_ref[i] * dec[:, dec_col:dec_col + LANES]
            lhs = jnp.concatenate([kn_ref[rows, km_col:km_col + LANES], s_mid.astype(BF16)],
                                  axis=0)
            first.append(jnp.dot(lhs, qt_ref[c, LANES * i:LANES * (i + 1), :],
                                 preferred_element_type=F32))
        yield
        for i in range(N_STATES):
            scores_t = jnp.where(key_before_query, first[i][:CHUNK], 0.0).astype(BF16)
            outs.append(first[i][CHUNK:] + jnp.dot(vt_ref[c, LANES * i:LANES * (i + 1), :],
                                                   scores_t, preferred_element_type=F32))
        yield
        for i, (_, ke_col, dec_col) in enumerate(HEAD_TABLE):
            st_ref[i] = (st_ref[i] * dec[:, DEC_COLS + dec_col:DEC_COLS + dec_col + LANES]
                         + jnp.dot(vt_ref[c, LANES * i:LANES * (i + 1), :],
                                   kn_ref[rows, ke_col:ke_col + LANES],
                                   preferred_element_type=F32))
        yield
        for i in range(N_STATES):
            o_t = outs[i]
            o_t = o_t * lax.rsqrt(jnp.mean(o_t * o_t, axis=0, keepdims=True) + NORM_EPS)
            o_ref[rows, LANES * i:LANES * (i + 1)] = (o_t.T * norms[i]).astype(o_ref.dtype)
        yield


STAGES_PER_PAUSE = (1, 1, 2, 1, 2, 1, 1, 2, 1, 2, 2)


def _fused_kernel(x_ref, g_ref, w_ref, wga_ref, wup_ref, bup_ref, lbt_ref, wout32_ref, w1_32_ref,
                  w2_32_ref, mkn_ref, mv_ref, hnorm_ref, gnorm_ref,
                  gate_ref, o_ref, wout_ref, w1_ref, w2_ref,
                  kn_scr, qt_scr, vt_scr, dec_scr, st_ref, *, n_tiles, steps_per_seq):
    s = pl.program_id(0)

    wout_ref[...] = wout32_ref[...].astype(BF16)
    w1_ref[...] = w1_32_ref[...].astype(BF16)
    w2_ref[...] = w2_32_ref[...].astype(BF16)

    @pl.when(lax.rem(s + steps_per_seq - 1, steps_per_seq) == 0)
    def _():
        _seed_state(mkn_ref, mv_ref, st_ref)

    def project(fill, between=None):
        _project(x_ref, g_ref, w_ref, wga_ref, wup_ref, bup_ref, lbt_ref, kn_scr.at[fill],
                 qt_scr.at[fill], vt_scr.at[fill], gate_ref, dec_scr.at[fill], CHUNK,
                 transposed=True, between=between)

    def recurrence(drain):
        return _recurrence_steps(kn_scr.at[drain], qt_scr.at[drain], vt_scr.at[drain],
                                 dec_scr.at[drain], hnorm_ref, gnorm_ref, o_ref, st_ref)

    @pl.when(s == 0)
    def _():
        project(0)

    for parity in (0, 1):
        @pl.when((s > 0) & (s < n_tiles) & (lax.rem(s, 2) == parity))
        def _():
            stages = recurrence(1 - parity)
            budget = list(STAGES_PER_PAUSE)

            def between():
                for _ in range(budget.pop(0) if budget else 1):
                    next(stages, None)

            project(parity, between)
            for _ in stages:
                pass

    @pl.when(s == n_tiles)
    def _():
        for _ in recurrence((n_tiles - 1) % 2):
            pass


def _fused(x2d, g, w_main, w_ga, w_up, b_up, lb_table, w_out, w1, w2, meta_kn, meta_v, hnorm,
           gnorm, seq, rows):
    n = x2d.shape[0]
    n_tiles = n // rows
    chunks = rows // CHUNK
    last = n_tiles - 1
    wout_rows = w_out.shape[0] // n_tiles
    w1_rows = w1.shape[0] // n_tiles
    w2_rows = 2 * w2.shape[0] // n_tiles
    assert wout_rows * n_tiles == w_out.shape[0] and w1_rows * n_tiles == w1.shape[0]
    assert w2_rows * n_tiles == 2 * w2.shape[0]
    const = lambda s: (0, 0)
    const3 = lambda s: (0, 0, 0)
    cur = lambda s: (jnp.minimum(s, last), 0)
    prev = lambda s: (jnp.maximum(s - 1, 0), 0)
    half_pace = lambda s: (jnp.minimum(s, last) // 2, 0)
    w_specs = [pl.BlockSpec((wout_rows, w_out.shape[1]), cur),
               pl.BlockSpec((w1_rows, w1.shape[1]), cur),
               pl.BlockSpec((w2_rows, w2.shape[1]), half_pace)]
    return pl.pallas_call(
        functools.partial(_fused_kernel, n_tiles=n_tiles, steps_per_seq=seq // rows),
        grid=(n_tiles + 1,),
        in_specs=[
            pl.BlockSpec((rows, D_MODEL), cur),
            pl.BlockSpec((1, D_MODEL), const),
            pl.BlockSpec(w_main.shape, const3, pipeline_mode=pl.Buffered(1)),
            pl.BlockSpec(w_ga.shape, const),
            pl.BlockSpec((GATE_RANK, GLA_QK_WIDTH), const),
            pl.BlockSpec((1, GLA_QK_WIDTH), const),
            pl.BlockSpec(lb_table.shape, const),
        ] + w_specs + [
            pl.BlockSpec(meta_kn.shape, const),
            pl.BlockSpec(meta_v.shape, const3),
            pl.BlockSpec((1, HEAD_V), const),
            pl.BlockSpec((1, HEAD_V), const),
        ],
        out_specs=[
            pl.BlockSpec((rows, MIX_WIDTH), cur),
            pl.BlockSpec((rows, MIX_WIDTH), prev),
        ] + w_specs,
        out_shape=[
            jax.ShapeDtypeStruct((n, MIX_WIDTH), F32),
            jax.ShapeDtypeStruct((n, MIX_WIDTH), BF16),
            jax.ShapeDtypeStruct(w_out.shape, BF16),
            jax.ShapeDtypeStruct(w1.shape, BF16),
            jax.ShapeDtypeStruct(w2.shape, BF16),
        ],
        scratch_shapes=[pltpu.VMEM((2, rows, KN_COLS), BF16),
                        pltpu.VMEM((2, chunks, MIX_WIDTH, CHUNK), BF16),
                        pltpu.VMEM((2, chunks, MIX_WIDTH, CHUNK), BF16),
                        pltpu.VMEM((2, chunks, 1, 2 * DEC_COLS), F32),
                        pltpu.VMEM((N_STATES, HEAD_V, LANES), F32)],
        compiler_params=pltpu.CompilerParams(
            dimension_semantics=("arbitrary",), vmem_limit_bytes=V7X_VMEM_LIMIT),
        name="mixer",
    )(x2d, g, w_main, w_ga, w_up, b_up, lb_table, w_out, w1, w2, meta_kn, meta_v, hnorm, gnorm)


def _ffn_kernel(x_ref, o_ref, gate_ref, wout_ref, g2_ref, w1_ref, w2_ref, g3_ref, y_ref):
    o = o_ref[...].astype(F32) * _silu(gate_ref[...])
    h = x_ref[...] + _dot(o, wout_ref[...])
    u = _rmsnorm(h, g2_ref[...]).astype(BF16)
    gate = jnp.dot(u, w1_ref[:, :FFN_HIDDEN], preferred_element_type=F32)
    up = jnp.dot(u, w1_ref[:, FFN_HIDDEN:], preferred_element_type=F32)
    h = h + _dot(_silu(gate) * up, w2_ref[...])
    y_ref[...] = _rmsnorm(h, g3_ref[...])


def _ffn(x2d, o, gates, w_out, g2, w1, w2, g3, rows):
    n = x2d.shape[0]
    const = lambda i: (0, 0)
    tile = lambda i: (i, 0)
    resident = functools.partial(pl.BlockSpec, index_map=const, pipeline_mode=pl.Buffered(1))
    return pl.pallas_call(
        _ffn_kernel,
        grid=(n // rows,),
        in_specs=[
            pl.BlockSpec((rows, D_MODEL), tile),
            pl.BlockSpec((rows, MIX_WIDTH), tile),
            pl.BlockSpec((rows, MIX_WIDTH), tile),
            resident((MIX_WIDTH, D_MODEL)),
            pl.BlockSpec((1, D_MODEL), const),
            resident((D_MODEL, 2 * FFN_HIDDEN)),
            resident((FFN_HIDDEN, D_MODEL)),
            pl.BlockSpec((1, D_MODEL), const),
        ],
        out_specs=pl.BlockSpec((rows, D_MODEL), tile),
        out_shape=jax.ShapeDtypeStruct((n, D_MODEL), F32),
        compiler_params=pltpu.CompilerParams(
            dimension_semantics=("arbitrary",), vmem_limit_bytes=V7X_VMEM_LIMIT),
        name="outproj_ffn",
    )(x2d, o, gates, w_out, g2, w1, w2, g3)


def kernel(x, meta_tokens, lb_table, norm_mix_g, w_in, w_gla_gate_up, b_gla_gate, hgrn_norm_g,
           gla_norm_g, w_out, norm_ffn_g, w_ffn_in, w_ffn_out, norm_final_g):
    batch, seq, _ = x.shape
    assert w_in.shape[0] == 1 and lb_table.shape[0] == 2, "single-layer block"
    assert seq % ROWS_MIXER == 0
    x2d = x.reshape(batch * seq, D_MODEL)
    g_mix = norm_mix_g[0][None, :]
    w_up = w_gla_gate_up[0]
    b_up = b_gla_gate[0][None, :]

    w_main, w_ga, meta_kn, meta_v = _meta(meta_tokens, g_mix, w_in[0].T, w_up, b_up, lb_table)
    gates, o, w_out_b, w1_b, w2_b = _fused(
        x2d, g_mix, w_main, w_ga, w_up, b_up, lb_table, w_out[0], w_ffn_in[0], w_ffn_out[0],
        meta_kn, meta_v, hgrn_norm_g[0][None, :], gla_norm_g[0][None, :], seq, ROWS_MIXER)
    y = _ffn(x2d, o, gates, w_out_b, norm_ffn_g[0][None, :], w1_b, w2_b, norm_final_g[None, :],
             ROWS_FFN)
    return y.reshape(batch, seq, D_MODEL)
```

```python
import functools

import jax
import jax.numpy as jnp
from jax import lax
from jax.experimental import pallas as pl
from jax.experimental.pallas import tpu as pltpu

F32 = jnp.float32
BF16 = jnp.bfloat16

D_MODEL = 1024
N_META = 16
CHUNK = 128
HEADS = 4
HEAD_V = 128
HGRN_WIDTH = HEADS * HEAD_V
GLA_K = 64
GLA_QK_WIDTH = HEADS * GLA_K
GLA_WIDTH = HEADS * HEAD_V
MIX_WIDTH = HGRN_WIDTH + GLA_WIDTH
GATE_RANK = 16
GATE_NORMALIZER = 16.0
LOG2_E = 1.4426950408889634
FFN_HIDDEN = 2816
NORM_EPS = 1e-6
LANES = 128
N_STATES = 2 * HEADS

C_HQ, C_HF, C_HI, C_HG = 0, 512, 1024, 1536
C_GQ, C_GK, C_GV, C_GG = 2048, 2304, 2560, 3072
C_GA = 3584
IN_COLS = C_GA + GATE_RANK
W_BLOCK = 512

KM_H, KM_G = 0, HGRN_WIDTH
KE_H, KE_G = HGRN_WIDTH + GLA_QK_WIDTH, 2 * HGRN_WIDTH + GLA_QK_WIDTH
KN_COLS = KE_G + GLA_WIDTH
DEC_COLS = HGRN_WIDTH + GLA_QK_WIDTH

HEAD_TABLE = tuple(
    [(KM_H + LANES * h, KE_H + LANES * h, LANES * h) for h in range(HEADS)]
    + [(KM_G + LANES * (h // 2), KE_G + LANES * h, HGRN_WIDTH + LANES * (h // 2))
       for h in range(HEADS)])

V7X_VMEM_LIMIT = 56 * 1024 * 1024

ROWS_MIXER = 512
ROWS_FFN = 512
FFN_BLOCK = 256


def _rmsnorm(x, g):
    return x * lax.rsqrt(jnp.mean(x * x, axis=-1, keepdims=True) + NORM_EPS) * g


def _dot(a, b):
    return jnp.dot(a.astype(BF16), b.astype(BF16), preferred_element_type=F32)


def _dot_nt(a, b):
    return lax.dot_general(a.astype(BF16), b.astype(BF16), (((1,), (1,)), ((), ())),
                           preferred_element_type=F32)


def _dot_tn(a, b):
    return lax.dot_general(a.astype(BF16), b.astype(BF16), (((0,), (0,)), ((), ())),
                           preferred_element_type=F32)


def _sigmoid(x):
    return 0.5 * jnp.tanh(0.5 * x) + 0.5


def _silu(x):
    return x * _sigmoid(x)


def _log_sigmoid(x):
    return jnp.minimum(x, 0.0) - jnp.log1p(jnp.exp(-jnp.abs(x)))


def _iota2(n, axis):
    return lax.broadcasted_iota(jnp.int32, (n, n), axis)


def _cumsum_rows(g, tri2_bf16):
    g_hi = g.astype(BF16)
    g_lo = (g - g_hi.astype(F32)).astype(BF16)
    return jnp.dot(tri2_bf16, jnp.concatenate([g_hi, g_lo], axis=0),
                   preferred_element_type=F32)


def _own_half(x, h):
    lane = lax.broadcasted_iota(jnp.int32, x.shape, 1)
    return jnp.where((lane >= GLA_K) == bool(h % 2), x, 0.0)


def _project(x_ref, g_ref, w_ref, wga_ref, wup_ref, bup_ref, lbt_ref, kn_ref, qt_ref, vt_ref,
             gate_ref, dec_ref, chunk, transposed, between=None):
    n_chunks = x_ref.shape[0] // chunk
    u = _rmsnorm(x_ref[...], g_ref[...]).astype(BF16)
    tri = (_iota2(chunk, 0) >= _iota2(chunk, 1)).astype(BF16)
    tri = jnp.concatenate([tri, tri], axis=1)
    mid = chunk // 2 - 1

    def proj(col, width):
        blk, off = divmod(col, W_BLOCK)
        assert off + width <= W_BLOCK
        return jnp.dot(u, w_ref[blk, :, off:off + width], preferred_element_type=F32)

    def decay_factors(g, c):
        b = _cumsum_rows(g[c * chunk:(c + 1) * chunk], tri)
        b_mid = b[mid:mid + 1, :]
        b_last = b[chunk - 1:, :]
        return (jnp.exp2(b - b_mid), jnp.exp2(b_mid - b), jnp.exp2(b_last - b),
                jnp.exp2(b_mid), jnp.exp2(b_last))

    def put_tile(ref, c, i, tile):
        if transposed:
            ref[c, LANES * i:LANES * (i + 1), :] = tile.T
        else:
            ref[c, :, LANES * i:LANES * (i + 1)] = tile

    t0 = lbt_ref[0:1, :]
    t1 = lbt_ref[1:2, :]
    m = jnp.maximum(t0, t1)
    e0 = jnp.exp(t0 - m)
    lb = e0 / (e0 + jnp.exp(t1 - m))

    def value_piece(col, first_head):
        def emit():
            v = proj(col, 2 * LANES).astype(BF16)
            for c in range(n_chunks):
                for h in range(2):
                    put_tile(vt_ref, c, first_head + h,
                             v[c * chunk:(c + 1) * chunk, LANES * h:LANES * (h + 1)])
        return emit

    def gate_piece(col, out_col):
        def emit():
            gate_ref[:, out_col:out_col + 2 * LANES] = proj(col, 2 * LANES)
        return emit

    light = [value_piece(C_HI, 0), value_piece(C_HI + 2 * LANES, 2),
             value_piece(C_GV, HEADS), value_piece(C_GV + 2 * LANES, HEADS + 2),
             gate_piece(C_HG, 0), gate_piece(C_HG + 2 * LANES, 2 * LANES),
             gate_piece(C_GG, HGRN_WIDTH), gate_piece(C_GG + 2 * LANES, HGRN_WIDTH + 2 * LANES)]

    def pause():
        if between is not None:
            between()

    def next_light():
        if light:
            light.pop(0)()
        pause()

    def ga_part(k):
        cols = slice(k * (D_MODEL // 4), (k + 1) * (D_MODEL // 4))
        return _dot_nt(wga_ref[:, cols], u[:, cols])

    ga_t = ga_part(0)
    hf = proj(C_HF, HGRN_WIDTH)
    ga_t += ga_part(1)
    hq = proj(C_HQ, HGRN_WIDTH)
    ga_t += ga_part(2)
    gq = proj(C_GQ, GLA_QK_WIDTH)
    ga_t += ga_part(3)
    gk = proj(C_GK, GLA_QK_WIDTH)
    pause()

    f = 0.5 * (1.0 + lb) + (0.5 * (1.0 - lb)) * jnp.tanh(0.5 * hf)
    g = jnp.log2(f)
    k = 1.0 - f
    q = _silu(hq)
    pause()
    for c in range(n_chunks):
        rows = slice(c * chunk, (c + 1) * chunk)
        e_q, e_k, e_end, e_mid, e_last = decay_factors(g, c)
        kn_ref[rows, KM_H:KM_H + HGRN_WIDTH] = (k[rows] * e_k).astype(BF16)
        kn_ref[rows, KE_H:KE_H + HGRN_WIDTH] = (k[rows] * e_end).astype(BF16)
        qm = (q[rows] * e_q).astype(BF16)
        for h in range(HEADS):
            put_tile(qt_ref, c, h, qm[:, LANES * h:LANES * (h + 1)])
        dec_ref[c, :, 0:HGRN_WIDTH] = e_mid
        dec_ref[c, :, DEC_COLS:DEC_COLS + HGRN_WIDTH] = e_last
        next_light()

    next_light()
    logits = _dot_tn(ga_t, wup_ref[...]) + bup_ref[...]
    g = _log_sigmoid(logits) * (LOG2_E / GATE_NORMALIZER)
    q = gq * (GLA_K ** -0.5)
    k = gk
    for c in range(n_chunks):
        rows = slice(c * chunk, (c + 1) * chunk)
        e_q, e_k, e_end, e_mid, e_last = decay_factors(g, c)
        qm = q[rows] * e_q
        ke = k[rows] * e_end
        kn_ref[rows, KM_G:KM_G + GLA_QK_WIDTH] = (k[rows] * e_k).astype(BF16)
        for h in range(HEADS):
            pair = slice(LANES * (h // 2), LANES * (h // 2 + 1))
            put_tile(qt_ref, c, HEADS + h, _own_half(qm[:, pair], h).astype(BF16))
            kn_ref[rows, KE_G + LANES * h:KE_G + LANES * (h + 1)] = (
                _own_half(ke[:, pair], h).astype(BF16))
        dec_ref[c, :, HGRN_WIDTH:DEC_COLS] = e_mid
        dec_ref[c, :, DEC_COLS + HGRN_WIDTH:2 * DEC_COLS] = e_last
        next_light()
    while light:
        next_light()


def _meta_kernel(x_ref, g_ref, wt_ref, wgat_ref, wup_ref, bup_ref, lbt_ref,
                 w_ref, wga_ref, kn_ref, q_ref, v_ref, gate_ref, dec_ref, w_scr):
    j = pl.program_id(0)
    n_blocks = w_scr.shape[0]

    @pl.when(j < n_blocks)
    def _():
        blk = wt_ref[...].T.astype(BF16)
        w_ref[0] = blk
        w_scr[j] = blk

    @pl.when(j == n_blocks)
    def _():
        wga_ref[...] = wgat_ref[...].astype(BF16)
        _project(x_ref, g_ref, w_scr, wga_ref, wup_ref, bup_ref, lbt_ref, kn_ref, q_ref, v_ref,
                 gate_ref, dec_ref, N_META, transposed=False)


def _meta(meta_tokens, g, w_in_t, w_up, b_up, lb_table):
    n_blocks = C_GA // W_BLOCK
    full = lambda a: pl.BlockSpec(a.shape, lambda j: (0,) * len(a.shape))
    last = n_blocks - 1
    out_shapes = [jax.ShapeDtypeStruct((n_blocks, D_MODEL, W_BLOCK), BF16),
                  jax.ShapeDtypeStruct((GATE_RANK, D_MODEL), BF16),
                  jax.ShapeDtypeStruct((N_META, KN_COLS), BF16),
                  jax.ShapeDtypeStruct((1, N_META, MIX_WIDTH), BF16),
                  jax.ShapeDtypeStruct((1, N_META, MIX_WIDTH), BF16),
                  jax.ShapeDtypeStruct((N_META, MIX_WIDTH), F32),
                  jax.ShapeDtypeStruct((1, 1, 2 * DEC_COLS), F32)]
    w_main, w_ga, kn, _, v, _, _ = pl.pallas_call(
        _meta_kernel,
        grid=(n_blocks + 1,),
        in_specs=[
            full(meta_tokens), full(g),
            pl.BlockSpec((W_BLOCK, D_MODEL), lambda j: (jnp.minimum(j, last), 0)),
            pl.BlockSpec((GATE_RANK, D_MODEL), lambda j: (C_GA // GATE_RANK, 0)),
            full(w_up), full(b_up), full(lb_table),
        ],
        out_specs=[pl.BlockSpec((1, D_MODEL, W_BLOCK), lambda j: (jnp.minimum(j, last), 0, 0))]
        + [full(s) for s in out_shapes[1:]],
        out_shape=out_shapes,
        scratch_shapes=[pltpu.VMEM((n_blocks, D_MODEL, W_BLOCK), BF16)],
        compiler_params=pltpu.CompilerParams(
            dimension_semantics=("arbitrary",), vmem_limit_bytes=V7X_VMEM_LIMIT),
        name="meta",
    )(meta_tokens, g, w_in_t, w_in_t, w_up, b_up, lb_table)
    return w_main, w_ga, kn, v


def _seed_state(mkn_ref, mv_ref, st_ref):
    for i, (_, ke_col, _) in enumerate(HEAD_TABLE):
        st_ref[i] = _dot_tn(mv_ref[0, :, LANES * i:LANES * (i + 1)],
                            mkn_ref[:, ke_col:ke_col + LANES])


def _recurrence_steps(kn_ref, qt_ref, vt_ref, dec_ref, hnorm_ref, gnorm_ref, o_ref, st_ref):
    key_before_query = _iota2(CHUNK, 0) <= _iota2(CHUNK, 1)
    norms = (hnorm_ref[...],) * HEADS + (gnorm_ref[...],) * HEADS
    for c in range(kn_ref.shape[0] // CHUNK):
        rows = slice(c * CHUNK, (c + 1) * CHUNK)
        dec = dec_ref[c]
        first, outs = [], []
        for i, (km_col, _, dec_col) in enumerate(HEAD_TABLE):
            s_mid = st_ref[i] * dec[:, dec_col:dec_col + LANES]
            lhs = jnp.concatenate([kn_ref[rows, km_col:km_col + LANES], s_mid.astype(BF16)],
                                  axis=0)
            first.append(jnp.dot(lhs, qt_ref[c, LANES * i:LANES * (i + 1), :],
                                 preferred_element_type=F32))
        yield
        for i in range(N_STATES):
            scores_t = jnp.where(key_before_query, first[i][:CHUNK], 0.0).astype(BF16)
            outs.append(first[i][CHUNK:] + jnp.dot(vt_ref[c, LANES * i:LANES * (i + 1), :],
                                                   scores_t, preferred_element_type=F32))
        yield
        for i, (_, ke_col, dec_col) in enumerate(HEAD_TABLE):
            st_ref[i] = (st_ref[i] * dec[:, DEC_COLS + dec_col:DEC_COLS + dec_col + LANES]
                         + jnp.dot(vt_ref[c, LANES * i:LANES * (i + 1), :],
                                   kn_ref[rows, ke_col:ke_col + LANES],
                                   preferred_element_type=F32))
        yield
        for i in range(N_STATES):
            o_t = outs[i]
            o_t = o_t * lax.rsqrt(jnp.mean(o_t * o_t, axis=0, keepdims=True) + NORM_EPS)
            o_ref[rows, LANES * i:LANES * (i + 1)] = (o_t.T * norms[i]).astype(o_ref.dtype)
        yield


STAGES_PER_PAUSE = (1, 1, 2, 1, 2, 1, 1, 2, 1, 2, 2)


def _fused_kernel(x_ref, g_ref, w_ref, wga_ref, wup_ref, bup_ref, lbt_ref, wout32_ref, w1_32_ref,
                  w2_32_ref, mkn_ref, mv_ref, hnorm_ref, gnorm_ref,
                  gate_ref, o_ref, wout_ref, w1_ref, w2_ref,
                  kn_scr, qt_scr, vt_scr, dec_scr, st_ref, *, n_tiles, steps_per_seq):
    s = pl.program_id(0)

    @pl.when(lax.rem(s + steps_per_seq - 1, steps_per_seq) == 0)
    def _():
        _seed_state(mkn_ref, mv_ref, st_ref)

    def project(fill, between=None):
        _project(x_ref, g_ref, w_ref, wga_ref, wup_ref, bup_ref, lbt_ref, kn_scr.at[fill],
                 qt_scr.at[fill], vt_scr.at[fill], gate_ref, dec_scr.at[fill], CHUNK,
                 transposed=True, between=between)
        wout_ref[...] = wout32_ref[...].astype(BF16)
        w1_ref[...] = w1_32_ref[...].astype(BF16)
        w2_ref[...] = w2_32_ref[...].astype(BF16)

    def recurrence(drain):
        return _recurrence_steps(kn_scr.at[drain], qt_scr.at[drain], vt_scr.at[drain],
                                 dec_scr.at[drain], hnorm_ref, gnorm_ref, o_ref, st_ref)

    @pl.when(s == 0)
    def _():
        project(0)

    for parity in (0, 1):
        @pl.when((s > 0) & (s < n_tiles) & (lax.rem(s, 2) == parity))
        def _():
            stages = recurrence(1 - parity)
            budget = list(STAGES_PER_PAUSE)

            def between():
                for _ in range(budget.pop(0) if budget else 1):
                    next(stages, None)

            project(parity, between)
            for _ in stages:
                pass

    @pl.when(s == n_tiles)
    def _():
        for _ in recurrence((n_tiles - 1) % 2):
            pass


def _fused(x2d, g, w_main, w_ga, w_up, b_up, lb_table, w_out, w1, w2, meta_kn, meta_v, hnorm,
           gnorm, seq, rows):
    n = x2d.shape[0]
    n_tiles = n // rows
    chunks = rows // CHUNK
    last = n_tiles - 1
    wout_rows = w_out.shape[0] // n_tiles
    w1_rows = w1.shape[0] // n_tiles
    w2_rows = 2 * w2.shape[0] // n_tiles
    assert wout_rows * n_tiles == w_out.shape[0] and w1_rows * n_tiles == w1.shape[0]
    assert w2_rows * n_tiles == 2 * w2.shape[0]
    const = lambda s: (0, 0)
    const3 = lambda s: (0, 0, 0)
    cur = lambda s: (jnp.minimum(s, last), 0)
    prev = lambda s: (jnp.maximum(s - 1, 0), 0)
    half_pace = lambda s: (jnp.minimum(s, last) // 2, 0)
    w_specs = [pl.BlockSpec((wout_rows, w_out.shape[1]), cur),
               pl.BlockSpec((w1_rows, w1.shape[1]), cur),
               pl.BlockSpec((w2_rows, w2.shape[1]), half_pace)]
    return pl.pallas_call(
        functools.partial(_fused_kernel, n_tiles=n_tiles, steps_per_seq=seq // rows),
        grid=(n_tiles + 1,),
        in_specs=[
            pl.BlockSpec((rows, D_MODEL), cur),
            pl.BlockSpec((1, D_MODEL), const),
            pl.BlockSpec(w_main.shape, const3, pipeline_mode=pl.Buffered(1)),
            pl.BlockSpec(w_ga.shape, const),
            pl.BlockSpec((GATE_RANK, GLA_QK_WIDTH), const),
            pl.BlockSpec((1, GLA_QK_WIDTH), const),
            pl.BlockSpec(lb_table.shape, const),
        ] + w_specs + [
            pl.BlockSpec(meta_kn.shape, const),
            pl.BlockSpec(meta_v.shape, const3),
            pl.BlockSpec((1, HEAD_V), const),
            pl.BlockSpec((1, HEAD_V), const),
        ],
        out_specs=[
            pl.BlockSpec((rows, MIX_WIDTH), cur),
            pl.BlockSpec((rows, MIX_WIDTH), prev),
        ] + w_specs,
        out_shape=[
            jax.ShapeDtypeStruct((n, MIX_WIDTH), F32),
            jax.ShapeDtypeStruct((n, MIX_WIDTH), BF16),
            jax.ShapeDtypeStruct(w_out.shape, BF16),
            jax.ShapeDtypeStruct(w1.shape, BF16),
            jax.ShapeDtypeStruct(w2.shape, BF16),
        ],
        scratch_shapes=[pltpu.VMEM((2, rows, KN_COLS), BF16),
                        pltpu.VMEM((2, chunks, MIX_WIDTH, CHUNK), BF16),
                        pltpu.VMEM((2, chunks, MIX_WIDTH, CHUNK), BF16),
                        pltpu.VMEM((2, chunks, 1, 2 * DEC_COLS), F32),
                        pltpu.VMEM((N_STATES, HEAD_V, LANES), F32)],
        compiler_params=pltpu.CompilerParams(
            dimension_semantics=("arbitrary",), vmem_limit_bytes=V7X_VMEM_LIMIT),
        name="mixer",
    )(x2d, g, w_main, w_ga, w_up, b_up, lb_table, w_out, w1, w2, meta_kn, meta_v, hnorm, gnorm)


def _ffn_kernel(x_ref, o_ref, gate_ref, wout_ref, g2_ref, w1_ref, w2_ref, g3_ref, y_ref):
    o = o_ref[...].astype(F32) * _silu(gate_ref[...])
    h = x_ref[...] + _dot(o, wout_ref[...])
    u = _rmsnorm(h, g2_ref[...]).astype(BF16)
    ffn = None
    for j in range(0, FFN_HIDDEN, FFN_BLOCK):
        gate = jnp.dot(u, w1_ref[:, j:j + FFN_BLOCK], preferred_element_type=F32)
        up = jnp.dot(u, w1_ref[:, FFN_HIDDEN + j:FFN_HIDDEN + j + FFN_BLOCK],
                     preferred_element_type=F32)
        part = _dot(_silu(gate) * up, w2_ref[j:j + FFN_BLOCK, :])
        ffn = part if ffn is None else ffn + part
    y_ref[...] = _rmsnorm(h + ffn, g3_ref[...])


def _ffn(x2d, o, gates, w_out, g2, w1, w2, g3, rows):
    n = x2d.shape[0]
    const = lambda i: (0, 0)
    tile = lambda i: (i, 0)
    resident = functools.partial(pl.BlockSpec, index_map=const, pipeline_mode=pl.Buffered(1))
    return pl.pallas_call(
        _ffn_kernel,
        grid=(n // rows,),
        in_specs=[
            pl.BlockSpec((rows, D_MODEL), tile),
            pl.BlockSpec((rows, MIX_WIDTH), tile),
            pl.BlockSpec((rows, MIX_WIDTH), tile),
            resident((MIX_WIDTH, D_MODEL)),
            pl.BlockSpec((1, D_MODEL), const),
            resident((D_MODEL, 2 * FFN_HIDDEN)),
            resident((FFN_HIDDEN, D_MODEL)),
            pl.BlockSpec((1, D_MODEL), const),
        ],
        out_specs=pl.BlockSpec((rows, D_MODEL), tile),
        out_shape=jax.ShapeDtypeStruct((n, D_MODEL), F32),
        compiler_params=pltpu.CompilerParams(
            dimension_semantics=("arbitrary",), vmem_limit_bytes=V7X_VMEM_LIMIT),
        name="outproj_ffn",
    )(x2d, o, gates, w_out, g2, w1, w2, g3)


def kernel(x, meta_tokens, lb_table, norm_mix_g, w_in, w_gla_gate_up, b_gla_gate, hgrn_norm_g,
           gla_norm_g, w_out, norm_ffn_g, w_ffn_in, w_ffn_out, norm_final_g):
    batch, seq, _ = x.shape
    assert w_in.shape[0] == 1 and lb_table.shape[0] == 2, "single-layer block"
    assert seq % ROWS_MIXER == 0
    x2d = x.reshape(batch * seq, D_MODEL)
    g_mix = norm_mix_g[0][None, :]
    w_up = w_gla_gate_up[0]
    b_up = b_gla_gate[0][None, :]

    w_main, w_ga, meta_kn, meta_v = _meta(meta_tokens, g_mix, w_in[0].T, w_up, b_up, lb_table)
    gates, o, w_out_b, w1_b, w2_b = _fused(
        x2d, g_mix, w_main, w_ga, w_up, b_up, lb_table, w_out[0], w_ffn_in[0], w_ffn_out[0],
        meta_kn, meta_v, hgrn_norm_g[0][None, :], gla_norm_g[0][None, :], seq, ROWS_MIXER)
    y = _ffn(x2d, o, gates, w_out_b, norm_ffn_g[0][None, :], w1_b, w2_b, norm_final_g[None, :],
             ROWS_FFN)
    return y.reshape(batch, seq, D_MODEL)
```

```python
import functools

import jax
import jax.numpy as jnp
from jax import lax
from jax.experimental import pallas as pl
from jax.experimental.pallas import tpu as pltpu

F32 = jnp.float32
BF16 = jnp.bfloat16

D_MODEL = 1024
N_META = 16
CHUNK = 128
HEADS = 4
HEAD_V = 128
HGRN_WIDTH = HEADS * HEAD_V
GLA_K = 64
GLA_QK_WIDTH = HEADS * GLA_K
GLA_WIDTH = HEADS * HEAD_V
MIX_WIDTH = HGRN_WIDTH + GLA_WIDTH
GATE_RANK = 16
GATE_NORMALIZER = 16.0
LOG2_E = 1.4426950408889634
FFN_HIDDEN = 2816
NORM_EPS = 1e-6
LANES = 128
N_PAIRS = HEADS

C_HQ, C_HF, C_HI, C_HG = 0, 512, 1024, 1536
C_GQ, C_GK, C_GV, C_GG = 2048, 2304, 2560, 3072
C_GA = 3584
IN_COLS = C_GA + GATE_RANK
W_BLOCK = 512

KM_H, KM_G = 0, HGRN_WIDTH
KE_H, KE_G = HGRN_WIDTH + GLA_QK_WIDTH, 2 * HGRN_WIDTH + GLA_QK_WIDTH
KN_COLS = KE_G + GLA_WIDTH
DEC_COLS = HGRN_WIDTH + GLA_QK_WIDTH

HEAD_TABLE = tuple(
    [(KM_H + LANES * h, KE_H + LANES * h, LANES * h) for h in range(HEADS)]
    + [(KM_G + LANES * (h // 2), KE_G + LANES * h, HGRN_WIDTH + LANES * (h // 2))
       for h in range(HEADS)])

V7X_VMEM_LIMIT = 56 * 1024 * 1024

ROWS_MIXER = 512
ROWS_FFN = 512
FFN_BLOCK = 256


def _rmsnorm(x, g):
    return x * lax.rsqrt(jnp.mean(x * x, axis=-1, keepdims=True) + NORM_EPS) * g


def _dot(a, b):
    return jnp.dot(a.astype(BF16), b.astype(BF16), preferred_element_type=F32)


def _dot_nt(a, b):
    return lax.dot_general(a.astype(BF16), b.astype(BF16), (((1,), (1,)), ((), ())),
                           preferred_element_type=F32)


def _dot_tn(a, b):
    return lax.dot_general(a.astype(BF16), b.astype(BF16), (((0,), (0,)), ((), ())),
                           preferred_element_type=F32)


def _sigmoid(x):
    return 0.5 * jnp.tanh(0.5 * x) + 0.5


def _silu(x):
    return x * _sigmoid(x)


def _log_sigmoid(x):
    return jnp.minimum(x, 0.0) - jnp.log1p(jnp.exp(-jnp.abs(x)))


def _iota2(n, axis):
    return lax.broadcasted_iota(jnp.int32, (n, n), axis)


def _cumsum_rows(g, tri2_bf16):
    g_hi = g.astype(BF16)
    g_lo = (g - g_hi.astype(F32)).astype(BF16)
    return jnp.dot(tri2_bf16, jnp.concatenate([g_hi, g_lo], axis=0),
                   preferred_element_type=F32)


def _own_half(x, h):
    lane = lax.broadcasted_iota(jnp.int32, x.shape, 1)
    return jnp.where((lane >= GLA_K) == bool(h % 2), x, 0.0)


def _project(x_ref, g_ref, w_ref, wga_ref, wup_ref, bup_ref, lbt_ref, kn_ref, qt_ref, vt_ref,
             gate_ref, dec_ref, chunk, transposed, between=None):
    n_chunks = x_ref.shape[0] // chunk
    u = _rmsnorm(x_ref[...], g_ref[...]).astype(BF16)
    tri = (_iota2(chunk, 0) >= _iota2(chunk, 1)).astype(BF16)
    tri = jnp.concatenate([tri, tri], axis=1)
    mid = chunk // 2 - 1

    def proj(col, width):
        blk, off = divmod(col, W_BLOCK)
        assert off + width <= W_BLOCK
        return jnp.dot(u, w_ref[blk, :, off:off + width], preferred_element_type=F32)

    def decay_factors(g, c):
        b = _cumsum_rows(g[c * chunk:(c + 1) * chunk], tri)
        b_mid = b[mid:mid + 1, :]
        b_last = b[chunk - 1:, :]
        return (jnp.exp2(b - b_mid), jnp.exp2(b_mid - b), jnp.exp2(b_last - b),
                jnp.exp2(b_mid), jnp.exp2(b_last))

    def put_tile(ref, c, i, tile):
        if transposed:
            ref[c, LANES * i:LANES * (i + 1), :] = tile.T
        else:
            ref[c, :, LANES * i:LANES * (i + 1)] = tile

    t0 = lbt_ref[0:1, :]
    t1 = lbt_ref[1:2, :]
    m = jnp.maximum(t0, t1)
    e0 = jnp.exp(t0 - m)
    lb = e0 / (e0 + jnp.exp(t1 - m))

    def value_piece(col, first_head):
        def emit():
            v = proj(col, 2 * LANES).astype(BF16)
            for c in range(n_chunks):
                for h in range(2):
                    put_tile(vt_ref, c, first_head + h,
                             v[c * chunk:(c + 1) * chunk, LANES * h:LANES * (h + 1)])
        return emit

    def gate_piece(col, out_col):
        def emit():
            gate_ref[:, out_col:out_col + 2 * LANES] = proj(col, 2 * LANES)
        return emit

    light = [value_piece(C_HI, 0), value_piece(C_HI + 2 * LANES, 2),
             value_piece(C_GV, HEADS), value_piece(C_GV + 2 * LANES, HEADS + 2),
             gate_piece(C_HG, 0), gate_piece(C_HG + 2 * LANES, 2 * LANES),
             gate_piece(C_GG, HGRN_WIDTH), gate_piece(C_GG + 2 * LANES, HGRN_WIDTH + 2 * LANES)]

    def pause():
        if between is not None:
            between()

    def next_light():
        if light:
            light.pop(0)()
        pause()

    def ga_part(k):
        cols = slice(k * (D_MODEL // 4), (k + 1) * (D_MODEL // 4))
        return _dot_nt(wga_ref[:, cols], u[:, cols])

    ga_t = ga_part(0)
    hf = proj(C_HF, HGRN_WIDTH)
    ga_t += ga_part(1)
    hq = proj(C_HQ, HGRN_WIDTH)
    ga_t += ga_part(2)
    gq = proj(C_GQ, GLA_QK_WIDTH)
    ga_t += ga_part(3)
    gk = proj(C_GK, GLA_QK_WIDTH)
    pause()

    f = 0.5 * (1.0 + lb) + (0.5 * (1.0 - lb)) * jnp.tanh(0.5 * hf)
    g = jnp.log2(f)
    k = 1.0 - f
    q = _silu(hq)
    pause()
    for c in range(n_chunks):
        rows = slice(c * chunk, (c + 1) * chunk)
        e_q, e_k, e_end, e_mid, e_last = decay_factors(g, c)
        kn_ref[rows, KM_H:KM_H + HGRN_WIDTH] = (k[rows] * e_k).astype(BF16)
        kn_ref[rows, KE_H:KE_H + HGRN_WIDTH] = (k[rows] * e_end).astype(BF16)
        qm = (q[rows] * e_q).astype(BF16)
        for h in range(HEADS):
            put_tile(qt_ref, c, h, qm[:, LANES * h:LANES * (h + 1)])
        dec_ref[c, :, 0:HGRN_WIDTH] = e_mid
        dec_ref[c, :, DEC_COLS:DEC_COLS + HGRN_WIDTH] = e_last
        next_light()

    next_light()
    logits = _dot_tn(ga_t, wup_ref[...]) + bup_ref[...]
    g = _log_sigmoid(logits) * (LOG2_E / GATE_NORMALIZER)
    q = gq * (GLA_K ** -0.5)
    k = gk
    for c in range(n_chunks):
        rows = slice(c * chunk, (c + 1) * chunk)
        e_q, e_k, e_end, e_mid, e_last = decay_factors(g, c)
        qm = q[rows] * e_q
        ke = k[rows] * e_end
        kn_ref[rows, KM_G:KM_G + GLA_QK_WIDTH] = (k[rows] * e_k).astype(BF16)
        for h in range(HEADS):
            pair = slice(LANES * (h // 2), LANES * (h // 2 + 1))
            put_tile(qt_ref, c, HEADS + h, _own_half(qm[:, pair], h).astype(BF16))
            kn_ref[rows, KE_G + LANES * h:KE_G + LANES * (h + 1)] = (
                _own_half(ke[:, pair], h).astype(BF16))
        dec_ref[c, :, HGRN_WIDTH:DEC_COLS] = e_mid
        dec_ref[c, :, DEC_COLS + HGRN_WIDTH:2 * DEC_COLS] = e_last
        next_light()
    while light:
        next_light()


def _meta_kernel(x_ref, g_ref, wt_ref, wgat_ref, wup_ref, bup_ref, lbt_ref,
                 w_ref, wga_ref, kn_ref, q_ref, v_ref, gate_ref, dec_ref, w_scr):
    j = pl.program_id(0)
    n_blocks = w_scr.shape[0]

    @pl.when(j < n_blocks)
    def _():
        blk = wt_ref[...].T.astype(BF16)
        w_ref[0] = blk
        w_scr[j] = blk

    @pl.when(j == n_blocks)
    def _():
        wga_ref[...] = wgat_ref[...].astype(BF16)
        _project(x_ref, g_ref, w_scr, wga_ref, wup_ref, bup_ref, lbt_ref, kn_ref, q_ref, v_ref,
                 gate_ref, dec_ref, N_META, transposed=False)


def _meta(meta_tokens, g, w_in_t, w_up, b_up, lb_table):
    n_blocks = C_GA // W_BLOCK
    full = lambda a: pl.BlockSpec(a.shape, lambda j: (0,) * len(a.shape))
    last = n_blocks - 1
    out_shapes = [jax.ShapeDtypeStruct((n_blocks, D_MODEL, W_BLOCK), BF16),
                  jax.ShapeDtypeStruct((GATE_RANK, D_MODEL), BF16),
                  jax.ShapeDtypeStruct((N_META, KN_COLS), BF16),
                  jax.ShapeDtypeStruct((1, N_META, MIX_WIDTH), BF16),
                  jax.ShapeDtypeStruct((1, N_META, MIX_WIDTH), BF16),
                  jax.ShapeDtypeStruct((N_META, MIX_WIDTH), F32),
                  jax.ShapeDtypeStruct((1, 1, 2 * DEC_COLS), F32)]
    w_main, w_ga, kn, _, v, _, _ = pl.pallas_call(
        _meta_kernel,
        grid=(n_blocks + 1,),
        in_specs=[
            full(meta_tokens), full(g),
            pl.BlockSpec((W_BLOCK, D_MODEL), lambda j: (jnp.minimum(j, last), 0)),
            pl.BlockSpec((GATE_RANK, D_MODEL), lambda j: (C_GA // GATE_RANK, 0)),
            full(w_up), full(b_up), full(lb_table),
        ],
        out_specs=[pl.BlockSpec((1, D_MODEL, W_BLOCK), lambda j: (jnp.minimum(j, last), 0, 0))]
        + [full(s) for s in out_shapes[1:]],
        out_shape=out_shapes,
        scratch_shapes=[pltpu.VMEM((n_blocks, D_MODEL, W_BLOCK), BF16)],
        compiler_params=pltpu.CompilerParams(
            dimension_semantics=("arbitrary",), vmem_limit_bytes=V7X_VMEM_LIMIT),
        name="meta",
    )(meta_tokens, g, w_in_t, w_in_t, w_up, b_up, lb_table)
    return w_main, w_ga, kn, v


def _side_by_side(a, b):
    return jnp.concatenate([a, b], axis=1)


def _block_diag(a, b):
    return jnp.concatenate([_side_by_side(a, jnp.zeros_like(b)),
                            _side_by_side(jnp.zeros_like(a), b)], axis=0)


def _seed_state(mkn_ref, mv_ref, st_ref):
    seed = [_dot_tn(mv_ref[0, :, LANES * i:LANES * (i + 1)], mkn_ref[:, ke_col:ke_col + LANES])
            for i, (_, ke_col, _) in enumerate(HEAD_TABLE)]
    for p in range(N_PAIRS):
        st_ref[p] = _side_by_side(seed[2 * p], seed[2 * p + 1])


def _recurrence_steps(kn_ref, qt_ref, vt_ref, dec_ref, hnorm_ref, gnorm_ref, o_ref, st_ref):
    key_before_query = _iota2(CHUNK, 0) <= _iota2(CHUNK, 1)
    key_before_query = _side_by_side(key_before_query, key_before_query)
    norms = (hnorm_ref[...],) * HEADS + (gnorm_ref[...],) * HEADS

    def pair(fn, p):
        return fn(2 * p), fn(2 * p + 1)

    for c in range(kn_ref.shape[0] // CHUNK):
        rows = slice(c * CHUNK, (c + 1) * CHUNK)
        dec = dec_ref[c]
        km = lambda i: kn_ref[rows, HEAD_TABLE[i][0]:HEAD_TABLE[i][0] + LANES]
        ke = lambda i: kn_ref[rows, HEAD_TABLE[i][1]:HEAD_TABLE[i][1] + LANES]
        e_mid = lambda i: dec[:, HEAD_TABLE[i][2]:HEAD_TABLE[i][2] + LANES]
        e_last = lambda i: dec[:, DEC_COLS + HEAD_TABLE[i][2]:DEC_COLS + HEAD_TABLE[i][2] + LANES]
        qm_t = lambda i: qt_ref[c, LANES * i:LANES * (i + 1), :]
        v_t = lambda i: vt_ref[c, LANES * i:LANES * (i + 1), :]
        first, outs = [], []
        for p in range(N_PAIRS):
            s_mid = st_ref[p] * _side_by_side(*pair(e_mid, p))
            lhs = jnp.concatenate([_side_by_side(*pair(km, p)), s_mid.astype(BF16)], axis=0)
            first.append(jnp.dot(lhs, _block_diag(*pair(qm_t, p)),
                                 preferred_element_type=F32))
        yield
        for p in range(N_PAIRS):
            scores_t = jnp.where(key_before_query, first[p][:CHUNK], 0.0).astype(BF16)
            outs.append(first[p][CHUNK:]
                        + jnp.dot(_side_by_side(*pair(v_t, p)),
                                  _block_diag(scores_t[:, :CHUNK], scores_t[:, CHUNK:]),
                                  preferred_element_type=F32))
        yield
        for p in range(N_PAIRS):
            st_ref[p] = (st_ref[p] * _side_by_side(*pair(e_last, p))
                         + jnp.dot(_side_by_side(*pair(v_t, p)), _block_diag(*pair(ke, p)),
                                   preferred_element_type=F32))
        yield
        for p in range(N_PAIRS):
            o_t = outs[p]
            o_t = o_t * lax.rsqrt(jnp.mean(o_t * o_t, axis=0, keepdims=True) + NORM_EPS)
            o = o_t.T
            for half, i in enumerate((2 * p, 2 * p + 1)):
                o_ref[rows, LANES * i:LANES * (i + 1)] = (
                    o[half * CHUNK:(half + 1) * CHUNK] * norms[i]).astype(o_ref.dtype)
        yield


STAGES_PER_PAUSE = (1, 1, 2, 1, 2, 1, 1, 2, 1, 2, 2)


def _fused_kernel(x_ref, g_ref, w_ref, wga_ref, wup_ref, bup_ref, lbt_ref, wout32_ref, w1_32_ref,
                  w2_32_ref, mkn_ref, mv_ref, hnorm_ref, gnorm_ref,
                  gate_ref, o_ref, wout_ref, w1_ref, w2_ref,
                  kn_scr, qt_scr, vt_scr, dec_scr, st_ref, *, n_tiles, steps_per_seq):
    s = pl.program_id(0)

    @pl.when(lax.rem(s + steps_per_seq - 1, steps_per_seq) == 0)
    def _():
        _seed_state(mkn_ref, mv_ref, st_ref)

    def project(fill, between=None):
        _project(x_ref, g_ref, w_ref, wga_ref, wup_ref, bup_ref, lbt_ref, kn_scr.at[fill],
                 qt_scr.at[fill], vt_scr.at[fill], gate_ref, dec_scr.at[fill], CHUNK,
                 transposed=True, between=between)
        wout_ref[...] = wout32_ref[...].astype(BF16)
        w1_ref[...] = w1_32_ref[...].astype(BF16)
        w2_ref[...] = w2_32_ref[...].astype(BF16)

    def recurrence(drain):
        return _recurrence_steps(kn_scr.at[drain], qt_scr.at[drain], vt_scr.at[drain],
                                 dec_scr.at[drain], hnorm_ref, gnorm_ref, o_ref, st_ref)

    @pl.when(s == 0)
    def _():
        project(0)

    for parity in (0, 1):
        @pl.when((s > 0) & (s < n_tiles) & (lax.rem(s, 2) == parity))
        def _():
            stages = recurrence(1 - parity)
            budget = list(STAGES_PER_PAUSE)

            def between():
                for _ in range(budget.pop(0) if budget else 1):
                    next(stages, None)

            project(parity, between)
            for _ in stages:
                pass

    @pl.when(s == n_tiles)
    def _():
        for _ in recurrence((n_tiles - 1) % 2):
            pass


def _fused(x2d, g, w_main, w_ga, w_up, b_up, lb_table, w_out, w1, w2, meta_kn, meta_v, hnorm,
           gnorm, seq, rows):
    n = x2d.shape[0]
    n_tiles = n // rows
    chunks = rows // CHUNK
    last = n_tiles - 1
    wout_rows = w_out.shape[0] // n_tiles
    w1_rows = w1.shape[0] // n_tiles
    w2_rows = 2 * w2.shape[0] // n_tiles
    assert wout_rows * n_tiles == w_out.shape[0] and w1_rows * n_tiles == w1.shape[0]
    assert w2_rows * n_tiles == 2 * w2.shape[0]
    const = lambda s: (0, 0)
    const3 = lambda s: (0, 0, 0)
    cur = lambda s: (jnp.minimum(s, last), 0)
    prev = lambda s: (jnp.maximum(s - 1, 0), 0)
    half_pace = lambda s: (jnp.minimum(s, last) // 2, 0)
    w_specs = [pl.BlockSpec((wout_rows, w_out.shape[1]), cur),
               pl.BlockSpec((w1_rows, w1.shape[1]), cur),
               pl.BlockSpec((w2_rows, w2.shape[1]), half_pace)]
    return pl.pallas_call(
        functools.partial(_fused_kernel, n_tiles=n_tiles, steps_per_seq=seq // rows),
        grid=(n_tiles + 1,),
        in_specs=[
            pl.BlockSpec((rows, D_MODEL), cur),
            pl.BlockSpec((1, D_MODEL), const),
            pl.BlockSpec(w_main.shape, const3, pipeline_mode=pl.Buffered(1)),
            pl.BlockSpec(w_ga.shape, const),
            pl.BlockSpec((GATE_RANK, GLA_QK_WIDTH), const),
            pl.BlockSpec((1, GLA_QK_WIDTH), const),
            pl.BlockSpec(lb_table.shape, const),
        ] + w_specs + [
            pl.BlockSpec(meta_kn.shape, const),
            pl.BlockSpec(meta_v.shape, const3),
            pl.BlockSpec((1, HEAD_V), const),
            pl.BlockSpec((1, HEAD_V), const),
        ],
        out_specs=[
            pl.BlockSpec((rows, MIX_WIDTH), cur),
            pl.BlockSpec((rows, MIX_WIDTH), prev),
        ] + w_specs,
        out_shape=[
            jax.ShapeDtypeStruct((n, MIX_WIDTH), F32),
            jax.ShapeDtypeStruct((n, MIX_WIDTH), BF16),
            jax.ShapeDtypeStruct(w_out.shape, BF16),
            jax.ShapeDtypeStruct(w1.shape, BF16),
            jax.ShapeDtypeStruct(w2.shape, BF16),
        ],
        scratch_shapes=[pltpu.VMEM((2, rows, KN_COLS), BF16),
                        pltpu.VMEM((2, chunks, MIX_WIDTH, CHUNK), BF16),
                        pltpu.VMEM((2, chunks, MIX_WIDTH, CHUNK), BF16),
                        pltpu.VMEM((2, chunks, 1, 2 * DEC_COLS), F32),
                        pltpu.VMEM((N_PAIRS, HEAD_V, 2 * LANES), F32)],
        compiler_params=pltpu.CompilerParams(
            dimension_semantics=("arbitrary",), vmem_limit_bytes=V7X_VMEM_LIMIT),
        name="mixer",
    )(x2d, g, w_main, w_ga, w_up, b_up, lb_table, w_out, w1, w2, meta_kn, meta_v, hnorm, gnorm)


def _ffn_kernel(x_ref, o_ref, gate_ref, wout_ref, g2_ref, w1_ref, w2_ref, g3_ref, y_ref):
    o = o_ref[...].astype(F32) * _silu(gate_ref[...])
    h = x_ref[...] + _dot(o, wout_ref[...])
    u = _rmsnorm(h, g2_ref[...]).astype(BF16)
    ffn = None
    for j in range(0, FFN_HIDDEN, FFN_BLOCK):
        gate = jnp.dot(u, w1_ref[:, j:j + FFN_BLOCK], preferred_element_type=F32)
        up = jnp.dot(u, w1_ref[:, FFN_HIDDEN + j:FFN_HIDDEN + j + FFN_BLOCK],
                     preferred_element_type=F32)
        part = _dot(_silu(gate) * up, w2_ref[j:j + FFN_BLOCK, :])
        ffn = part if ffn is None else ffn + part
    y_ref[...] = _rmsnorm(h + ffn, g3_ref[...])


def _ffn(x2d, o, gates, w_out, g2, w1, w2, g3, rows):
    n = x2d.shape[0]
    const = lambda i: (0, 0)
    tile = lambda i: (i, 0)
    resident = functools.partial(pl.BlockSpec, index_map=const, pipeline_mode=pl.Buffered(1))
    return pl.pallas_call(
        _ffn_kernel,
        grid=(n // rows,),
        in_specs=[
            pl.BlockSpec((rows, D_MODEL), tile),
            pl.BlockSpec((rows, MIX_WIDTH), tile),
            pl.BlockSpec((rows, MIX_WIDTH), tile),
            resident((MIX_WIDTH, D_MODEL)),
            pl.BlockSpec((1, D_MODEL), const),
            resident((D_MODEL, 2 * FFN_HIDDEN)),
            resident((FFN_HIDDEN, D_MODEL)),
            pl.BlockSpec((1, D_MODEL), const),
        ],
        out_specs=pl.BlockSpec((rows, D_MODEL), tile),
        out_shape=jax.ShapeDtypeStruct((n, D_MODEL), F32),
        compiler_params=pltpu.CompilerParams(
            dimension_semantics=("arbitrary",), vmem_limit_bytes=V7X_VMEM_LIMIT),
        name="outproj_ffn",
    )(x2d, o, gates, w_out, g2, w1, w2, g3)


def kernel(x, meta_tokens, lb_table, norm_mix_g, w_in, w_gla_gate_up, b_gla_gate, hgrn_norm_g,
           gla_norm_g, w_out, norm_ffn_g, w_ffn_in, w_ffn_out, norm_final_g):
    batch, seq, _ = x.shape
    assert w_in.shape[0] == 1 and lb_table.shape[0] == 2, "single-layer block"
    assert seq % ROWS_MIXER == 0
    x2d = x.reshape(batch * seq, D_MODEL)
    g_mix = norm_mix_g[0][None, :]
    w_up = w_gla_gate_up[0]
    b_up = b_gla_gate[0][None, :]

    w_main, w_ga, meta_kn, meta_v = _meta(meta_tokens, g_mix, w_in[0].T, w_up, b_up, lb_table)
    gates, o, w_out_b, w1_b, w2_b = _fused(
        x2d, g_mix, w_main, w_ga, w_up, b_up, lb_table, w_out[0], w_ffn_in[0], w_ffn_out[0],
        meta_kn, meta_v, hgrn_norm_g[0][None, :], gla_norm_g[0][None, :], seq, ROWS_MIXER)
    y = _ffn(x2d, o, gates, w_out_b, norm_ffn_g[0][None, :], w1_b, w2_b, norm_final_g[None, :],
             ROWS_FFN)
    return y.reshape(batch, seq, D_MODEL)
```

```python
import functools

import jax
import jax.numpy as jnp
from jax import lax
from jax.experimental import pallas as pl
from jax.experimental.pallas import tpu as pltpu

F32 = jnp.float32
BF16 = jnp.bfloat16

D_MODEL = 1024
N_META = 16
CHUNK = 128
HEADS = 4
HEAD_V = 128
HGRN_WIDTH = HEADS * HEAD_V
GLA_K = 64
GLA_QK_WIDTH = HEADS * GLA_K
GLA_WIDTH = HEADS * HEAD_V
MIX_WIDTH = HGRN_WIDTH + GLA_WIDTH
GATE_RANK = 16
GATE_NORMALIZER = 16.0
LOG2_E = 1.4426950408889634
FFN_HIDDEN = 2816
NORM_EPS = 1e-6
LANES = 128
N_PAIRS = HEADS

C_HQ, C_HF, C_HI, C_HG = 0, 512, 1024, 1536
C_GQ, C_GK, C_GV, C_GG = 2048, 2304, 2560, 3072
C_GA = 3584
IN_COLS = C_GA + GATE_RANK
W_BLOCK = 512

KM_H, KM_G = 0, HGRN_WIDTH
KE_H, KE_G = HGRN_WIDTH + GLA_QK_WIDTH, 2 * HGRN_WIDTH + GLA_QK_WIDTH
KN_COLS = KE_G + GLA_WIDTH
DEC_COLS = HGRN_WIDTH + GLA_QK_WIDTH

HEAD_TABLE = tuple(
    [(KM_H + LANES * h, KE_H + LANES * h, LANES * h) for h in range(HEADS)]
    + [(KM_G + LANES * (h // 2), KE_G + LANES * h, HGRN_WIDTH + LANES * (h // 2))
       for h in range(HEADS)])

V7X_VMEM_LIMIT = 56 * 1024 * 1024

ROWS_MIXER = 512
ROWS_FFN = 512
FFN_BLOCK = 256


def _rmsnorm(x, g):
    return x * lax.rsqrt(jnp.mean(x * x, axis=-1, keepdims=True) + NORM_EPS) * g


def _dot(a, b):
    return jnp.dot(a.astype(BF16), b.astype(BF16), preferred_element_type=F32)


def _dot_nt(a, b):
    return lax.dot_general(a.astype(BF16), b.astype(BF16), (((1,), (1,)), ((), ())),
                           preferred_element_type=F32)


def _dot_tn(a, b):
    return lax.dot_general(a.astype(BF16), b.astype(BF16), (((0,), (0,)), ((), ())),
                           preferred_element_type=F32)


def _sigmoid(x):
    return 0.5 * jnp.tanh(0.5 * x) + 0.5


def _silu(x):
    return x * _sigmoid(x)


def _log_sigmoid(x):
    return jnp.minimum(x, 0.0) - jnp.log1p(jnp.exp(-jnp.abs(x)))


def _iota2(n, axis):
    return lax.broadcasted_iota(jnp.int32, (n, n), axis)


def _cumsum_rows(g, tri2_bf16):
    g_hi = g.astype(BF16)
    g_lo = (g - g_hi.astype(F32)).astype(BF16)
    return jnp.dot(tri2_bf16, jnp.concatenate([g_hi, g_lo], axis=0),
                   preferred_element_type=F32)


def _own_half(x, h):
    lane = lax.broadcasted_iota(jnp.int32, x.shape, 1)
    return jnp.where((lane >= GLA_K) == bool(h % 2), x, 0.0)


def _project(x_ref, g_ref, w_ref, wga_ref, wup_ref, bup_ref, lbt_ref, kn_ref, qt_ref, vt_ref,
             gate_ref, dec_ref, chunk, transposed, between=None):
    n_chunks = x_ref.shape[0] // chunk
    u = _rmsnorm(x_ref[...], g_ref[...]).astype(BF16)
    tri = (_iota2(chunk, 0) >= _iota2(chunk, 1)).astype(BF16)
    tri = jnp.concatenate([tri, tri], axis=1)
    mid = chunk // 2 - 1

    def proj(col, width):
        blk, off = divmod(col, W_BLOCK)
        assert off + width <= W_BLOCK
        return jnp.dot(u, w_ref[blk, :, off:off + width], preferred_element_type=F32)

    def decay_factors(g, c):
        b = _cumsum_rows(g[c * chunk:(c + 1) * chunk], tri)
        b_mid = b[mid:mid + 1, :]
        b_last = b[chunk - 1:, :]
        return (jnp.exp2(b - b_mid), jnp.exp2(b_mid - b), jnp.exp2(b_last - b),
                jnp.exp2(b_mid), jnp.exp2(b_last))

    def put_tile(ref, c, i, tile):
        if transposed:
            ref[c, LANES * i:LANES * (i + 1), :] = tile.T
        else:
            ref[c, :, LANES * i:LANES * (i + 1)] = tile

    t0 = lbt_ref[0:1, :]
    t1 = lbt_ref[1:2, :]
    m = jnp.maximum(t0, t1)
    e0 = jnp.exp(t0 - m)
    lb = e0 / (e0 + jnp.exp(t1 - m))

    def value_piece(col, first_head):
        def emit():
            v = proj(col, 2 * LANES).astype(BF16)
            for c in range(n_chunks):
                for h in range(2):
                    put_tile(vt_ref, c, first_head + h,
                             v[c * chunk:(c + 1) * chunk, LANES * h:LANES * (h + 1)])
        return emit

    def gate_piece(col, out_col):
        def emit():
            gate_ref[:, out_col:out_col + 2 * LANES] = proj(col, 2 * LANES)
        return emit

    light = [value_piece(C_HI, 0), value_piece(C_HI + 2 * LANES, 2),
             value_piece(C_GV, HEADS), value_piece(C_GV + 2 * LANES, HEADS + 2),
             gate_piece(C_HG, 0), gate_piece(C_HG + 2 * LANES, 2 * LANES),
             gate_piece(C_GG, HGRN_WIDTH), gate_piece(C_GG + 2 * LANES, HGRN_WIDTH + 2 * LANES)]

    def pause():
        if between is not None:
            between()

    def next_light():
        if light:
            light.pop(0)()
        pause()

    def ga_part(k):
        cols = slice(k * (D_MODEL // 4), (k + 1) * (D_MODEL // 4))
        return _dot_nt(wga_ref[:, cols], u[:, cols])

    ga_t = ga_part(0)
    hf = proj(C_HF, HGRN_WIDTH)
    ga_t += ga_part(1)
    hq = proj(C_HQ, HGRN_WIDTH)
    ga_t += ga_part(2)
    gq = proj(C_GQ, GLA_QK_WIDTH)
    ga_t += ga_part(3)
    gk = proj(C_GK, GLA_QK_WIDTH)
    pause()

    f = 0.5 * (1.0 + lb) + (0.5 * (1.0 - lb)) * jnp.tanh(0.5 * hf)
    g = jnp.log2(f)
    k = 1.0 - f
    q = _silu(hq)
    pause()
    for c in range(n_chunks):
        rows = slice(c * chunk, (c + 1) * chunk)
        e_q, e_k, e_end, e_mid, e_last = decay_factors(g, c)
        kn_ref[rows, KM_H:KM_H + HGRN_WIDTH] = (k[rows] * e_k).astype(BF16)
        kn_ref[rows, KE_H:KE_H + HGRN_WIDTH] = (k[rows] * e_end).astype(BF16)
        qm = (q[rows] * e_q).astype(BF16)
        for h in range(HEADS):
            put_tile(qt_ref, c, h, qm[:, LANES * h:LANES * (h + 1)])
        dec_ref[c, :, 0:HGRN_WIDTH] = e_mid
        dec_ref[c, :, DEC_COLS:DEC_COLS + HGRN_WIDTH] = e_last
        next_light()

    next_light()
    logits = _dot_tn(ga_t, wup_ref[...]) + bup_ref[...]
    g = _log_sigmoid(logits) * (LOG2_E / GATE_NORMALIZER)
    q = gq * (GLA_K ** -0.5)
    k = gk
    for c in range(n_chunks):
        rows = slice(c * chunk, (c + 1) * chunk)
        e_q, e_k, e_end, e_mid, e_last = decay_factors(g, c)
        qm = q[rows] * e_q
        ke = k[rows] * e_end
        kn_ref[rows, KM_G:KM_G + GLA_QK_WIDTH] = (k[rows] * e_k).astype(BF16)
        for h in range(HEADS):
            pair = slice(LANES * (h // 2), LANES * (h // 2 + 1))
            put_tile(qt_ref, c, HEADS + h, _own_half(qm[:, pair], h).astype(BF16))
            kn_ref[rows, KE_G + LANES * h:KE_G + LANES * (h + 1)] = (
                _own_half(ke[:, pair], h).astype(BF16))
        dec_ref[c, :, HGRN_WIDTH:DEC_COLS] = e_mid
        dec_ref[c, :, DEC_COLS + HGRN_WIDTH:2 * DEC_COLS] = e_last
        next_light()
    while light:
        next_light()


def _meta_kernel(x_ref, g_ref, wt_ref, wgat_ref, wup_ref, bup_ref, lbt_ref,
                 w_ref, wga_ref, kn_ref, q_ref, v_ref, gate_ref, dec_ref, w_scr):
    j = pl.program_id(0)
    n_blocks = w_scr.shape[0]

    @pl.when(j < n_blocks)
    def _():
        blk = wt_ref[...].T.astype(BF16)
        w_ref[0] = blk
        w_scr[j] = blk

    @pl.when(j == n_blocks)
    def _():
        wga_ref[...] = wgat_ref[...].astype(BF16)
        _project(x_ref, g_ref, w_scr, wga_ref, wup_ref, bup_ref, lbt_ref, kn_ref, q_ref, v_ref,
                 gate_ref, dec_ref, N_META, transposed=False)


def _meta(meta_tokens, g, w_in_t, w_up, b_up, lb_table):
    n_blocks = C_GA // W_BLOCK
    full = lambda a: pl.BlockSpec(a.shape, lambda j: (0,) * len(a.shape))
    last = n_blocks - 1
    out_shapes = [jax.ShapeDtypeStruct((n_blocks, D_MODEL, W_BLOCK), BF16),
                  jax.ShapeDtypeStruct((GATE_RANK, D_MODEL), BF16),
                  jax.ShapeDtypeStruct((N_META, KN_COLS), BF16),
                  jax.ShapeDtypeStruct((1, N_META, MIX_WIDTH), BF16),
                  jax.ShapeDtypeStruct((1, N_META, MIX_WIDTH), BF16),
                  jax.ShapeDtypeStruct((N_META, MIX_WIDTH), F32),
                  jax.ShapeDtypeStruct((1, 1, 2 * DEC_COLS), F32)]
    w_main, w_ga, kn, _, v, _, _ = pl.pallas_call(
        _meta_kernel,
        grid=(n_blocks + 1,),
        in_specs=[
            full(meta_tokens), full(g),
            pl.BlockSpec((W_BLOCK, D_MODEL), lambda j: (jnp.minimum(j, last), 0)),
            pl.BlockSpec((GATE_RANK, D_MODEL), lambda j: (C_GA // GATE_RANK, 0)),
            full(w_up), full(b_up), full(lb_table),
        ],
        out_specs=[pl.BlockSpec((1, D_MODEL, W_BLOCK), lambda j: (jnp.minimum(j, last), 0, 0))]
        + [full(s) for s in out_shapes[1:]],
        out_shape=out_shapes,
        scratch_shapes=[pltpu.VMEM((n_blocks, D_MODEL, W_BLOCK), BF16)],
        compiler_params=pltpu.CompilerParams(
            dimension_semantics=("arbitrary",), vmem_limit_bytes=V7X_VMEM_LIMIT),
        name="meta",
    )(meta_tokens, g, w_in_t, w_in_t, w_up, b_up, lb_table)
    return w_main, w_ga, kn, v


def _side_by_side(a, b):
    return jnp.concatenate([a, b], axis=1)


def _block_diag(a, b):
    return jnp.concatenate([_side_by_side(a, jnp.zeros_like(b)),
                            _side_by_side(jnp.zeros_like(a), b)], axis=0)


def _seed_state(mkn_ref, mv_ref, st_ref):
    seed = [_dot_tn(mv_ref[0, :, LANES * i:LANES * (i + 1)], mkn_ref[:, ke_col:ke_col + LANES])
            for i, (_, ke_col, _) in enumerate(HEAD_TABLE)]
    for p in range(N_PAIRS):
        st_ref[p] = _side_by_side(seed[2 * p], seed[2 * p + 1])


def _recurrence_steps(kn_ref, qt_ref, vt_ref, dec_ref, hnorm_ref, gnorm_ref, o_ref, st_ref):
    key_before_query = _iota2(CHUNK, 0) <= _iota2(CHUNK, 1)
    key_before_query = _side_by_side(key_before_query, key_before_query)
    norms = (hnorm_ref[...],) * HEADS + (gnorm_ref[...],) * HEADS

    def pair(fn, p):
        return fn(2 * p), fn(2 * p + 1)

    for c in range(kn_ref.shape[0] // CHUNK):
        rows = slice(c * CHUNK, (c + 1) * CHUNK)
        dec = dec_ref[c]
        km = lambda i: kn_ref[rows, HEAD_TABLE[i][0]:HEAD_TABLE[i][0] + LANES]
        ke = lambda i: kn_ref[rows, HEAD_TABLE[i][1]:HEAD_TABLE[i][1] + LANES]
        e_mid = lambda i: dec[:, HEAD_TABLE[i][2]:HEAD_TABLE[i][2] + LANES]
        e_last = lambda i: dec[:, DEC_COLS + HEAD_TABLE[i][2]:DEC_COLS + HEAD_TABLE[i][2] + LANES]
        qm_t = lambda i: qt_ref[c, LANES * i:LANES * (i + 1), :]
        v_t = lambda i: vt_ref[c, LANES * i:LANES * (i + 1), :]
        first, outs = [], []
        for p in range(N_PAIRS):
            s_mid = st_ref[p] * _side_by_side(*pair(e_mid, p))
            lhs = jnp.concatenate([_side_by_side(*pair(km, p)), s_mid.astype(BF16)], axis=0)
            first.append(jnp.dot(lhs, _block_diag(*pair(qm_t, p)),
                                 preferred_element_type=F32))
        yield
        for p in range(N_PAIRS):
            scores_t = jnp.where(key_before_query, first[p][:CHUNK], 0.0).astype(BF16)
            outs.append(first[p][CHUNK:]
                        + jnp.dot(_side_by_side(*pair(v_t, p)),
                                  _block_diag(scores_t[:, :CHUNK], scores_t[:, CHUNK:]),
                                  preferred_element_type=F32))
        yield
        for p in range(N_PAIRS):
            st_ref[p] = (st_ref[p] * _side_by_side(*pair(e_last, p))
                         + jnp.dot(_side_by_side(*pair(v_t, p)), _block_diag(*pair(ke, p)),
                                   preferred_element_type=F32))
        yield
        for p in range(N_PAIRS):
            o_t = outs[p]
            o_t = o_t * lax.rsqrt(jnp.mean(o_t * o_t, axis=0, keepdims=True) + NORM_EPS)
            o = o_t.T
            for half, i in enumerate((2 * p, 2 * p + 1)):
                o_ref[rows, LANES * i:LANES * (i + 1)] = (
                    o[half * CHUNK:(half + 1) * CHUNK] * norms[i]).astype(o_ref.dtype)
        yield


STAGES_PER_PAUSE = (1, 1, 2, 1, 2, 1, 1, 2, 1, 2, 2)


def _fused_kernel(x_ref, g_ref, w_ref, wga_ref, wup_ref, bup_ref, lbt_ref, wout32_ref, w1_32_ref,
                  w2_32_ref, mkn_ref, mv_ref, hnorm_ref, gnorm_ref,
                  gate_ref, o_ref, wout_ref, w1_ref, w2_ref,
                  kn_scr, qt_scr, vt_scr, dec_scr, st_ref, *, n_tiles, steps_per_seq):
    s = pl.program_id(0)

    @pl.when(lax.rem(s + steps_per_seq - 1, steps_per_seq) == 0)
    def _():
        _seed_state(mkn_ref, mv_ref, st_ref)

    def project(fill, between=None):
        _project(x_ref, g_ref, w_ref, wga_ref, wup_ref, bup_ref, lbt_ref, kn_scr.at[fill],
                 qt_scr.at[fill], vt_scr.at[fill], gate_ref, dec_scr.at[fill], CHUNK,
                 transposed=True, between=between)
        wout_ref[...] = wout32_ref[...].astype(BF16)
        w1_ref[...] = w1_32_ref[...].astype(BF16)
        w2_ref[...] = w2_32_ref[...].astype(BF16)

    def recurrence(drain):
        return _recurrence_steps(kn_scr.at[drain], qt_scr.at[drain], vt_scr.at[drain],
                                 dec_scr.at[drain], hnorm_ref, gnorm_ref, o_ref, st_ref)

    @pl.when(s == 0)
    def _():
        project(0)

    for parity in (0, 1):
        @pl.when((s > 0) & (s < n_tiles) & (lax.rem(s, 2) == parity))
        def _():
            stages = recurrence(1 - parity)
            budget = list(STAGES_PER_PAUSE)

            def between():
                for _ in range(budget.pop(0) if budget else 1):
                    next(stages, None)

            project(parity, between)
            for _ in stages:
                pass

    @pl.when(s == n_tiles)
    def _():
        for _ in recurrence((n_tiles - 1) % 2):
            pass


def _fused(x2d, g, w_main, w_ga, w_up, b_up, lb_table, w_out, w1, w2, meta_kn, meta_v, hnorm,
           gnorm, seq, rows):
    n = x2d.shape[0]
    n_tiles = n // rows
    chunks = rows // CHUNK
    last = n_tiles - 1
    wout_rows = w_out.shape[0] // n_tiles
    w1_rows = w1.shape[0] // n_tiles
    w2_rows = 2 * w2.shape[0] // n_tiles
    assert wout_rows * n_tiles == w_out.shape[0] and w1_rows * n_tiles == w1.shape[0]
    assert w2_rows * n_tiles == 2 * w2.shape[0]
    const = lambda s: (0, 0)
    const3 = lambda s: (0, 0, 0)
    cur = lambda s: (jnp.minimum(s, last), 0)
    prev = lambda s: (jnp.maximum(s - 1, 0), 0)
    half_pace = lambda s: (jnp.minimum(s, last) // 2, 0)
    w_specs = [pl.BlockSpec((wout_rows, w_out.shape[1]), cur),
               pl.BlockSpec((w1_rows, w1.shape[1]), cur),
               pl.BlockSpec((w2_rows, w2.shape[1]), half_pace)]
    return pl.pallas_call(
        functools.partial(_fused_kernel, n_tiles=n_tiles, steps_per_seq=seq // rows),
        grid=(n_tiles + 1,),
        in_specs=[
            pl.BlockSpec((rows, D_MODEL), cur),
            pl.BlockSpec((1, D_MODEL), const),
            pl.BlockSpec(w_main.shape, const3, pipeline_mode=pl.Buffered(1)),
            pl.BlockSpec(w_ga.shape, const),
            pl.BlockSpec((GATE_RANK, GLA_QK_WIDTH), const),
            pl.BlockSpec((1, GLA_QK_WIDTH), const),
            pl.BlockSpec(lb_table.shape, const),
        ] + w_specs + [
            pl.BlockSpec(meta_kn.shape, const),
            pl.BlockSpec(meta_v.shape, const3),
            pl.BlockSpec((1, HEAD_V), const),
            pl.BlockSpec((1, HEAD_V), const),
        ],
        out_specs=[
            pl.BlockSpec((rows, MIX_WIDTH), cur),
            pl.BlockSpec((rows, MIX_WIDTH), prev),
        ] + w_specs,
        out_shape=[
            jax.ShapeDtypeStruct((n, MIX_WIDTH), F32),
            jax.ShapeDtypeStruct((n, MIX_WIDTH), BF16),
            jax.ShapeDtypeStruct(w_out.shape, BF16),
            jax.ShapeDtypeStruct(w1.shape, BF16),
            jax.ShapeDtypeStruct(w2.shape, BF16),
        ],
        scratch_shapes=[pltpu.VMEM((2, rows, KN_COLS), BF16),
                        pltpu.VMEM((2, chunks, MIX_WIDTH, CHUNK), BF16),
                        pltpu.VMEM((2, chunks, MIX_WIDTH, CHUNK), BF16),
                        pltpu.VMEM((2, chunks, 1, 2 * DEC_COLS), F32),
                        pltpu.VMEM((N_PAIRS, HEAD_V, 2 * LANES), F32)],
        compiler_params=pltpu.CompilerParams(
            dimension_semantics=("arbitrary",), vmem_limit_bytes=V7X_VMEM_LIMIT),
        name="mixer",
    )(x2d, g, w_main, w_ga, w_up, b_up, lb_table, w_out, w1, w2, meta_kn, meta_v, hnorm, gnorm)


def _ffn_kernel(x_ref, o_ref, gate_ref, wout_ref, g2_ref, w1_ref, w2_ref, g3_ref, y_ref):
    o = o_ref[...].astype(F32) * _silu(gate_ref[...])
    h = x_ref[...] + _dot(o, wout_ref[...])
    u = _rmsnorm(h, g2_ref[...]).astype(BF16)
    acts = []
    for j in range(0, FFN_HIDDEN, FFN_BLOCK):
        gate = jnp.dot(u, w1_ref[:, j:j + FFN_BLOCK], preferred_element_type=F32)
        up = jnp.dot(u, w1_ref[:, FFN_HIDDEN + j:FFN_HIDDEN + j + FFN_BLOCK],
                     preferred_element_type=F32)
        acts.append((_silu(gate) * up).astype(BF16))
    ffn = jnp.dot(jnp.concatenate(acts, axis=1), w2_ref[...], preferred_element_type=F32)
    y_ref[...] = _rmsnorm(h + ffn, g3_ref[...])


def _ffn(x2d, o, gates, w_out, g2, w1, w2, g3, rows):
    n = x2d.shape[0]
    const = lambda i: (0, 0)
    tile = lambda i: (i, 0)
    resident = functools.partial(pl.BlockSpec, index_map=const, pipeline_mode=pl.Buffered(1))
    return pl.pallas_call(
        _ffn_kernel,
        grid=(n // rows,),
        in_specs=[
            pl.BlockSpec((rows, D_MODEL), tile),
            pl.BlockSpec((rows, MIX_WIDTH), tile),
            pl.BlockSpec((rows, MIX_WIDTH), tile),
            resident((MIX_WIDTH, D_MODEL)),
            pl.BlockSpec((1, D_MODEL), const),
            resident((D_MODEL, 2 * FFN_HIDDEN)),
            resident((FFN_HIDDEN, D_MODEL)),
            pl.BlockSpec((1, D_MODEL), const),
        ],
        out_specs=pl.BlockSpec((rows, D_MODEL), tile),
        out_shape=jax.ShapeDtypeStruct((n, D_MODEL), F32),
        compiler_params=pltpu.CompilerParams(
            dimension_semantics=("arbitrary",), vmem_limit_bytes=V7X_VMEM_LIMIT),
        name="outproj_ffn",
    )(x2d, o, gates, w_out, g2, w1, w2, g3)


def kernel(x, meta_tokens, lb_table, norm_mix_g, w_in, w_gla_gate_up, b_gla_gate, hgrn_norm_g,
           gla_norm_g, w_out, norm_ffn_g, w_ffn_in, w_ffn_out, norm_final_g):
    batch, seq, _ = x.shape
    assert w_in.shape[0] == 1 and lb_table.shape[0] == 2, "single-layer block"
    assert seq % ROWS_MIXER == 0
    x2d = x.reshape(batch * seq, D_MODEL)
    g_mix = norm_mix_g[0][None, :]
    w_up = w_gla_gate_up[0]
    b_up = b_gla_gate[0][None, :]

    w_main, w_ga, meta_kn, meta_v = _meta(meta_tokens, g_mix, w_in[0].T, w_up, b_up, lb_table)
    gates, o, w_out_b, w1_b, w2_b = _fused(
        x2d, g_mix, w_main, w_ga, w_up, b_up, lb_table, w_out[0], w_ffn_in[0], w_ffn_out[0],
        meta_kn, meta_v, hgrn_norm_g[0][None, :], gla_norm_g[0][None, :], seq, ROWS_MIXER)
    y = _ffn(x2d, o, gates, w_out_b, norm_ffn_g[0][None, :], w1_b, w2_b, norm_final_g[None, :],
             ROWS_FFN)
    return y.reshape(batch, seq, D_MODEL)
```

```python
import functools

import jax
import jax.numpy as jnp
from jax import lax
from jax.experimental import pallas as pl
from jax.experimental.pallas import tpu as pltpu

F32 = jnp.float32
BF16 = jnp.bfloat16

D_MODEL = 1024
N_META = 16
CHUNK = 128
HEADS = 4
HEAD_V = 128
HGRN_WIDTH = HEADS * HEAD_V
GLA_K = 64
GLA_QK_WIDTH = HEADS * GLA_K
GLA_WIDTH = HEADS * HEAD_V
MIX_WIDTH = HGRN_WIDTH + GLA_WIDTH
GATE_RANK = 16
GATE_NORMALIZER = 16.0
LOG2_E = 1.4426950408889634
FFN_HIDDEN = 2816
NORM_EPS = 1e-6
LANES = 128
N_PAIRS = HEADS

C_HQ, C_HF, C_HI, C_HG = 0, 512, 1024, 1536
C_GQ, C_GK, C_GV, C_GG = 2048, 2304, 2560, 3072
C_GA = 3584
IN_COLS = C_GA + GATE_RANK
W_BLOCK = 1792

KM_H, KM_G = 0, HGRN_WIDTH
KE_H, KE_G = HGRN_WIDTH + GLA_QK_WIDTH, 2 * HGRN_WIDTH + GLA_QK_WIDTH
KN_COLS = KE_G + GLA_WIDTH
DEC_COLS = HGRN_WIDTH + GLA_QK_WIDTH

HEAD_TABLE = tuple(
    [(KM_H + LANES * h, KE_H + LANES * h, LANES * h) for h in range(HEADS)]
    + [(KM_G + LANES * (h // 2), KE_G + LANES * h, HGRN_WIDTH + LANES * (h // 2))
       for h in range(HEADS)])

V7X_VMEM_LIMIT = 56 * 1024 * 1024

ROWS_MIXER = 512
ROWS_FFN = 512
FFN_BLOCK = 256


def _rmsnorm(x, g):
    return x * lax.rsqrt(jnp.mean(x * x, axis=-1, keepdims=True) + NORM_EPS) * g


def _dot(a, b):
    return jnp.dot(a.astype(BF16), b.astype(BF16), preferred_element_type=F32)


def _dot_nt(a, b):
    return lax.dot_general(a.astype(BF16), b.astype(BF16), (((1,), (1,)), ((), ())),
                           preferred_element_type=F32)


def _dot_tn(a, b):
    return lax.dot_general(a.astype(BF16), b.astype(BF16), (((0,), (0,)), ((), ())),
                           preferred_element_type=F32)


def _sigmoid(x):
    return 0.5 * jnp.tanh(0.5 * x) + 0.5


def _silu(x):
    return x * _sigmoid(x)


def _log_sigmoid(x):
    return jnp.minimum(x, 0.0) - jnp.log1p(jnp.exp(-jnp.abs(x)))


def _iota2(n, axis):
    return lax.broadcasted_iota(jnp.int32, (n, n), axis)


def _cumsum_rows(g, tri2_bf16):
    g_hi = g.astype(BF16)
    g_lo = (g - g_hi.astype(F32)).astype(BF16)
    return jnp.dot(tri2_bf16, jnp.concatenate([g_hi, g_lo], axis=0),
                   preferred_element_type=F32)


def _own_half(x, h):
    lane = lax.broadcasted_iota(jnp.int32, x.shape, 1)
    return jnp.where((lane >= GLA_K) == bool(h % 2), x, 0.0)


def _project(x_ref, g_ref, w_ref, wga_ref, wup_ref, bup_ref, lbt_ref, kn_ref, qt_ref, vt_ref,
             gate_ref, dec_ref, chunk, transposed, between=None):
    n_chunks = x_ref.shape[0] // chunk
    u = _rmsnorm(x_ref[...], g_ref[...]).astype(BF16)
    tri = (_iota2(chunk, 0) >= _iota2(chunk, 1)).astype(BF16)
    tri = jnp.concatenate([tri, tri], axis=1)
    mid = chunk // 2 - 1

    def proj(col, width):
        blk, off = divmod(col, W_BLOCK)
        assert off + width <= W_BLOCK
        return jnp.dot(u, w_ref[blk, :, off:off + width], preferred_element_type=F32)

    def decay_factors(g, c):
        b = _cumsum_rows(g[c * chunk:(c + 1) * chunk], tri)
        b_mid = b[mid:mid + 1, :]
        b_last = b[chunk - 1:, :]
        return (jnp.exp2(b - b_mid), jnp.exp2(b_mid - b), jnp.exp2(b_last - b),
                jnp.exp2(b_mid), jnp.exp2(b_last))

    def put_tile(ref, c, i, tile):
        if transposed:
            ref[c, LANES * i:LANES * (i + 1), :] = tile.T
        else:
            ref[c, :, LANES * i:LANES * (i + 1)] = tile

    t0 = lbt_ref[0:1, :]
    t1 = lbt_ref[1:2, :]
    m = jnp.maximum(t0, t1)
    e0 = jnp.exp(t0 - m)
    lb = e0 / (e0 + jnp.exp(t1 - m))

    def value_piece(col, first_head):
        def emit():
            v = proj(col, 2 * LANES).astype(BF16)
            for c in range(n_chunks):
                for h in range(2):
                    put_tile(vt_ref, c, first_head + h,
                             v[c * chunk:(c + 1) * chunk, LANES * h:LANES * (h + 1)])
        return emit

    def gate_piece(col, out_col):
        def emit():
            gate_ref[:, out_col:out_col + 2 * LANES] = proj(col, 2 * LANES)
        return emit

    light = [value_piece(C_HI, 0), value_piece(C_HI + 2 * LANES, 2),
             value_piece(C_GV, HEADS), value_piece(C_GV + 2 * LANES, HEADS + 2),
             gate_piece(C_HG, 0), gate_piece(C_HG + 2 * LANES, 2 * LANES),
             gate_piece(C_GG, HGRN_WIDTH), gate_piece(C_GG + 2 * LANES, HGRN_WIDTH + 2 * LANES)]

    def pause():
        if between is not None:
            between()

    def next_light():
        if light:
            light.pop(0)()
        pause()

    def ga_part(k):
        cols = slice(k * (D_MODEL // 4), (k + 1) * (D_MODEL // 4))
        return _dot_nt(wga_ref[:, cols], u[:, cols])

    ga_t = ga_part(0)
    hf = proj(C_HF, HGRN_WIDTH)
    ga_t += ga_part(1)
    hq = proj(C_HQ, HGRN_WIDTH)
    ga_t += ga_part(2)
    gq = proj(C_GQ, GLA_QK_WIDTH)
    ga_t += ga_part(3)
    gk = proj(C_GK, GLA_QK_WIDTH)
    pause()

    f = 0.5 * (1.0 + lb) + (0.5 * (1.0 - lb)) * jnp.tanh(0.5 * hf)
    g = jnp.log2(f)
    k = 1.0 - f
    q = _silu(hq)
    pause()
    for c in range(n_chunks):
        rows = slice(c * chunk, (c + 1) * chunk)
        e_q, e_k, e_end, e_mid, e_last = decay_factors(g, c)
        kn_ref[rows, KM_H:KM_H + HGRN_WIDTH] = (k[rows] * e_k).astype(BF16)
        kn_ref[rows, KE_H:KE_H + HGRN_WIDTH] = (k[rows] * e_end).astype(BF16)
        qm = (q[rows] * e_q).astype(BF16)
        for h in range(HEADS):
            put_tile(qt_ref, c, h, qm[:, LANES * h:LANES * (h + 1)])
        dec_ref[c, :, 0:HGRN_WIDTH] = e_mid
        dec_ref[c, :, DEC_COLS:DEC_COLS + HGRN_WIDTH] = e_last
        next_light()

    next_light()
    logits = _dot_tn(ga_t, wup_ref[...]) + bup_ref[...]
    g = _log_sigmoid(logits) * (LOG2_E / GATE_NORMALIZER)
    q = gq * (GLA_K ** -0.5)
    k = gk
    for c in range(n_chunks):
        rows = slice(c * chunk, (c + 1) * chunk)
        e_q, e_k, e_end, e_mid, e_last = decay_factors(g, c)
        qm = q[rows] * e_q
        ke = k[rows] * e_end
        kn_ref[rows, KM_G:KM_G + GLA_QK_WIDTH] = (k[rows] * e_k).astype(BF16)
        for h in range(HEADS):
            pair = slice(LANES * (h // 2), LANES * (h // 2 + 1))
            put_tile(qt_ref, c, HEADS + h, _own_half(qm[:, pair], h).astype(BF16))
            kn_ref[rows, KE_G + LANES * h:KE_G + LANES * (h + 1)] = (
                _own_half(ke[:, pair], h).astype(BF16))
        dec_ref[c, :, HGRN_WIDTH:DEC_COLS] = e_mid
        dec_ref[c, :, DEC_COLS + HGRN_WIDTH:2 * DEC_COLS] = e_last
        next_light()
    while light:
        next_light()


def _meta_kernel(x_ref, g_ref, wt_ref, wgat_ref, wup_ref, bup_ref, lbt_ref,
                 w_ref, wga_ref, kn_ref, q_ref, v_ref, gate_ref, dec_ref, w_scr):
    j = pl.program_id(0)
    n_blocks = w_scr.shape[0]

    @pl.when(j < n_blocks)
    def _():
        blk = wt_ref[...].T.astype(BF16)
        w_ref[0] = blk
        w_scr[j] = blk

    @pl.when(j == n_blocks)
    def _():
        wga_ref[...] = wgat_ref[...].astype(BF16)
        _project(x_ref, g_ref, w_scr, wga_ref, wup_ref, bup_ref, lbt_ref, kn_ref, q_ref, v_ref,
                 gate_ref, dec_ref, N_META, transposed=False)


def _meta(meta_tokens, g, w_in_t, w_up, b_up, lb_table):
    n_blocks = C_GA // W_BLOCK
    full = lambda a: pl.BlockSpec(a.shape, lambda j: (0,) * len(a.shape))
    last = n_blocks - 1
    out_shapes = [jax.ShapeDtypeStruct((n_blocks, D_MODEL, W_BLOCK), BF16),
                  jax.ShapeDtypeStruct((GATE_RANK, D_MODEL), BF16),
                  jax.ShapeDtypeStruct((N_META, KN_COLS), BF16),
                  jax.ShapeDtypeStruct((1, N_META, MIX_WIDTH), BF16),
                  jax.ShapeDtypeStruct((1, N_META, MIX_WIDTH), BF16),
                  jax.ShapeDtypeStruct((N_META, MIX_WIDTH), F32),
                  jax.ShapeDtypeStruct((1, 1, 2 * DEC_COLS), F32)]
    w_main, w_ga, kn, _, v, _, _ = pl.pallas_call(
        _meta_kernel,
        grid=(n_blocks + 1,),
        in_specs=[
            full(meta_tokens), full(g),
            pl.BlockSpec((W_BLOCK, D_MODEL), lambda j: (jnp.minimum(j, last), 0)),
            pl.BlockSpec((GATE_RANK, D_MODEL), lambda j: (C_GA // GATE_RANK, 0)),
            full(w_up), full(b_up), full(lb_table),
        ],
        out_specs=[pl.BlockSpec((1, D_MODEL, W_BLOCK), lambda j: (jnp.minimum(j, last), 0, 0))]
        + [full(s) for s in out_shapes[1:]],
        out_shape=out_shapes,
        scratch_shapes=[pltpu.VMEM((n_blocks, D_MODEL, W_BLOCK), BF16)],
        compiler_params=pltpu.CompilerParams(
            dimension_semantics=("arbitrary",), vmem_limit_bytes=V7X_VMEM_LIMIT),
        name="meta",
    )(meta_tokens, g, w_in_t, w_in_t, w_up, b_up, lb_table)
    return w_main, w_ga, kn, v


def _side_by_side(a, b):
    return jnp.concatenate([a, b], axis=1)


def _block_diag(a, b):
    return jnp.concatenate([_side_by_side(a, jnp.zeros_like(b)),
                            _side_by_side(jnp.zeros_like(a), b)], axis=0)


def _seed_state(mkn_ref, mv_ref, st_ref):
    seed = [_dot_tn(mv_ref[0, :, LANES * i:LANES * (i + 1)], mkn_ref[:, ke_col:ke_col + LANES])
            for i, (_, ke_col, _) in enumerate(HEAD_TABLE)]
    for p in range(N_PAIRS):
        st_ref[p] = _side_by_side(seed[2 * p], seed[2 * p + 1])


def _recurrence_steps(kn_ref, qt_ref, vt_ref, dec_ref, hnorm_ref, gnorm_ref, o_ref, st_ref):
    key_before_query = _iota2(CHUNK, 0) <= _iota2(CHUNK, 1)
    key_before_query = _side_by_side(key_before_query, key_before_query)
    norms = (hnorm_ref[...],) * HEADS + (gnorm_ref[...],) * HEADS

    def pair(fn, p):
        return fn(2 * p), fn(2 * p + 1)

    for c in range(kn_ref.shape[0] // CHUNK):
        rows = slice(c * CHUNK, (c + 1) * CHUNK)
        dec = dec_ref[c]
        km = lambda i: kn_ref[rows, HEAD_TABLE[i][0]:HEAD_TABLE[i][0] + LANES]
        ke = lambda i: kn_ref[rows, HEAD_TABLE[i][1]:HEAD_TABLE[i][1] + LANES]
        e_mid = lambda i: dec[:, HEAD_TABLE[i][2]:HEAD_TABLE[i][2] + LANES]
        e_last = lambda i: dec[:, DEC_COLS + HEAD_TABLE[i][2]:DEC_COLS + HEAD_TABLE[i][2] + LANES]
        qm_t = lambda i: qt_ref[c, LANES * i:LANES * (i + 1), :]
        v_t = lambda i: vt_ref[c, LANES * i:LANES * (i + 1), :]
        first, outs = [], []
        for p in range(N_PAIRS):
            s_mid = st_ref[p] * _side_by_side(*pair(e_mid, p))
            lhs = jnp.concatenate([_side_by_side(*pair(km, p)), s_mid.astype(BF16)], axis=0)
            first.append(jnp.dot(lhs, _block_diag(*pair(qm_t, p)),
                                 preferred_element_type=F32))
        yield
        for p in range(N_PAIRS):
            scores_t = jnp.where(key_before_query, first[p][:CHUNK], 0.0).astype(BF16)
            outs.append(first[p][CHUNK:]
                        + jnp.dot(_side_by_side(*pair(v_t, p)),
                                  _block_diag(scores_t[:, :CHUNK], scores_t[:, CHUNK:]),
                                  preferred_element_type=F32))
        yield
        for p in range(N_PAIRS):
            st_ref[p] = (st_ref[p] * _side_by_side(*pair(e_last, p))
                         + jnp.dot(_side_by_side(*pair(v_t, p)), _block_diag(*pair(ke, p)),
                                   preferred_element_type=F32))
        yield
        for p in range(N_PAIRS):
            o_t = outs[p]
            o_t = o_t * lax.rsqrt(jnp.mean(o_t * o_t, axis=0, keepdims=True) + NORM_EPS)
            o = o_t.T
            for half, i in enumerate((2 * p, 2 * p + 1)):
                o_ref[rows, LANES * i:LANES * (i + 1)] = (
                    o[half * CHUNK:(half + 1) * CHUNK] * norms[i]).astype(o_ref.dtype)
        yield


STAGES_PER_PAUSE = (1, 1, 2, 1, 2, 1, 1, 2, 1, 2, 2)


def _fused_kernel(x_ref, g_ref, w_ref, wga_ref, wup_ref, bup_ref, lbt_ref, wout32_ref, w1_32_ref,
                  w2_32_ref, mkn_ref, mv_ref, hnorm_ref, gnorm_ref,
                  gate_ref, o_ref, wout_ref, w1_ref, w2_ref,
                  kn_scr, qt_scr, vt_scr, dec_scr, st_ref, *, n_tiles, steps_per_seq):
    s = pl.program_id(0)

    @pl.when(lax.rem(s + steps_per_seq - 1, steps_per_seq) == 0)
    def _():
        _seed_state(mkn_ref, mv_ref, st_ref)

    def project(fill, between=None):
        _project(x_ref, g_ref, w_ref, wga_ref, wup_ref, bup_ref, lbt_ref, kn_scr.at[fill],
                 qt_scr.at[fill], vt_scr.at[fill], gate_ref, dec_scr.at[fill], CHUNK,
                 transposed=True, between=between)
        wout_ref[...] = wout32_ref[...].astype(BF16)
        w1_ref[...] = w1_32_ref[...].astype(BF16)
        w2_ref[...] = w2_32_ref[...].astype(BF16)

    def recurrence(drain):
        return _recurrence_steps(kn_scr.at[drain], qt_scr.at[drain], vt_scr.at[drain],
                                 dec_scr.at[drain], hnorm_ref, gnorm_ref, o_ref, st_ref)

    @pl.when(s == 0)
    def _():
        project(0)

    for parity in (0, 1):
        @pl.when((s > 0) & (s < n_tiles) & (lax.rem(s, 2) == parity))
        def _():
            stages = recurrence(1 - parity)
            budget = list(STAGES_PER_PAUSE)

            def between():
                for _ in range(budget.pop(0) if budget else 1):
                    next(stages, None)

            project(parity, between)
            for _ in stages:
                pass

    @pl.when(s == n_tiles)
    def _():
        for _ in recurrence((n_tiles - 1) % 2):
            pass


def _fused(x2d, g, w_main, w_ga, w_up, b_up, lb_table, w_out, w1, w2, meta_kn, meta_v, hnorm,
           gnorm, seq, rows):
    n = x2d.shape[0]
    n_tiles = n // rows
    chunks = rows // CHUNK
    last = n_tiles - 1
    wout_rows = w_out.shape[0] // n_tiles
    w1_rows = w1.shape[0] // n_tiles
    w2_rows = 2 * w2.shape[0] // n_tiles
    assert wout_rows * n_tiles == w_out.shape[0] and w1_rows * n_tiles == w1.shape[0]
    assert w2_rows * n_tiles == 2 * w2.shape[0]
    const = lambda s: (0, 0)
    const3 = lambda s: (0, 0, 0)
    cur = lambda s: (jnp.minimum(s, last), 0)
    prev = lambda s: (jnp.maximum(s - 1, 0), 0)
    half_pace = lambda s: (jnp.minimum(s, last) // 2, 0)
    w_specs = [pl.BlockSpec((wout_rows, w_out.shape[1]), cur),
               pl.BlockSpec((w1_rows, w1.shape[1]), cur),
               pl.BlockSpec((w2_rows, w2.shape[1]), half_pace)]
    return pl.pallas_call(
        functools.partial(_fused_kernel, n_tiles=n_tiles, steps_per_seq=seq // rows),
        grid=(n_tiles + 1,),
        in_specs=[
            pl.BlockSpec((rows, D_MODEL), cur),
            pl.BlockSpec((1, D_MODEL), const),
            pl.BlockSpec(w_main.shape, const3, pipeline_mode=pl.Buffered(1)),
            pl.BlockSpec(w_ga.shape, const),
            pl.BlockSpec((GATE_RANK, GLA_QK_WIDTH), const),
            pl.BlockSpec((1, GLA_QK_WIDTH), const),
            pl.BlockSpec(lb_table.shape, const),
        ] + w_specs + [
            pl.BlockSpec(meta_kn.shape, const),
            pl.BlockSpec(meta_v.shape, const3),
            pl.BlockSpec((1, HEAD_V), const),
            pl.BlockSpec((1, HEAD_V), const),
        ],
        out_specs=[
            pl.BlockSpec((rows, MIX_WIDTH), cur),
            pl.BlockSpec((rows, MIX_WIDTH), prev),
        ] + w_specs,
        out_shape=[
            jax.ShapeDtypeStruct((n, MIX_WIDTH), F32),
            jax.ShapeDtypeStruct((n, MIX_WIDTH), BF16),
            jax.ShapeDtypeStruct(w_out.shape, BF16),
            jax.ShapeDtypeStruct(w1.shape, BF16),
            jax.ShapeDtypeStruct(w2.shape, BF16),
        ],
        scratch_shapes=[pltpu.VMEM((2, rows, KN_COLS), BF16),
                        pltpu.VMEM((2, chunks, MIX_WIDTH, CHUNK), BF16),
                        pltpu.VMEM((2, chunks, MIX_WIDTH, CHUNK), BF16),
                        pltpu.VMEM((2, chunks, 1, 2 * DEC_COLS), F32),
                        pltpu.VMEM((N_PAIRS, HEAD_V, 2 * LANES), F32)],
        compiler_params=pltpu.CompilerParams(
            dimension_semantics=("arbitrary",), vmem_limit_bytes=V7X_VMEM_LIMIT),
        name="mixer",
    )(x2d, g, w_main, w_ga, w_up, b_up, lb_table, w_out, w1, w2, meta_kn, meta_v, hnorm, gnorm)


def _ffn_kernel(x_ref, o_ref, gate_ref, wout_ref, g2_ref, w1_ref, w2_ref, g3_ref, y_ref):
    o = o_ref[...].astype(F32) * _silu(gate_ref[...])
    h = x_ref[...] + _dot(o, wout_ref[...])
    u = _rmsnorm(h, g2_ref[...]).astype(BF16)
    acts = []
    for j in range(0, FFN_HIDDEN, FFN_BLOCK):
        gate = jnp.dot(u, w1_ref[:, j:j + FFN_BLOCK], preferred_element_type=F32)
        up = jnp.dot(u, w1_ref[:, FFN_HIDDEN + j:FFN_HIDDEN + j + FFN_BLOCK],
                     preferred_element_type=F32)
        acts.append((_silu(gate) * up).astype(BF16))
    ffn = jnp.dot(jnp.concatenate(acts, axis=1), w2_ref[...], preferred_element_type=F32)
    y_ref[...] = _rmsnorm(h + ffn, g3_ref[...])


def _ffn(x2d, o, gates, w_out, g2, w1, w2, g3, rows):
    n = x2d.shape[0]
    const = lambda i: (0, 0)
    tile = lambda i: (i, 0)
    resident = functools.partial(pl.BlockSpec, index_map=const, pipeline_mode=pl.Buffered(1))
    return pl.pallas_call(
        _ffn_kernel,
        grid=(n // rows,),
        in_specs=[
            pl.BlockSpec((rows, D_MODEL), tile),
            pl.BlockSpec((rows, MIX_WIDTH), tile),
            pl.BlockSpec((rows, MIX_WIDTH), tile),
            resident((MIX_WIDTH, D_MODEL)),
            pl.BlockSpec((1, D_MODEL), const),
            resident((D_MODEL, 2 * FFN_HIDDEN)),
            resident((FFN_HIDDEN, D_MODEL)),
            pl.BlockSpec((1, D_MODEL), const),
        ],
        out_specs=pl.BlockSpec((rows, D_MODEL), tile),
        out_shape=jax.ShapeDtypeStruct((n, D_MODEL), F32),
        compiler_params=pltpu.CompilerParams(
            dimension_semantics=("arbitrary",), vmem_limit_bytes=V7X_VMEM_LIMIT),
        name="outproj_ffn",
    )(x2d, o, gates, w_out, g2, w1, w2, g3)


def kernel(x, meta_tokens, lb_table, norm_mix_g, w_in, w_gla_gate_up, b_gla_gate, hgrn_norm_g,
           gla_norm_g, w_out, norm_ffn_g, w_ffn_in, w_ffn_out, norm_final_g):
    batch, seq, _ = x.shape
    assert w_in.shape[0] == 1 and lb_table.shape[0] == 2, "single-layer block"
    assert seq % ROWS_MIXER == 0
    x2d = x.reshape(batch * seq, D_MODEL)
    g_mix = norm_mix_g[0][None, :]
    w_up = w_gla_gate_up[0]
    b_up = b_gla_gate[0][None, :]

    w_main, w_ga, meta_kn, meta_v = _meta(meta_tokens, g_mix, w_in[0].T, w_up, b_up, lb_table)
    gates, o, w_out_b, w1_b, w2_b = _fused(
        x2d, g_mix, w_main, w_ga, w_up, b_up, lb_table, w_out[0], w_ffn_in[0], w_ffn_out[0],
        meta_kn, meta_v, hgrn_norm_g[0][None, :], gla_norm_g[0][None, :], seq, ROWS_MIXER)
    y = _ffn(x2d, o, gates, w_out_b, norm_ffn_g[0][None, :], w1_b, w2_b, norm_final_g[None, :],
             ROWS_FFN)
    return y.reshape(batch, seq, D_MODEL)
```

```python
import functools

import jax
import jax.numpy as jnp
from jax import lax
from jax.experimental import pallas as pl
from jax.experimental.pallas import tpu as pltpu

F32 = jnp.float32
BF16 = jnp.bfloat16

D_MODEL = 1024
N_META = 16
CHUNK = 128
HEADS = 4
HEAD_V = 128
HGRN_WIDTH = HEADS * HEAD_V
GLA_K = 64
GLA_QK_WIDTH = HEADS * GLA_K
GLA_WIDTH = HEADS * HEAD_V
MIX_WIDTH = HGRN_WIDTH + GLA_WIDTH
GATE_RANK = 16
GATE_NORMALIZER = 16.0
LOG2_E = 1.4426950408889634
FFN_HIDDEN = 2816
NORM_EPS = 1e-6
LANES = 128
N_PAIRS = HEADS

C_HQ, C_HF, C_HI, C_HG = 0, 512, 1024, 1536
C_GQ, C_GK, C_GV, C_GG = 2048, 2304, 2560, 3072
C_GA = 3584
IN_COLS = C_GA + GATE_RANK
W_BLOCK = 1792

KM_H, KM_G = 0, HGRN_WIDTH
KE_H, KE_G = HGRN_WIDTH + GLA_QK_WIDTH, 2 * HGRN_WIDTH + GLA_QK_WIDTH
KN_COLS = KE_G + GLA_WIDTH
DEC_COLS = HGRN_WIDTH + GLA_QK_WIDTH

HEAD_TABLE = tuple(
    [(KM_H + LANES * h, KE_H + LANES * h, LANES * h) for h in range(HEADS)]
    + [(KM_G + LANES * (h // 2), KE_G + LANES * h, HGRN_WIDTH + LANES * (h // 2))
       for h in range(HEADS)])

V7X_VMEM_LIMIT = 56 * 1024 * 1024

ROWS_MIXER = 512
ROWS_FFN = 512
FFN_BLOCK = 256


def _rmsnorm(x, g):
    return x * lax.rsqrt(jnp.mean(x * x, axis=-1, keepdims=True) + NORM_EPS) * g


def _dot(a, b):
    return jnp.dot(a.astype(BF16), b.astype(BF16), preferred_element_type=F32)


def _dot_nt(a, b):
    return lax.dot_general(a.astype(BF16), b.astype(BF16), (((1,), (1,)), ((), ())),
                           preferred_element_type=F32)


def _dot_tn(a, b):
    return lax.dot_general(a.astype(BF16), b.astype(BF16), (((0,), (0,)), ((), ())),
                           preferred_element_type=F32)


def _sigmoid(x):
    return 0.5 * jnp.tanh(0.5 * x) + 0.5


def _silu(x):
    return x * _sigmoid(x)


def _log_sigmoid(x):
    return jnp.minimum(x, 0.0) - jnp.log1p(jnp.exp(-jnp.abs(x)))


def _iota2(n, axis):
    return lax.broadcasted_iota(jnp.int32, (n, n), axis)


def _cumsum_rows(g, tri2_bf16):
    g_hi = g.astype(BF16)
    g_lo = (g - g_hi.astype(F32)).astype(BF16)
    return jnp.dot(tri2_bf16, jnp.concatenate([g_hi, g_lo], axis=0),
                   preferred_element_type=F32)


def _own_half(x, h):
    lane = lax.broadcasted_iota(jnp.int32, x.shape, 1)
    return jnp.where((lane >= GLA_K) == bool(h % 2), x, 0.0)


def _project(x_ref, g_ref, w_ref, wga_ref, wup_ref, bup_ref, lbt_ref, kn_ref, qt_ref, vt_ref,
             gate_ref, dec_ref, chunk, transposed, between=None):
    n_chunks = x_ref.shape[0] // chunk
    u = _rmsnorm(x_ref[...], g_ref[...]).astype(BF16)
    tri = (_iota2(chunk, 0) >= _iota2(chunk, 1)).astype(BF16)
    tri = jnp.concatenate([tri, tri], axis=1)
    mid = chunk // 2 - 1

    def proj(col, width):
        blk, off = divmod(col, W_BLOCK)
        assert off + width <= W_BLOCK
        return jnp.dot(u, w_ref[blk, :, off:off + width], preferred_element_type=F32)

    def decay_factors(g, c):
        b = _cumsum_rows(g[c * chunk:(c + 1) * chunk], tri)
        b_mid = b[mid:mid + 1, :]
        b_last = b[chunk - 1:, :]
        return (jnp.exp2(b - b_mid), jnp.exp2(b_mid - b), jnp.exp2(b_last - b),
                jnp.exp2(b_mid), jnp.exp2(b_last))

    def put_tile(ref, c, i, tile):
        if transposed:
            ref[c, LANES * i:LANES * (i + 1), :] = tile.T
        else:
            ref[c, :, LANES * i:LANES * (i + 1)] = tile

    t0 = lbt_ref[0:1, :]
    t1 = lbt_ref[1:2, :]
    m = jnp.maximum(t0, t1)
    e0 = jnp.exp(t0 - m)
    lb = e0 / (e0 + jnp.exp(t1 - m))

    def value_piece(col, first_head):
        def emit():
            v = proj(col, 2 * LANES).astype(BF16)
            for c in range(n_chunks):
                for h in range(2):
                    put_tile(vt_ref, c, first_head + h,
                             v[c * chunk:(c + 1) * chunk, LANES * h:LANES * (h + 1)])
        return emit

    def gate_piece(col, out_col):
        def emit():
            gate_ref[:, out_col:out_col + 2 * LANES] = proj(col, 2 * LANES)
        return emit

    light = [value_piece(C_HI, 0), value_piece(C_HI + 2 * LANES, 2),
             value_piece(C_GV, HEADS), value_piece(C_GV + 2 * LANES, HEADS + 2),
             gate_piece(C_HG, 0), gate_piece(C_HG + 2 * LANES, 2 * LANES),
             gate_piece(C_GG, HGRN_WIDTH), gate_piece(C_GG + 2 * LANES, HGRN_WIDTH + 2 * LANES)]

    def pause():
        if between is not None:
            between()

    def next_light():
        if light:
            light.pop(0)()
        pause()

    def ga_part(k):
        cols = slice(k * (D_MODEL // 4), (k + 1) * (D_MODEL // 4))
        return _dot_nt(wga_ref[:, cols], u[:, cols])

    ga_t = ga_part(0)
    hf = proj(C_HF, HGRN_WIDTH)
    ga_t += ga_part(1)
    hq = proj(C_HQ, HGRN_WIDTH)
    ga_t += ga_part(2)
    gq = proj(C_GQ, GLA_QK_WIDTH)
    ga_t += ga_part(3)
    gk = proj(C_GK, GLA_QK_WIDTH)
    pause()

    f = 0.5 * (1.0 + lb) + (0.5 * (1.0 - lb)) * jnp.tanh(0.5 * hf)
    g = jnp.log2(f)
    k = 1.0 - f
    q = _silu(hq)
    pause()
    for c in range(n_chunks):
        rows = slice(c * chunk, (c + 1) * chunk)
        e_q, e_k, e_end, e_mid, e_last = decay_factors(g, c)
        kn_ref[rows, KM_H:KM_H + HGRN_WIDTH] = (k[rows] * e_k).astype(BF16)
        kn_ref[rows, KE_H:KE_H + HGRN_WIDTH] = (k[rows] * e_end).astype(BF16)
        qm = (q[rows] * e_q).astype(BF16)
        for h in range(HEADS):
            put_tile(qt_ref, c, h, qm[:, LANES * h:LANES * (h + 1)])
        dec_ref[c, :, 0:HGRN_WIDTH] = e_mid
        dec_ref[c, :, DEC_COLS:DEC_COLS + HGRN_WIDTH] = e_last
        next_light()

    next_light()
    logits = _dot_tn(ga_t, wup_ref[...]) + bup_ref[...]
    g = _log_sigmoid(logits) * (LOG2_E / GATE_NORMALIZER)
    q = gq * (GLA_K ** -0.5)
    k = gk
    for c in range(n_chunks):
        rows = slice(c * chunk, (c + 1) * chunk)
        e_q, e_k, e_end, e_mid, e_last = decay_factors(g, c)
        qm = q[rows] * e_q
        ke = k[rows] * e_end
        kn_ref[rows, KM_G:KM_G + GLA_QK_WIDTH] = (k[rows] * e_k).astype(BF16)
        for h in range(HEADS):
            pair = slice(LANES * (h // 2), LANES * (h // 2 + 1))
            put_tile(qt_ref, c, HEADS + h, _own_half(qm[:, pair], h).astype(BF16))
            kn_ref[rows, KE_G + LANES * h:KE_G + LANES * (h + 1)] = (
                _own_half(ke[:, pair], h).astype(BF16))
        dec_ref[c, :, HGRN_WIDTH:DEC_COLS] = e_mid
        dec_ref[c, :, DEC_COLS + HGRN_WIDTH:2 * DEC_COLS] = e_last
        next_light()
    while light:
        next_light()


def _meta_kernel(x_ref, g_ref, wt_ref, wgat_ref, wup_ref, bup_ref, lbt_ref,
                 w_ref, wga_ref, kn_ref, v_ref, w_scr, q_ref, gate_ref, dec_ref):
    j = pl.program_id(0)
    n_blocks = w_scr.shape[0]

    @pl.when(j < n_blocks)
    def _():
        blk = wt_ref[...].T.astype(BF16)
        w_ref[0] = blk
        w_scr[j] = blk

    @pl.when(j == n_blocks)
    def _():
        wga_ref[...] = wgat_ref[...].astype(BF16)
        _project(x_ref, g_ref, w_scr, wga_ref, wup_ref, bup_ref, lbt_ref, kn_ref, q_ref, v_ref,
                 gate_ref, dec_ref, N_META, transposed=False)


def _meta(meta_tokens, g, w_in_t, w_up, b_up, lb_table):
    n_blocks = C_GA // W_BLOCK
    full = lambda a: pl.BlockSpec(a.shape, lambda j: (0,) * len(a.shape))
    last = n_blocks - 1
    out_shapes = [jax.ShapeDtypeStruct((n_blocks, D_MODEL, W_BLOCK), BF16),
                  jax.ShapeDtypeStruct((GATE_RANK, D_MODEL), BF16),
                  jax.ShapeDtypeStruct((N_META, KN_COLS), BF16),
                  jax.ShapeDtypeStruct((1, N_META, MIX_WIDTH), BF16)]
    return pl.pallas_call(
        _meta_kernel,
        grid=(n_blocks + 1,),
        in_specs=[
            full(meta_tokens), full(g),
            pl.BlockSpec((W_BLOCK, D_MODEL), lambda j: (jnp.minimum(j, last), 0)),
            pl.BlockSpec((GATE_RANK, D_MODEL), lambda j: (C_GA // GATE_RANK, 0)),
            full(w_up), full(b_up), full(lb_table),
        ],
        out_specs=[pl.BlockSpec((1, D_MODEL, W_BLOCK), lambda j: (jnp.minimum(j, last), 0, 0))]
        + [full(s) for s in out_shapes[1:]],
        out_shape=out_shapes,
        scratch_shapes=[pltpu.VMEM((n_blocks, D_MODEL, W_BLOCK), BF16),
                        pltpu.VMEM((1, N_META, MIX_WIDTH), BF16),
                        pltpu.VMEM((N_META, MIX_WIDTH), F32),
                        pltpu.VMEM((1, 1, 2 * DEC_COLS), F32)],
        compiler_params=pltpu.CompilerParams(
            dimension_semantics=("arbitrary",), vmem_limit_bytes=V7X_VMEM_LIMIT),
        name="meta",
    )(meta_tokens, g, w_in_t, w_in_t, w_up, b_up, lb_table)


def _side_by_side(a, b):
    return jnp.concatenate([a, b], axis=1)


def _block_diag(a, b):
    return jnp.concatenate([_side_by_side(a, jnp.zeros_like(b)),
                            _side_by_side(jnp.zeros_like(a), b)], axis=0)


def _seed_state(mkn_ref, mv_ref, st_ref):
    seed = [_dot_tn(mv_ref[0, :, LANES * i:LANES * (i + 1)], mkn_ref[:, ke_col:ke_col + LANES])
            for i, (_, ke_col, _) in enumerate(HEAD_TABLE)]
    for p in range(N_PAIRS):
        st_ref[p] = _side_by_side(seed[2 * p], seed[2 * p + 1])


def _recurrence_steps(kn_ref, qt_ref, vt_ref, dec_ref, hnorm_ref, gnorm_ref, o_ref, st_ref):
    key_before_query = _iota2(CHUNK, 0) <= _iota2(CHUNK, 1)
    key_before_query = _side_by_side(key_before_query, key_before_query)
    norms = (hnorm_ref[...],) * HEADS + (gnorm_ref[...],) * HEADS

    def pair(fn, p):
        return fn(2 * p), fn(2 * p + 1)

    for c in range(kn_ref.shape[0] // CHUNK):
        rows = slice(c * CHUNK, (c + 1) * CHUNK)
        dec = dec_ref[c]
        km = lambda i: kn_ref[rows, HEAD_TABLE[i][0]:HEAD_TABLE[i][0] + LANES]
        ke = lambda i: kn_ref[rows, HEAD_TABLE[i][1]:HEAD_TABLE[i][1] + LANES]
        e_mid = lambda i: dec[:, HEAD_TABLE[i][2]:HEAD_TABLE[i][2] + LANES]
        e_last = lambda i: dec[:, DEC_COLS + HEAD_TABLE[i][2]:DEC_COLS + HEAD_TABLE[i][2] + LANES]
        qm_t = lambda i: qt_ref[c, LANES * i:LANES * (i + 1), :]
        v_t = lambda i: vt_ref[c, LANES * i:LANES * (i + 1), :]
        first, outs = [], []
        for p in range(N_PAIRS):
            s_mid = st_ref[p] * _side_by_side(*pair(e_mid, p))
            lhs = jnp.concatenate([_side_by_side(*pair(km, p)), s_mid.astype(BF16)], axis=0)
            first.append(jnp.dot(lhs, _block_diag(*pair(qm_t, p)),
                                 preferred_element_type=F32))
        yield
        for p in range(N_PAIRS):
            scores_t = jnp.where(key_before_query, first[p][:CHUNK], 0.0).astype(BF16)
            outs.append(first[p][CHUNK:]
                        + jnp.dot(_side_by_side(*pair(v_t, p)),
                                  _block_diag(scores_t[:, :CHUNK], scores_t[:, CHUNK:]),
                                  preferred_element_type=F32))
        yield
        for p in range(N_PAIRS):
            st_ref[p] = (st_ref[p] * _side_by_side(*pair(e_last, p))
                         + jnp.dot(_side_by_side(*pair(v_t, p)), _block_diag(*pair(ke, p)),
                                   preferred_element_type=F32))
        yield
        for p in range(N_PAIRS):
            o_t = outs[p]
            o_t = o_t * lax.rsqrt(jnp.mean(o_t * o_t, axis=0, keepdims=True) + NORM_EPS)
            o = o_t.T
            for half, i in enumerate((2 * p, 2 * p + 1)):
                o_ref[rows, LANES * i:LANES * (i + 1)] = (
                    o[half * CHUNK:(half + 1) * CHUNK] * norms[i]).astype(o_ref.dtype)
        yield


STAGES_PER_PAUSE = (1, 1, 2, 1, 2, 1, 1, 2, 1, 2, 2)


def _fused_kernel(x_ref, g_ref, w_ref, wga_ref, wup_ref, bup_ref, lbt_ref, wout32_ref, w1_32_ref,
                  w2_32_ref, mkn_ref, mv_ref, hnorm_ref, gnorm_ref,
                  gate_ref, o_ref, wout_ref, w1_ref, w2_ref,
                  kn_scr, qt_scr, vt_scr, dec_scr, st_ref, *, n_tiles, steps_per_seq):
    s = pl.program_id(0)

    @pl.when(lax.rem(s + steps_per_seq - 1, steps_per_seq) == 0)
    def _():
        _seed_state(mkn_ref, mv_ref, st_ref)

    def project(fill, between=None):
        _project(x_ref, g_ref, w_ref, wga_ref, wup_ref, bup_ref, lbt_ref, kn_scr.at[fill],
                 qt_scr.at[fill], vt_scr.at[fill], gate_ref, dec_scr.at[fill], CHUNK,
                 transposed=True, between=between)
        wout_ref[...] = wout32_ref[...].astype(BF16)
        w1_ref[...] = w1_32_ref[...].astype(BF16)
        w2_ref[...] = w2_32_ref[...].astype(BF16)

    def recurrence(drain):
        return _recurrence_steps(kn_scr.at[drain], qt_scr.at[drain], vt_scr.at[drain],
                                 dec_scr.at[drain], hnorm_ref, gnorm_ref, o_ref, st_ref)

    @pl.when(s == 0)
    def _():
        project(0)

    for parity in (0, 1):
        @pl.when((s > 0) & (s < n_tiles) & (lax.rem(s, 2) == parity))
        def _():
            stages = recurrence(1 - parity)
            budget = list(STAGES_PER_PAUSE)

            def between():
                for _ in range(budget.pop(0) if budget else 1):
                    next(stages, None)

            project(parity, between)
            for _ in stages:
                pass

    @pl.when(s == n_tiles)
    def _():
        for _ in recurrence((n_tiles - 1) % 2):
            pass


def _fused(x2d, g, w_main, w_ga, w_up, b_up, lb_table, w_out, w1, w2, meta_kn, meta_v, hnorm,
           gnorm, seq, rows):
    n = x2d.shape[0]
    n_tiles = n // rows
    chunks = rows // CHUNK
    last = n_tiles - 1
    wout_rows = w_out.shape[0] // n_tiles
    w1_rows = w1.shape[0] // n_tiles
    w2_rows = 2 * w2.shape[0] // n_tiles
    assert wout_rows * n_tiles == w_out.shape[0] and w1_rows * n_tiles == w1.shape[0]
    assert w2_rows * n_tiles == 2 * w2.shape[0]
    const = lambda s: (0, 0)
    const3 = lambda s: (0, 0, 0)
    cur = lambda s: (jnp.minimum(s, last), 0)
    prev = lambda s: (jnp.maximum(s - 1, 0), 0)
    half_pace = lambda s: (jnp.minimum(s, last) // 2, 0)
    w_specs = [pl.BlockSpec((wout_rows, w_out.shape[1]), cur),
               pl.BlockSpec((w1_rows, w1.shape[1]), cur),
               pl.BlockSpec((w2_rows, w2.shape[1]), half_pace)]
    return pl.pallas_call(
        functools.partial(_fused_kernel, n_tiles=n_tiles, steps_per_seq=seq // rows),
        grid=(n_tiles + 1,),
        in_specs=[
            pl.BlockSpec((rows, D_MODEL), cur),
            pl.BlockSpec((1, D_MODEL), const),
            pl.BlockSpec(w_main.shape, const3, pipeline_mode=pl.Buffered(1)),
            pl.BlockSpec(w_ga.shape, const),
            pl.BlockSpec((GATE_RANK, GLA_QK_WIDTH), const),
            pl.BlockSpec((1, GLA_QK_WIDTH), const),
            pl.BlockSpec(lb_table.shape, const),
        ] + w_specs + [
            pl.BlockSpec(meta_kn.shape, const),
            pl.BlockSpec(meta_v.shape, const3),
            pl.BlockSpec((1, HEAD_V), const),
            pl.BlockSpec((1, HEAD_V), const),
        ],
        out_specs=[
            pl.BlockSpec((rows, MIX_WIDTH), cur),
            pl.BlockSpec((rows, MIX_WIDTH), prev),
        ] + w_specs,
        out_shape=[
            jax.ShapeDtypeStruct((n, MIX_WIDTH), F32),
            jax.ShapeDtypeStruct((n, MIX_WIDTH), BF16),
            jax.ShapeDtypeStruct(w_out.shape, BF16),
            jax.ShapeDtypeStruct(w1.shape, BF16),
            jax.ShapeDtypeStruct(w2.shape, BF16),
        ],
        scratch_shapes=[pltpu.VMEM((2, rows, KN_COLS), BF16),
                        pltpu.VMEM((2, chunks, MIX_WIDTH, CHUNK), BF16),
                        pltpu.VMEM((2, chunks, MIX_WIDTH, CHUNK), BF16),
                        pltpu.VMEM((2, chunks, 1, 2 * DEC_COLS), F32),
                        pltpu.VMEM((N_PAIRS, HEAD_V, 2 * LANES), F32)],
        compiler_params=pltpu.CompilerParams(
            dimension_semantics=("arbitrary",), vmem_limit_bytes=V7X_VMEM_LIMIT),
        name="mixer",
    )(x2d, g, w_main, w_ga, w_up, b_up, lb_table, w_out, w1, w2, meta_kn, meta_v, hnorm, gnorm)


def _ffn_kernel(x_ref, o_ref, gate_ref, wout_ref, g2_ref, w1_ref, w2_ref, g3_ref, y_ref):
    o = o_ref[...].astype(F32) * _silu(gate_ref[...])
    h = x_ref[...] + _dot(o, wout_ref[...])
    u = _rmsnorm(h, g2_ref[...]).astype(BF16)
    acts = []
    for j in range(0, FFN_HIDDEN, FFN_BLOCK):
        gate = jnp.dot(u, w1_ref[:, j:j + FFN_BLOCK], preferred_element_type=F32)
        up = jnp.dot(u, w1_ref[:, FFN_HIDDEN + j:FFN_HIDDEN + j + FFN_BLOCK],
                     preferred_element_type=F32)
        acts.append((_silu(gate) * up).astype(BF16))
    ffn = jnp.dot(jnp.concatenate(acts, axis=1), w2_ref[...], preferred_element_type=F32)
    y_ref[...] = _rmsnorm(h + ffn, g3_ref[...])


def _ffn(x2d, o, gates, w_out, g2, w1, w2, g3, rows):
    n = x2d.shape[0]
    const = lambda i: (0, 0)
    tile = lambda i: (i, 0)
    resident = functools.partial(pl.BlockSpec, index_map=const, pipeline_mode=pl.Buffered(1))
    return pl.pallas_call(
        _ffn_kernel,
        grid=(n // rows,),
        in_specs=[
            pl.BlockSpec((rows, D_MODEL), tile),
            pl.BlockSpec((rows, MIX_WIDTH), tile),
            pl.BlockSpec((rows, MIX_WIDTH), tile),
            resident((MIX_WIDTH, D_MODEL)),
            pl.BlockSpec((1, D_MODEL), const),
            resident((D_MODEL, 2 * FFN_HIDDEN)),
            resident((FFN_HIDDEN, D_MODEL)),
            pl.BlockSpec((1, D_MODEL), const),
        ],
        out_specs=pl.BlockSpec((rows, D_MODEL), tile),
        out_shape=jax.ShapeDtypeStruct((n, D_MODEL), F32),
        compiler_params=pltpu.CompilerParams(
            dimension_semantics=("arbitrary",), vmem_limit_bytes=V7X_VMEM_LIMIT),
        name="outproj_ffn",
    )(x2d, o, gates, w_out, g2, w1, w2, g3)


def kernel(x, meta_tokens, lb_table, norm_mix_g, w_in, w_gla_gate_up, b_gla_gate, hgrn_norm_g,
           gla_norm_g, w_out, norm_ffn_g, w_ffn_in, w_ffn_out, norm_final_g):
    batch, seq, _ = x.shape
    assert w_in.shape[0] == 1 and lb_table.shape[0] == 2, "single-layer block"
    assert seq % ROWS_MIXER == 0
    x2d = x.reshape(batch * seq, D_MODEL)
    g_mix = norm_mix_g[0][None, :]
    w_up = w_gla_gate_up[0]
    b_up = b_gla_gate[0][None, :]

    w_main, w_ga, meta_kn, meta_v = _meta(meta_tokens, g_mix, w_in[0].T, w_up, b_up, lb_table)
    gates, o, w_out_b, w1_b, w2_b = _fused(
        x2d, g_mix, w_main, w_ga, w_up, b_up, lb_table, w_out[0], w_ffn_in[0], w_ffn_out[0],
        meta_kn, meta_v, hgrn_norm_g[0][None, :], gla_norm_g[0][None, :], seq, ROWS_MIXER)
    y = _ffn(x2d, o, gates, w_out_b, norm_ffn_g[0][None, :], w1_b, w2_b, norm_final_g[None, :],
             ROWS_FFN)
    return y.reshape(batch, seq, D_MODEL)
```

```python
import functools

import jax
import jax.numpy as jnp
from jax import lax
from jax.experimental import pallas as pl
from jax.experimental.pallas import tpu as pltpu

F32 = jnp.float32
BF16 = jnp.bfloat16

D_MODEL = 1024
N_META = 16
CHUNK = 128
HEADS = 4
HEAD_V = 128
HGRN_WIDTH = HEADS * HEAD_V
GLA_K = 64
GLA_QK_WIDTH = HEADS * GLA_K
GLA_WIDTH = HEADS * HEAD_V
MIX_WIDTH = HGRN_WIDTH + GLA_WIDTH
GATE_RANK = 16
GATE_NORMALIZER = 16.0
LOG2_E = 1.4426950408889634
FFN_HIDDEN = 2816
NORM_EPS = 1e-6
LANES = 128
N_PAIRS = HEADS

C_HQ, C_HF, C_HI, C_HG = 0, 512, 1024, 1536
C_GQ, C_GK, C_GV, C_GG = 2048, 2304, 2560, 3072
C_GA = 3584
IN_COLS = C_GA + GATE_RANK
W_BLOCK = 1792

KM_H, KM_G = 0, HGRN_WIDTH
KE_H, KE_G = HGRN_WIDTH + GLA_QK_WIDTH, 2 * HGRN_WIDTH + GLA_QK_WIDTH
KN_COLS = KE_G + GLA_WIDTH
DEC_COLS = HGRN_WIDTH + GLA_QK_WIDTH

HEAD_TABLE = tuple(
    [(KM_H + LANES * h, KE_H + LANES * h, LANES * h) for h in range(HEADS)]
    + [(KM_G + LANES * (h // 2), KE_G + LANES * h, HGRN_WIDTH + LANES * (h // 2))
       for h in range(HEADS)])

V7X_VMEM_LIMIT = 58 * 1024 * 1024

ROWS_MIXER = 512
ROWS_FFN = 1024
FFN_BLOCK = 256


def _rmsnorm(x, g):
    return x * lax.rsqrt(jnp.mean(x * x, axis=-1, keepdims=True) + NORM_EPS) * g


def _dot(a, b):
    return jnp.dot(a.astype(BF16), b.astype(BF16), preferred_element_type=F32)


def _dot_nt(a, b):
    return lax.dot_general(a.astype(BF16), b.astype(BF16), (((1,), (1,)), ((), ())),
                           preferred_element_type=F32)


def _dot_tn(a, b):
    return lax.dot_general(a.astype(BF16), b.astype(BF16), (((0,), (0,)), ((), ())),
                           preferred_element_type=F32)


def _sigmoid(x):
    return 0.5 * jnp.tanh(0.5 * x) + 0.5


def _silu(x):
    return x * _sigmoid(x)


def _log_sigmoid(x):
    return jnp.minimum(x, 0.0) - jnp.log1p(jnp.exp(-jnp.abs(x)))


def _iota2(n, axis):
    return lax.broadcasted_iota(jnp.int32, (n, n), axis)


def _cumsum_rows(g, tri2_bf16):
    g_hi = g.astype(BF16)
    g_lo = (g - g_hi.astype(F32)).astype(BF16)
    return jnp.dot(tri2_bf16, jnp.concatenate([g_hi, g_lo], axis=0),
                   preferred_element_type=F32)


def _own_half(x, h):
    lane = lax.broadcasted_iota(jnp.int32, x.shape, 1)
    return jnp.where((lane >= GLA_K) == bool(h % 2), x, 0.0)


def _project(x_ref, g_ref, w_ref, wga_ref, wup_ref, bup_ref, lbt_ref, kn_ref, qt_ref, vt_ref,
             gate_ref, dec_ref, chunk, transposed, between=None):
    n_chunks = x_ref.shape[0] // chunk
    u = _rmsnorm(x_ref[...], g_ref[...]).astype(BF16)
    tri = (_iota2(chunk, 0) >= _iota2(chunk, 1)).astype(BF16)
    tri = jnp.concatenate([tri, tri], axis=1)
    mid = chunk // 2 - 1

    def proj(col, width):
        blk, off = divmod(col, W_BLOCK)
        assert off + width <= W_BLOCK
        return jnp.dot(u, w_ref[blk, :, off:off + width], preferred_element_type=F32)

    def decay_factors(g, c):
        b = _cumsum_rows(g[c * chunk:(c + 1) * chunk], tri)
        b_mid = b[mid:mid + 1, :]
        b_last = b[chunk - 1:, :]
        return (jnp.exp2(b - b_mid), jnp.exp2(b_mid - b), jnp.exp2(b_last - b),
                jnp.exp2(b_mid), jnp.exp2(b_last))

    def put_tile(ref, c, i, tile):
        if transposed:
            ref[c, LANES * i:LANES * (i + 1), :] = tile.T
        else:
            ref[c, :, LANES * i:LANES * (i + 1)] = tile

    t0 = lbt_ref[0:1, :]
    t1 = lbt_ref[1:2, :]
    m = jnp.maximum(t0, t1)
    e0 = jnp.exp(t0 - m)
    lb = e0 / (e0 + jnp.exp(t1 - m))

    def value_piece(col, first_head):
        def emit():
            v = proj(col, 2 * LANES).astype(BF16)
            for c in range(n_chunks):
                for h in range(2):
                    put_tile(vt_ref, c, first_head + h,
                             v[c * chunk:(c + 1) * chunk, LANES * h:LANES * (h + 1)])
        return emit

    def gate_piece(col, out_col):
        def emit():
            gate_ref[:, out_col:out_col + 2 * LANES] = proj(col, 2 * LANES)
        return emit

    light = [value_piece(C_HI, 0), value_piece(C_HI + 2 * LANES, 2),
             value_piece(C_GV, HEADS), value_piece(C_GV + 2 * LANES, HEADS + 2),
             gate_piece(C_HG, 0), gate_piece(C_HG + 2 * LANES, 2 * LANES),
             gate_piece(C_GG, HGRN_WIDTH), gate_piece(C_GG + 2 * LANES, HGRN_WIDTH + 2 * LANES)]

    def pause():
        if between is not None:
            between()

    def next_light():
        if light:
            light.pop(0)()
        pause()

    def ga_part(k):
        cols = slice(k * (D_MODEL // 4), (k + 1) * (D_MODEL // 4))
        return _dot_nt(wga_ref[:, cols], u[:, cols])

    ga_t = ga_part(0)
    hf = proj(C_HF, HGRN_WIDTH)
    ga_t += ga_part(1)
    hq = proj(C_HQ, HGRN_WIDTH)
    ga_t += ga_part(2)
    gq = proj(C_GQ, GLA_QK_WIDTH)
    ga_t += ga_part(3)
    gk = proj(C_GK, GLA_QK_WIDTH)
    pause()

    f = 0.5 * (1.0 + lb) + (0.5 * (1.0 - lb)) * jnp.tanh(0.5 * hf)
    g = jnp.log2(f)
    k = 1.0 - f
    q = _silu(hq)
    pause()
    for c in range(n_chunks):
        rows = slice(c * chunk, (c + 1) * chunk)
        e_q, e_k, e_end, e_mid, e_last = decay_factors(g, c)
        kn_ref[rows, KM_H:KM_H + HGRN_WIDTH] = (k[rows] * e_k).astype(BF16)
        kn_ref[rows, KE_H:KE_H + HGRN_WIDTH] = (k[rows] * e_end).astype(BF16)
        qm = (q[rows] * e_q).astype(BF16)
        for h in range(HEADS):
            put_tile(qt_ref, c, h, qm[:, LANES * h:LANES * (h + 1)])
        dec_ref[c, :, 0:HGRN_WIDTH] = e_mid
        dec_ref[c, :, DEC_COLS:DEC_COLS + HGRN_WIDTH] = e_last
        next_light()

    next_light()
    logits = _dot_tn(ga_t, wup_ref[...]) + bup_ref[...]
    g = _log_sigmoid(logits) * (LOG2_E / GATE_NORMALIZER)
    q = gq * (GLA_K ** -0.5)
    k = gk
    for c in range(n_chunks):
        rows = slice(c * chunk, (c + 1) * chunk)
        e_q, e_k, e_end, e_mid, e_last = decay_factors(g, c)
        qm = q[rows] * e_q
        ke = k[rows] * e_end
        kn_ref[rows, KM_G:KM_G + GLA_QK_WIDTH] = (k[rows] * e_k).astype(BF16)
        for h in range(HEADS):
            pair = slice(LANES * (h // 2), LANES * (h // 2 + 1))
            put_tile(qt_ref, c, HEADS + h, _own_half(qm[:, pair], h).astype(BF16))
            kn_ref[rows, KE_G + LANES * h:KE_G + LANES * (h + 1)] = (
                _own_half(ke[:, pair], h).astype(BF16))
        dec_ref[c, :, HGRN_WIDTH:DEC_COLS] = e_mid
        dec_ref[c, :, DEC_COLS + HGRN_WIDTH:2 * DEC_COLS] = e_last
        next_light()
    while light:
        next_light()


def _meta_kernel(x_ref, g_ref, wt_ref, wgat_ref, wup_ref, bup_ref, lbt_ref,
                 w_ref, wga_ref, kn_ref, v_ref, w_scr, q_ref, gate_ref, dec_ref):
    j = pl.program_id(0)
    n_blocks = w_scr.shape[0]

    @pl.when(j < n_blocks)
    def _():
        blk = wt_ref[...].T.astype(BF16)
        w_ref[0] = blk
        w_scr[j] = blk

    @pl.when(j == n_blocks)
    def _():
        wga_ref[...] = wgat_ref[...].astype(BF16)
        _project(x_ref, g_ref, w_scr, wga_ref, wup_ref, bup_ref, lbt_ref, kn_ref, q_ref, v_ref,
                 gate_ref, dec_ref, N_META, transposed=False)


def _meta(meta_tokens, g, w_in_t, w_up, b_up, lb_table):
    n_blocks = C_GA // W_BLOCK
    full = lambda a: pl.BlockSpec(a.shape, lambda j: (0,) * len(a.shape))
    last = n_blocks - 1
    out_shapes = [jax.ShapeDtypeStruct((n_blocks, D_MODEL, W_BLOCK), BF16),
                  jax.ShapeDtypeStruct((GATE_RANK, D_MODEL), BF16),
                  jax.ShapeDtypeStruct((N_META, KN_COLS), BF16),
                  jax.ShapeDtypeStruct((1, N_META, MIX_WIDTH), BF16)]
    return pl.pallas_call(
        _meta_kernel,
        grid=(n_blocks + 1,),
        in_specs=[
            full(meta_tokens), full(g),
            pl.BlockSpec((W_BLOCK, D_MODEL), lambda j: (jnp.minimum(j, last), 0)),
            pl.BlockSpec((GATE_RANK, D_MODEL), lambda j: (C_GA // GATE_RANK, 0)),
            full(w_up), full(b_up), full(lb_table),
        ],
        out_specs=[pl.BlockSpec((1, D_MODEL, W_BLOCK), lambda j: (jnp.minimum(j, last), 0, 0))]
        + [full(s) for s in out_shapes[1:]],
        out_shape=out_shapes,
        scratch_shapes=[pltpu.VMEM((n_blocks, D_MODEL, W_BLOCK), BF16),
                        pltpu.VMEM((1, N_META, MIX_WIDTH), BF16),
                        pltpu.VMEM((N_META, MIX_WIDTH), F32),
                        pltpu.VMEM((1, 1, 2 * DEC_COLS), F32)],
        compiler_params=pltpu.CompilerParams(
            dimension_semantics=("arbitrary",), vmem_limit_bytes=V7X_VMEM_LIMIT),
        name="meta",
    )(meta_tokens, g, w_in_t, w_in_t, w_up, b_up, lb_table)


def _side_by_side(a, b):
    return jnp.concatenate([a, b], axis=1)


def _block_diag(a, b):
    return jnp.concatenate([_side_by_side(a, jnp.zeros_like(b)),
                            _side_by_side(jnp.zeros_like(a), b)], axis=0)


def _seed_state(mkn_ref, mv_ref, st_ref):
    seed = [_dot_tn(mv_ref[0, :, LANES * i:LANES * (i + 1)], mkn_ref[:, ke_col:ke_col + LANES])
            for i, (_, ke_col, _) in enumerate(HEAD_TABLE)]
    for p in range(N_PAIRS):
        st_ref[p] = _side_by_side(seed[2 * p], seed[2 * p + 1])


def _recurrence_steps(kn_ref, qt_ref, vt_ref, dec_ref, hnorm_ref, gnorm_ref, o_ref, st_ref):
    key_before_query = _iota2(CHUNK, 0) <= _iota2(CHUNK, 1)
    key_before_query = _side_by_side(key_before_query, key_before_query)
    norms = (hnorm_ref[...],) * HEADS + (gnorm_ref[...],) * HEADS

    def pair(fn, p):
        return fn(2 * p), fn(2 * p + 1)

    for c in range(kn_ref.shape[0] // CHUNK):
        rows = slice(c * CHUNK, (c + 1) * CHUNK)
        dec = dec_ref[c]
        km = lambda i: kn_ref[rows, HEAD_TABLE[i][0]:HEAD_TABLE[i][0] + LANES]
        ke = lambda i: kn_ref[rows, HEAD_TABLE[i][1]:HEAD_TABLE[i][1] + LANES]
        e_mid = lambda i: dec[:, HEAD_TABLE[i][2]:HEAD_TABLE[i][2] + LANES]
        e_last = lambda i: dec[:, DEC_COLS + HEAD_TABLE[i][2]:DEC_COLS + HEAD_TABLE[i][2] + LANES]
        qm_t = lambda i: qt_ref[c, LANES * i:LANES * (i + 1), :]
        v_t = lambda i: vt_ref[c, LANES * i:LANES * (i + 1), :]
        first, outs = [], []
        for p in range(N_PAIRS):
            s_mid = st_ref[p] * _side_by_side(*pair(e_mid, p))
            lhs = jnp.concatenate([_side_by_side(*pair(km, p)), s_mid.astype(BF16)], axis=0)
            first.append(jnp.dot(lhs, _block_diag(*pair(qm_t, p)),
                                 preferred_element_type=F32))
        yield
        for p in range(N_PAIRS):
            scores_t = jnp.where(key_before_query, first[p][:CHUNK], 0.0).astype(BF16)
            outs.append(first[p][CHUNK:]
                        + jnp.dot(_side_by_side(*pair(v_t, p)),
                                  _block_diag(scores_t[:, :CHUNK], scores_t[:, CHUNK:]),
                                  preferred_element_type=F32))
        yield
        for p in range(N_PAIRS):
            st_ref[p] = (st_ref[p] * _side_by_side(*pair(e_last, p))
                         + jnp.dot(_side_by_side(*pair(v_t, p)), _block_diag(*pair(ke, p)),
                                   preferred_element_type=F32))
        yield
        for p in range(N_PAIRS):
            o_t = outs[p]
            o_t = o_t * lax.rsqrt(jnp.mean(o_t * o_t, axis=0, keepdims=True) + NORM_EPS)
            o = o_t.T
            for half, i in enumerate((2 * p, 2 * p + 1)):
                o_ref[rows, LANES * i:LANES * (i + 1)] = (
                    o[half * CHUNK:(half + 1) * CHUNK] * norms[i]).astype(o_ref.dtype)
        yield


STAGES_PER_PAUSE = (1, 1, 2, 1, 2, 1, 1, 2, 1, 2, 2)


def _fused_kernel(x_ref, g_ref, w_ref, wga_ref, wup_ref, bup_ref, lbt_ref, wout32_ref, w1_32_ref,
                  w2_32_ref, mkn_ref, mv_ref, hnorm_ref, gnorm_ref,
                  gate_ref, o_ref, wout_ref, w1_ref, w2_ref,
                  kn_scr, qt_scr, vt_scr, dec_scr, st_ref, *, n_tiles, steps_per_seq):
    s = pl.program_id(0)

    @pl.when(lax.rem(s + steps_per_seq - 1, steps_per_seq) == 0)
    def _():
        _seed_state(mkn_ref, mv_ref, st_ref)

    def project(fill, between=None):
        _project(x_ref, g_ref, w_ref, wga_ref, wup_ref, bup_ref, lbt_ref, kn_scr.at[fill],
                 qt_scr.at[fill], vt_scr.at[fill], gate_ref, dec_scr.at[fill], CHUNK,
                 transposed=True, between=between)
        wout_ref[...] = wout32_ref[...].astype(BF16)
        w1_ref[...] = w1_32_ref[...].astype(BF16)
        w2_ref[...] = w2_32_ref[...].astype(BF16)

    def recurrence(drain):
        return _recurrence_steps(kn_scr.at[drain], qt_scr.at[drain], vt_scr.at[drain],
                                 dec_scr.at[drain], hnorm_ref, gnorm_ref, o_ref, st_ref)

    @pl.when(s == 0)
    def _():
        project(0)

    for parity in (0, 1):
        @pl.when((s > 0) & (s < n_tiles) & (lax.rem(s, 2) == parity))
        def _():
            stages = recurrence(1 - parity)
            budget = list(STAGES_PER_PAUSE)

            def between():
                for _ in range(budget.pop(0) if budget else 1):
                    next(stages, None)

            project(parity, between)
            for _ in stages:
                pass

    @pl.when(s == n_tiles)
    def _():
        for _ in recurrence((n_tiles - 1) % 2):
            pass


def _fused(x2d, g, w_main, w_ga, w_up, b_up, lb_table, w_out, w1, w2, meta_kn, meta_v, hnorm,
           gnorm, seq, rows):
    n = x2d.shape[0]
    n_tiles = n // rows
    chunks = rows // CHUNK
    last = n_tiles - 1
    wout_rows = w_out.shape[0] // n_tiles
    w1_rows = w1.shape[0] // n_tiles
    w2_rows = 2 * w2.shape[0] // n_tiles
    assert wout_rows * n_tiles == w_out.shape[0] and w1_rows * n_tiles == w1.shape[0]
    assert w2_rows * n_tiles == 2 * w2.shape[0]
    const = lambda s: (0, 0)
    const3 = lambda s: (0, 0, 0)
    cur = lambda s: (jnp.minimum(s, last), 0)
    prev = lambda s: (jnp.maximum(s - 1, 0), 0)
    half_pace = lambda s: (jnp.minimum(s, last) // 2, 0)
    w_specs = [pl.BlockSpec((wout_rows, w_out.shape[1]), cur),
               pl.BlockSpec((w1_rows, w1.shape[1]), cur),
               pl.BlockSpec((w2_rows, w2.shape[1]), half_pace)]
    return pl.pallas_call(
        functools.partial(_fused_kernel, n_tiles=n_tiles, steps_per_seq=seq // rows),
        grid=(n_tiles + 1,),
        in_specs=[
            pl.BlockSpec((rows, D_MODEL), cur),
            pl.BlockSpec((1, D_MODEL), const),
            pl.BlockSpec(w_main.shape, const3, pipeline_mode=pl.Buffered(1)),
            pl.BlockSpec(w_ga.shape, const),
            pl.BlockSpec((GATE_RANK, GLA_QK_WIDTH), const),
            pl.BlockSpec((1, GLA_QK_WIDTH), const),
            pl.BlockSpec(lb_table.shape, const),
        ] + w_specs + [
            pl.BlockSpec(meta_kn.shape, const),
            pl.BlockSpec(meta_v.shape, const3),
            pl.BlockSpec((1, HEAD_V), const),
            pl.BlockSpec((1, HEAD_V), const),
        ],
        out_specs=[
            pl.BlockSpec((rows, MIX_WIDTH), cur),
            pl.BlockSpec((rows, MIX_WIDTH), prev),
        ] + w_specs,
        out_shape=[
            jax.ShapeDtypeStruct((n, MIX_WIDTH), F32),
            jax.ShapeDtypeStruct((n, MIX_WIDTH), BF16),
            jax.ShapeDtypeStruct(w_out.shape, BF16),
            jax.ShapeDtypeStruct(w1.shape, BF16),
            jax.ShapeDtypeStruct(w2.shape, BF16),
        ],
        scratch_shapes=[pltpu.VMEM((2, rows, KN_COLS), BF16),
                        pltpu.VMEM((2, chunks, MIX_WIDTH, CHUNK), BF16),
                        pltpu.VMEM((2, chunks, MIX_WIDTH, CHUNK), BF16),
                        pltpu.VMEM((2, chunks, 1, 2 * DEC_COLS), F32),
                        pltpu.VMEM((N_PAIRS, HEAD_V, 2 * LANES), F32)],
        compiler_params=pltpu.CompilerParams(
            dimension_semantics=("arbitrary",), vmem_limit_bytes=V7X_VMEM_LIMIT),
        name="mixer",
    )(x2d, g, w_main, w_ga, w_up, b_up, lb_table, w_out, w1, w2, meta_kn, meta_v, hnorm, gnorm)


def _ffn_kernel(x_ref, o_ref, gate_ref, wout_ref, g2_ref, w1_ref, w2_ref, g3_ref, y_ref):
    o = o_ref[...].astype(F32) * _silu(gate_ref[...])
    h = x_ref[...] + _dot(o, wout_ref[...])
    u = _rmsnorm(h, g2_ref[...]).astype(BF16)
    acts = []
    for j in range(0, FFN_HIDDEN, FFN_BLOCK):
        gate = jnp.dot(u, w1_ref[:, j:j + FFN_BLOCK], preferred_element_type=F32)
        up = jnp.dot(u, w1_ref[:, FFN_HIDDEN + j:FFN_HIDDEN + j + FFN_BLOCK],
                     preferred_element_type=F32)
        acts.append((_silu(gate) * up).astype(BF16))
    ffn = jnp.dot(jnp.concatenate(acts, axis=1), w2_ref[...], preferred_element_type=F32)
    y_ref[...] = _rmsnorm(h + ffn, g3_ref[...])


def _ffn(x2d, o, gates, w_out, g2, w1, w2, g3, rows):
    n = x2d.shape[0]
    const = lambda i: (0, 0)
    tile = lambda i: (i, 0)
    resident = functools.partial(pl.BlockSpec, index_map=const, pipeline_mode=pl.Buffered(1))
    return pl.pallas_call(
        _ffn_kernel,
        grid=(n // rows,),
        in_specs=[
            pl.BlockSpec((rows, D_MODEL), tile),
            pl.BlockSpec((rows, MIX_WIDTH), tile),
            pl.BlockSpec((rows, MIX_WIDTH), tile),
            resident((MIX_WIDTH, D_MODEL)),
            pl.BlockSpec((1, D_MODEL), const),
            resident((D_MODEL, 2 * FFN_HIDDEN)),
            resident((FFN_HIDDEN, D_MODEL)),
            pl.BlockSpec((1, D_MODEL), const),
        ],
        out_specs=pl.BlockSpec((rows, D_MODEL), tile),
        out_shape=jax.ShapeDtypeStruct((n, D_MODEL), F32),
        compiler_params=pltpu.CompilerParams(
            dimension_semantics=("arbitrary",), vmem_limit_bytes=V7X_VMEM_LIMIT),
        name="outproj_ffn",
    )(x2d, o, gates, w_out, g2, w1, w2, g3)


def kernel(x, meta_tokens, lb_table, norm_mix_g, w_in, w_gla_gate_up, b_gla_gate, hgrn_norm_g,
           gla_norm_g, w_out, norm_ffn_g, w_ffn_in, w_ffn_out, norm_final_g):
    batch, seq, _ = x.shape
    assert w_in.shape[0] == 1 and lb_table.shape[0] == 2, "single-layer block"
    assert seq % ROWS_MIXER == 0
    x2d = x.reshape(batch * seq, D_MODEL)
    g_mix = norm_mix_g[0][None, :]
    w_up = w_gla_gate_up[0]
    b_up = b_gla_gate[0][None, :]

    w_main, w_ga, meta_kn, meta_v = _meta(meta_tokens, g_mix, w_in[0].T, w_up, b_up, lb_table)
    gates, o, w_out_b, w1_b, w2_b = _fused(
        x2d, g_mix, w_main, w_ga, w_up, b_up, lb_table, w_out[0], w_ffn_in[0], w_ffn_out[0],
        meta_kn, meta_v, hgrn_norm_g[0][None, :], gla_norm_g[0][None, :], seq, ROWS_MIXER)
    y = _ffn(x2d, o, gates, w_out_b, norm_ffn_g[0][None, :], w1_b, w2_b, norm_final_g[None, :],
             ROWS_FFN)
    return y.reshape(batch, seq, D_MODEL)
```

```python
import functools

import jax
import jax.numpy as jnp
from jax import lax
from jax.experimental import pallas as pl
from jax.experimental.pallas import tpu as pltpu

F32 = jnp.float32
BF16 = jnp.bfloat16

D_MODEL = 1024
N_META = 16
CHUNK = 128
HEADS = 4
HEAD_V = 128
HGRN_WIDTH = HEADS * HEAD_V
GLA_K = 64
GLA_QK_WIDTH = HEADS * GLA_K
GLA_WIDTH = HEADS * HEAD_V
MIX_WIDTH = HGRN_WIDTH + GLA_WIDTH
GATE_RANK = 16
GATE_NORMALIZER = 16.0
LOG2_E = 1.4426950408889634
FFN_HIDDEN = 2816
NORM_EPS = 1e-6
LANES = 128
N_PAIRS = HEADS

C_HQ, C_HF, C_HI, C_HG = 0, 512, 1024, 1536
C_GQ, C_GK, C_GV, C_GG = 2048, 2304, 2560, 3072
C_GA = 3584
IN_COLS = C_GA + GATE_RANK
W_BLOCK = 1792

KM_H, KM_G = 0, HGRN_WIDTH
KE_H, KE_G = HGRN_WIDTH + GLA_QK_WIDTH, 2 * HGRN_WIDTH + GLA_QK_WIDTH
KN_COLS = KE_G + GLA_WIDTH
DEC_COLS = HGRN_WIDTH + GLA_QK_WIDTH

HEAD_TABLE = tuple(
    [(KM_H + LANES * h, KE_H + LANES * h, LANES * h) for h in range(HEADS)]
    + [(KM_G + LANES * (h // 2), KE_G + LANES * h, HGRN_WIDTH + LANES * (h // 2))
       for h in range(HEADS)])

V7X_VMEM_LIMIT = 58 * 1024 * 1024

ROWS_MIXER = 512
ROWS_FFN = 1024
FFN_BLOCK = 256


def _rmsnorm(x, g):
    return x * lax.rsqrt(jnp.mean(x * x, axis=-1, keepdims=True) + NORM_EPS) * g


def _dot(a, b):
    return jnp.dot(a.astype(BF16), b.astype(BF16), preferred_element_type=F32)


def _dot_nt(a, b):
    return lax.dot_general(a.astype(BF16), b.astype(BF16), (((1,), (1,)), ((), ())),
                           preferred_element_type=F32)


def _dot_tn(a, b):
    return lax.dot_general(a.astype(BF16), b.astype(BF16), (((0,), (0,)), ((), ())),
                           preferred_element_type=F32)


def _sigmoid(x):
    return 0.5 * jnp.tanh(0.5 * x) + 0.5


def _silu(x):
    return x * _sigmoid(x)


def _log_sigmoid(x):
    return jnp.minimum(x, 0.0) - jnp.log1p(jnp.exp(-jnp.abs(x)))


def _iota2(n, axis):
    return lax.broadcasted_iota(jnp.int32, (n, n), axis)


def _cumsum_rows(g, tri2_bf16):
    g_hi = g.astype(BF16)
    g_lo = (g - g_hi.astype(F32)).astype(BF16)
    return jnp.dot(tri2_bf16, jnp.concatenate([g_hi, g_lo], axis=0),
                   preferred_element_type=F32)


def _own_half(x, h):
    lane = lax.broadcasted_iota(jnp.int32, x.shape, 1)
    return jnp.where((lane >= GLA_K) == bool(h % 2), x, 0.0)


def _project(x_ref, g_ref, w_ref, wga_ref, wup_ref, bup_ref, lbt_ref, kn_ref, qt_ref, vt_ref,
             gate_ref, dec_ref, chunk, transposed, between=None):
    n_chunks = x_ref.shape[0] // chunk
    u = _rmsnorm(x_ref[...], g_ref[...]).astype(BF16)
    tri = (_iota2(chunk, 0) >= _iota2(chunk, 1)).astype(BF16)
    tri = jnp.concatenate([tri, tri], axis=1)
    mid = chunk // 2 - 1

    def proj(col, width):
        blk, off = divmod(col, W_BLOCK)
        assert off + width <= W_BLOCK
        return jnp.dot(u, w_ref[blk, :, off:off + width], preferred_element_type=F32)

    def decay_factors(g, c):
        b = _cumsum_rows(g[c * chunk:(c + 1) * chunk], tri)
        b_mid = b[mid:mid + 1, :]
        b_last = b[chunk - 1:, :]
        return (jnp.exp2(b - b_mid), jnp.exp2(b_mid - b), jnp.exp2(b_last - b),
                jnp.exp2(b_mid), jnp.exp2(b_last))

    def put_tile(ref, c, i, tile):
        if transposed:
            ref[c, LANES * i:LANES * (i + 1), :] = tile.T
        else:
            ref[c, :, LANES * i:LANES * (i + 1)] = tile

    t0 = lbt_ref[0:1, :]
    t1 = lbt_ref[1:2, :]
    m = jnp.maximum(t0, t1)
    e0 = jnp.exp(t0 - m)
    lb = e0 / (e0 + jnp.exp(t1 - m))

    def value_piece(col, first_head):
        def emit():
            v = proj(col, 2 * LANES).astype(BF16)
            for c in range(n_chunks):
                for h in range(2):
                    put_tile(vt_ref, c, first_head + h,
                             v[c * chunk:(c + 1) * chunk, LANES * h:LANES * (h + 1)])
        return emit

    def gate_piece(col, out_col):
        def emit():
            gate_ref[:, out_col:out_col + 2 * LANES] = proj(col, 2 * LANES)
        return emit

    light = [value_piece(C_HI, 0), value_piece(C_HI + 2 * LANES, 2),
             value_piece(C_GV, HEADS), value_piece(C_GV + 2 * LANES, HEADS + 2),
             gate_piece(C_HG, 0), gate_piece(C_HG + 2 * LANES, 2 * LANES),
             gate_piece(C_GG, HGRN_WIDTH), gate_piece(C_GG + 2 * LANES, HGRN_WIDTH + 2 * LANES)]

    def pause():
        if between is not None:
            between()

    def next_light():
        if light:
            light.pop(0)()
        pause()

    def ga_part(k):
        cols = slice(k * (D_MODEL // 4), (k + 1) * (D_MODEL // 4))
        return _dot_nt(wga_ref[:, cols], u[:, cols])

    ga_t = ga_part(0)
    hf = proj(C_HF, HGRN_WIDTH)
    ga_t += ga_part(1)
    hq = proj(C_HQ, HGRN_WIDTH)

    f = 0.5 * (1.0 + lb) + (0.5 * (1.0 - lb)) * jnp.tanh(0.5 * hf)
    g = jnp.log2(f)
    k = 1.0 - f
    q = _silu(hq)
    pause()
    ga_t += ga_part(2)
    gq = proj(C_GQ, GLA_QK_WIDTH)
    ga_t += ga_part(3)
    gk = proj(C_GK, GLA_QK_WIDTH)
    pause()
    for c in range(n_chunks):
        rows = slice(c * chunk, (c + 1) * chunk)
        e_q, e_k, e_end, e_mid, e_last = decay_factors(g, c)
        kn_ref[rows, KM_H:KM_H + HGRN_WIDTH] = (k[rows] * e_k).astype(BF16)
        kn_ref[rows, KE_H:KE_H + HGRN_WIDTH] = (k[rows] * e_end).astype(BF16)
        qm = (q[rows] * e_q).astype(BF16)
        for h in range(HEADS):
            put_tile(qt_ref, c, h, qm[:, LANES * h:LANES * (h + 1)])
        dec_ref[c, :, 0:HGRN_WIDTH] = e_mid
        dec_ref[c, :, DEC_COLS:DEC_COLS + HGRN_WIDTH] = e_last
        next_light()

    next_light()
    logits = _dot_tn(ga_t, wup_ref[...]) + bup_ref[...]
    g = _log_sigmoid(logits) * (LOG2_E / GATE_NORMALIZER)
    q = gq * (GLA_K ** -0.5)
    k = gk
    for c in range(n_chunks):
        rows = slice(c * chunk, (c + 1) * chunk)
        e_q, e_k, e_end, e_mid, e_last = decay_factors(g, c)
        qm = q[rows] * e_q
        ke = k[rows] * e_end
        kn_ref[rows, KM_G:KM_G + GLA_QK_WIDTH] = (k[rows] * e_k).astype(BF16)
        for h in range(HEADS):
            pair = slice(LANES * (h // 2), LANES * (h // 2 + 1))
            put_tile(qt_ref, c, HEADS + h, _own_half(qm[:, pair], h).astype(BF16))
            kn_ref[rows, KE_G + LANES * h:KE_G + LANES * (h + 1)] = (
                _own_half(ke[:, pair], h).astype(BF16))
        dec_ref[c, :, HGRN_WIDTH:DEC_COLS] = e_mid
        dec_ref[c, :, DEC_COLS + HGRN_WIDTH:2 * DEC_COLS] = e_last
        next_light()
    while light:
        next_light()


def _meta_kernel(x_ref, g_ref, wt_ref, wgat_ref, wup_ref, bup_ref, lbt_ref,
                 w_ref, wga_ref, kn_ref, v_ref, w_scr, q_ref, gate_ref, dec_ref):
    j = pl.program_id(0)
    n_blocks = w_scr.shape[0]

    @pl.when(j < n_blocks)
    def _():
        blk = wt_ref[...].T.astype(BF16)
        w_ref[0] = blk
        w_scr[j] = blk

    @pl.when(j == n_blocks)
    def _():
        wga_ref[...] = wgat_ref[...].astype(BF16)
        _project(x_ref, g_ref, w_scr, wga_ref, wup_ref, bup_ref, lbt_ref, kn_ref, q_ref, v_ref,
                 gate_ref, dec_ref, N_META, transposed=False)


def _meta(meta_tokens, g, w_in_t, w_up, b_up, lb_table):
    n_blocks = C_GA // W_BLOCK
    full = lambda a: pl.BlockSpec(a.shape, lambda j: (0,) * len(a.shape))
    last = n_blocks - 1
    out_shapes = [jax.ShapeDtypeStruct((n_blocks, D_MODEL, W_BLOCK), BF16),
                  jax.ShapeDtypeStruct((GATE_RANK, D_MODEL), BF16),
                  jax.ShapeDtypeStruct((N_META, KN_COLS), BF16),
                  jax.ShapeDtypeStruct((1, N_META, MIX_WIDTH), BF16)]
    return pl.pallas_call(
        _meta_kernel,
        grid=(n_blocks + 1,),
        in_specs=[
            full(meta_tokens), full(g),
            pl.BlockSpec((W_BLOCK, D_MODEL), lambda j: (jnp.minimum(j, last), 0)),
            pl.BlockSpec((GATE_RANK, D_MODEL), lambda j: (C_GA // GATE_RANK, 0)),
            full(w_up), full(b_up), full(lb_table),
        ],
        out_specs=[pl.BlockSpec((1, D_MODEL, W_BLOCK), lambda j: (jnp.minimum(j, last), 0, 0))]
        + [full(s) for s in out_shapes[1:]],
        out_shape=out_shapes,
        scratch_shapes=[pltpu.VMEM((n_blocks, D_MODEL, W_BLOCK), BF16),
                        pltpu.VMEM((1, N_META, MIX_WIDTH), BF16),
                        pltpu.VMEM((N_META, MIX_WIDTH), F32),
                        pltpu.VMEM((1, 1, 2 * DEC_COLS), F32)],
        compiler_params=pltpu.CompilerParams(
            dimension_semantics=("arbitrary",), vmem_limit_bytes=V7X_VMEM_LIMIT),
        name="meta",
    )(meta_tokens, g, w_in_t, w_in_t, w_up, b_up, lb_table)


def _side_by_side(a, b):
    return jnp.concatenate([a, b], axis=1)


def _block_diag(a, b):
    return jnp.concatenate([_side_by_side(a, jnp.zeros_like(b)),
                            _side_by_side(jnp.zeros_like(a), b)], axis=0)


def _seed_state(mkn_ref, mv_ref, st_ref):
    seed = [_dot_tn(mv_ref[0, :, LANES * i:LANES * (i + 1)], mkn_ref[:, ke_col:ke_col + LANES])
            for i, (_, ke_col, _) in enumerate(HEAD_TABLE)]
    for p in range(N_PAIRS):
        st_ref[p] = _side_by_side(seed[2 * p], seed[2 * p + 1])


def _recurrence_steps(kn_ref, qt_ref, vt_ref, dec_ref, hnorm_ref, gnorm_ref, o_ref, st_ref):
    key_before_query = _iota2(CHUNK, 0) <= _iota2(CHUNK, 1)
    key_before_query = _side_by_side(key_before_query, key_before_query)
    norms = (hnorm_ref[...],) * HEADS + (gnorm_ref[...],) * HEADS

    def pair(fn, p):
        return fn(2 * p), fn(2 * p + 1)

    for c in range(kn_ref.shape[0] // CHUNK):
        rows = slice(c * CHUNK, (c + 1) * CHUNK)
        dec = dec_ref[c]
        km = lambda i: kn_ref[rows, HEAD_TABLE[i][0]:HEAD_TABLE[i][0] + LANES]
        ke = lambda i: kn_ref[rows, HEAD_TABLE[i][1]:HEAD_TABLE[i][1] + LANES]
        e_mid = lambda i: dec[:, HEAD_TABLE[i][2]:HEAD_TABLE[i][2] + LANES]
        e_last = lambda i: dec[:, DEC_COLS + HEAD_TABLE[i][2]:DEC_COLS + HEAD_TABLE[i][2] + LANES]
        qm_t = lambda i: qt_ref[c, LANES * i:LANES * (i + 1), :]
        v_t = lambda i: vt_ref[c, LANES * i:LANES * (i + 1), :]
        first, outs = [], []
        for p in range(N_PAIRS):
            s_mid = st_ref[p] * _side_by_side(*pair(e_mid, p))
            lhs = jnp.concatenate([_side_by_side(*pair(km, p)), s_mid.astype(BF16)], axis=0)
            first.append(jnp.dot(lhs, _block_diag(*pair(qm_t, p)),
                                 preferred_element_type=F32))
        yield
        for p in range(N_PAIRS):
            scores_t = jnp.where(key_before_query, first[p][:CHUNK], 0.0).astype(BF16)
            outs.append(first[p][CHUNK:]
                        + jnp.dot(_side_by_side(*pair(v_t, p)),
                                  _block_diag(scores_t[:, :CHUNK], scores_t[:, CHUNK:]),
                                  preferred_element_type=F32))
        yield
        for p in range(N_PAIRS):
            st_ref[p] = (st_ref[p] * _side_by_side(*pair(e_last, p))
                         + jnp.dot(_side_by_side(*pair(v_t, p)), _block_diag(*pair(ke, p)),
                                   preferred_element_type=F32))
        yield
        for p in range(N_PAIRS):
            o_t = outs[p]
            o_t = o_t * lax.rsqrt(jnp.mean(o_t * o_t, axis=0, keepdims=True) + NORM_EPS)
            o = o_t.T
            for half, i in enumerate((2 * p, 2 * p + 1)):
                o_ref[rows, LANES * i:LANES * (i + 1)] = (
                    o[half * CHUNK:(half + 1) * CHUNK] * norms[i]).astype(o_ref.dtype)
        yield


STAGES_PER_PAUSE = (1, 1, 2, 1, 2, 1, 1, 2, 1, 2, 2)


def _fused_kernel(x_ref, g_ref, w_ref, wga_ref, wup_ref, bup_ref, lbt_ref, wout32_ref, w1_32_ref,
                  w2_32_ref, mkn_ref, mv_ref, hnorm_ref, gnorm_ref,
                  gate_ref, o_ref, wout_ref, w1_ref, w2_ref,
                  kn_scr, qt_scr, vt_scr, dec_scr, st_ref, *, n_tiles, steps_per_seq):
    s = pl.program_id(0)

    @pl.when(lax.rem(s + steps_per_seq - 1, steps_per_seq) == 0)
    def _():
        _seed_state(mkn_ref, mv_ref, st_ref)

    def project(fill, between=None):
        _project(x_ref, g_ref, w_ref, wga_ref, wup_ref, bup_ref, lbt_ref, kn_scr.at[fill],
                 qt_scr.at[fill], vt_scr.at[fill], gate_ref, dec_scr.at[fill], CHUNK,
                 transposed=True, between=between)
        wout_ref[...] = wout32_ref[...].astype(BF16)
        w1_ref[...] = w1_32_ref[...].astype(BF16)
        w2_ref[...] = w2_32_ref[...].astype(BF16)

    def recurrence(drain):
        return _recurrence_steps(kn_scr.at[drain], qt_scr.at[drain], vt_scr.at[drain],
                                 dec_scr.at[drain], hnorm_ref, gnorm_ref, o_ref, st_ref)

    @pl.when(s == 0)
    def _():
        project(0)

    for parity in (0, 1):
        @pl.when((s > 0) & (s < n_tiles) & (lax.rem(s, 2) == parity))
        def _():
            stages = recurrence(1 - parity)
            budget = list(STAGES_PER_PAUSE)

            def between():
                for _ in range(budget.pop(0) if budget else 1):
                    next(stages, None)

            project(parity, between)
            for _ in stages:
                pass

    @pl.when(s == n_tiles)
    def _():
        for _ in recurrence((n_tiles - 1) % 2):
            pass


def _fused(x2d, g, w_main, w_ga, w_up, b_up, lb_table, w_out, w1, w2, meta_kn, meta_v, hnorm,
           gnorm, seq, rows):
    n = x2d.shape[0]
    n_tiles = n // rows
    chunks = rows // CHUNK
    last = n_tiles - 1
    wout_rows = w_out.shape[0] // n_tiles
    w1_rows = w1.shape[0] // n_tiles
    w2_rows = 2 * w2.shape[0] // n_tiles
    assert wout_rows * n_tiles == w_out.shape[0] and w1_rows * n_tiles == w1.shape[0]
    assert w2_rows * n_tiles == 2 * w2.shape[0]
    const = lambda s: (0, 0)
    const3 = lambda s: (0, 0, 0)
    cur = lambda s: (jnp.minimum(s, last), 0)
    prev = lambda s: (jnp.maximum(s - 1, 0), 0)
    half_pace = lambda s: (jnp.minimum(s, last) // 2, 0)
    w_specs = [pl.BlockSpec((wout_rows, w_out.shape[1]), cur),
               pl.BlockSpec((w1_rows, w1.shape[1]), cur),
               pl.BlockSpec((w2_rows, w2.shape[1]), half_pace)]
    return pl.pallas_call(
        functools.partial(_fused_kernel, n_tiles=n_tiles, steps_per_seq=seq // rows),
        grid=(n_tiles + 1,),
        in_specs=[
            pl.BlockSpec((rows, D_MODEL), cur),
            pl.BlockSpec((1, D_MODEL), const),
            pl.BlockSpec(w_main.shape, const3, pipeline_mode=pl.Buffered(1)),
            pl.BlockSpec(w_ga.shape, const),
            pl.BlockSpec((GATE_RANK, GLA_QK_WIDTH), const),
            pl.BlockSpec((1, GLA_QK_WIDTH), const),
            pl.BlockSpec(lb_table.shape, const),
        ] + w_specs + [
            pl.BlockSpec(meta_kn.shape, const),
            pl.BlockSpec(meta_v.shape, const3),
            pl.BlockSpec((1, HEAD_V), const),
            pl.BlockSpec((1, HEAD_V), const),
        ],
        out_specs=[
            pl.BlockSpec((rows, MIX_WIDTH), cur),
            pl.BlockSpec((rows, MIX_WIDTH), prev),
        ] + w_specs,
        out_shape=[
            jax.ShapeDtypeStruct((n, MIX_WIDTH), F32),
            jax.ShapeDtypeStruct((n, MIX_WIDTH), BF16),
            jax.ShapeDtypeStruct(w_out.shape, BF16),
            jax.ShapeDtypeStruct(w1.shape, BF16),
            jax.ShapeDtypeStruct(w2.shape, BF16),
        ],
        scratch_shapes=[pltpu.VMEM((2, rows, KN_COLS), BF16),
                        pltpu.VMEM((2, chunks, MIX_WIDTH, CHUNK), BF16),
                        pltpu.VMEM((2, chunks, MIX_WIDTH, CHUNK), BF16),
                        pltpu.VMEM((2, chunks, 1, 2 * DEC_COLS), F32),
                        pltpu.VMEM((N_PAIRS, HEAD_V, 2 * LANES), F32)],
        compiler_params=pltpu.CompilerParams(
            dimension_semantics=("arbitrary",), vmem_limit_bytes=V7X_VMEM_LIMIT),
        name="mixer",
    )(x2d, g, w_main, w_ga, w_up, b_up, lb_table, w_out, w1, w2, meta_kn, meta_v, hnorm, gnorm)


def _ffn_kernel(x_ref, o_ref, gate_ref, wout_ref, g2_ref, w1_ref, w2_ref, g3_ref, y_ref):
    o = o_ref[...].astype(F32) * _silu(gate_ref[...])
    h = x_ref[...] + _dot(o, wout_ref[...])
    u = _rmsnorm(h, g2_ref[...]).astype(BF16)
    acts = []
    for j in range(0, FFN_HIDDEN, FFN_BLOCK):
        gate = jnp.dot(u, w1_ref[:, j:j + FFN_BLOCK], preferred_element_type=F32)
        up = jnp.dot(u, w1_ref[:, FFN_HIDDEN + j:FFN_HIDDEN + j + FFN_BLOCK],
                     preferred_element_type=F32)
        acts.append((_silu(gate) * up).astype(BF16))
    ffn = jnp.dot(jnp.concatenate(acts, axis=1), w2_ref[...], preferred_element_type=F32)
    y_ref[...] = _rmsnorm(h + ffn, g3_ref[...])


def _ffn(x2d, o, gates, w_out, g2, w1, w2, g3, rows):
    n = x2d.shape[0]
    const = lambda i: (0, 0)
    tile = lambda i: (i, 0)
    resident = functools.partial(pl.BlockSpec, index_map=const, pipeline_mode=pl.Buffered(1))
    return pl.pallas_call(
        _ffn_kernel,
        grid=(n // rows,),
        in_specs=[
            pl.BlockSpec((rows, D_MODEL), tile),
            pl.BlockSpec((rows, MIX_WIDTH), tile),
            pl.BlockSpec((rows, MIX_WIDTH), tile),
            resident((MIX_WIDTH, D_MODEL)),
            pl.BlockSpec((1, D_MODEL), const),
            resident((D_MODEL, 2 * FFN_HIDDEN)),
            resident((FFN_HIDDEN, D_MODEL)),
            pl.BlockSpec((1, D_MODEL), const),
        ],
        out_specs=pl.BlockSpec((rows, D_MODEL), tile),
        out_shape=jax.ShapeDtypeStruct((n, D_MODEL), F32),
        compiler_params=pltpu.CompilerParams(
            dimension_semantics=("arbitrary",), vmem_limit_bytes=V7X_VMEM_LIMIT),
        name="outproj_ffn",
    )(x2d, o, gates, w_out, g2, w1, w2, g3)


def kernel(x, meta_tokens, lb_table, norm_mix_g, w_in, w_gla_gate_up, b_gla_gate, hgrn_norm_g,
           gla_norm_g, w_out, norm_ffn_g, w_ffn_in, w_ffn_out, norm_final_g):
    batch, seq, _ = x.shape
    assert w_in.shape[0] == 1 and lb_table.shape[0] == 2, "single-layer block"
    assert seq % ROWS_MIXER == 0
    x2d = x.reshape(batch * seq, D_MODEL)
    g_mix = norm_mix_g[0][None, :]
    w_up = w_gla_gate_up[0]
    b_up = b_gla_gate[0][None, :]

    w_main, w_ga, meta_kn, meta_v = _meta(meta_tokens, g_mix, w_in[0].T, w_up, b_up, lb_table)
    gates, o, w_out_b, w1_b, w2_b = _fused(
        x2d, g_mix, w_main, w_ga, w_up, b_up, lb_table, w_out[0], w_ffn_in[0], w_ffn_out[0],
        meta_kn, meta_v, hgrn_norm_g[0][None, :], gla_norm_g[0][None, :], seq, ROWS_MIXER)
    y = _ffn(x2d, o, gates, w_out_b, norm_ffn_g[0][None, :], w1_b, w2_b, norm_final_g[None, :],
             ROWS_FFN)
    return y.reshape(batch, seq, D_MODEL)
```

```python
import functools

import jax
import jax.numpy as jnp
from jax import lax
from jax.experimental import pallas as pl
from jax.experimental.pallas import tpu as pltpu

F32 = jnp.float32
BF16 = jnp.bfloat16

D_MODEL = 1024
N_META = 16
CHUNK = 128
HEADS = 4
HEAD_V = 128
HGRN_WIDTH = HEADS * HEAD_V
GLA_K = 64
GLA_QK_WIDTH = HEADS * GLA_K
GLA_WIDTH = HEADS * HEAD_V
MIX_WIDTH = HGRN_WIDTH + GLA_WIDTH
GATE_RANK = 16
GATE_NORMALIZER = 16.0
LOG2_E = 1.4426950408889634
FFN_HIDDEN = 2816
NORM_EPS = 1e-6
LANES = 128
N_PAIRS = HEADS

C_HQ, C_HF, C_HI, C_HG = 0, 512, 1024, 1536
C_GQ, C_GK, C_GV, C_GG = 2048, 2304, 2560, 3072
C_GA = 3584
IN_COLS = C_GA + GATE_RANK
W_BLOCK = 1792

KM_H, KM_G = 0, HGRN_WIDTH
KE_H, KE_G = HGRN_WIDTH + GLA_QK_WIDTH, 2 * HGRN_WIDTH + GLA_QK_WIDTH
KN_COLS = KE_G + GLA_WIDTH
DEC_COLS = HGRN_WIDTH + GLA_QK_WIDTH

HEAD_TABLE = tuple(
    [(KM_H + LANES * h, KE_H + LANES * h, LANES * h) for h in range(HEADS)]
    + [(KM_G + LANES * (h // 2), KE_G + LANES * h, HGRN_WIDTH + LANES * (h // 2))
       for h in range(HEADS)])

V7X_VMEM_LIMIT = 58 * 1024 * 1024

ROWS_MIXER = 512
ROWS_FFN = 1024
FFN_BLOCK = 256


def _rmsnorm(x, g):
    return x * lax.rsqrt(jnp.mean(x * x, axis=-1, keepdims=True) + NORM_EPS) * g


def _dot(a, b):
    return jnp.dot(a.astype(BF16), b.astype(BF16), preferred_element_type=F32)


def _dot_nt(a, b):
    return lax.dot_general(a.astype(BF16), b.astype(BF16), (((1,), (1,)), ((), ())),
                           preferred_element_type=F32)


def _dot_tn(a, b):
    return lax.dot_general(a.astype(BF16), b.astype(BF16), (((0,), (0,)), ((), ())),
                           preferred_element_type=F32)


def _sigmoid(x):
    return 0.5 * jnp.tanh(0.5 * x) + 0.5


def _silu(x):
    return x * _sigmoid(x)


def _log_sigmoid(x):
    return jnp.minimum(x, 0.0) - jnp.log1p(jnp.exp(-jnp.abs(x)))


def _iota2(n, axis):
    return lax.broadcasted_iota(jnp.int32, (n, n), axis)


def _cumsum_rows(g, tri2_bf16):
    g_hi = g.astype(BF16)
    g_lo = (g - g_hi.astype(F32)).astype(BF16)
    return jnp.dot(tri2_bf16, jnp.concatenate([g_hi, g_lo], axis=0),
                   preferred_element_type=F32)


def _own_half(x, h):
    lane = lax.broadcasted_iota(jnp.int32, x.shape, 1)
    return jnp.where((lane >= GLA_K) == bool(h % 2), x, 0.0)


def _project(x_ref, g_ref, w_ref, wga_ref, wup_ref, bup_ref, lbt_ref, kn_ref, qt_ref, vt_ref,
             gate_ref, dec_ref, chunk, transposed, between=None):
    n_chunks = x_ref.shape[0] // chunk
    u = _rmsnorm(x_ref[...], g_ref[...]).astype(BF16)
    tri = (_iota2(chunk, 0) >= _iota2(chunk, 1)).astype(BF16)
    tri = jnp.concatenate([tri, tri], axis=1)
    mid = chunk // 2 - 1

    def proj(col, width):
        blk, off = divmod(col, W_BLOCK)
        assert off + width <= W_BLOCK
        return jnp.dot(u, w_ref[blk, :, off:off + width], preferred_element_type=F32)

    def decay_factors(g, c):
        b = _cumsum_rows(g[c * chunk:(c + 1) * chunk], tri)
        b_mid = b[mid:mid + 1, :]
        b_last = b[chunk - 1:, :]
        return (jnp.exp2(b - b_mid), jnp.exp2(b_mid - b), jnp.exp2(b_last - b),
                jnp.exp2(b_mid), jnp.exp2(b_last))

    def put_tile(ref, c, i, tile):
        if transposed:
            ref[c, LANES * i:LANES * (i + 1), :] = tile.T
        else:
            ref[c, :, LANES * i:LANES * (i + 1)] = tile

    t0 = lbt_ref[0:1, :]
    t1 = lbt_ref[1:2, :]
    m = jnp.maximum(t0, t1)
    e0 = jnp.exp(t0 - m)
    lb = e0 / (e0 + jnp.exp(t1 - m))

    def value_piece(col, first_head):
        def emit():
            v = proj(col, 2 * LANES).astype(BF16)
            for c in range(n_chunks):
                for h in range(2):
                    put_tile(vt_ref, c, first_head + h,
                             v[c * chunk:(c + 1) * chunk, LANES * h:LANES * (h + 1)])
        return emit

    def gate_piece(col, out_col):
        def emit():
            gate_ref[:, out_col:out_col + 2 * LANES] = proj(col, 2 * LANES)
        return emit

    light = [value_piece(C_HI, 0), value_piece(C_HI + 2 * LANES, 2),
             value_piece(C_GV, HEADS), value_piece(C_GV + 2 * LANES, HEADS + 2),
             gate_piece(C_HG, 0), gate_piece(C_HG + 2 * LANES, 2 * LANES),
             gate_piece(C_GG, HGRN_WIDTH), gate_piece(C_GG + 2 * LANES, HGRN_WIDTH + 2 * LANES)]

    def pause():
        if between is not None:
            between()

    def next_light():
        if light:
            light.pop(0)()
        pause()

    def ga_part(k):
        cols = slice(k * (D_MODEL // 4), (k + 1) * (D_MODEL // 4))
        return _dot_nt(wga_ref[:, cols], u[:, cols])

    ga_t = ga_part(0)
    hf = proj(C_HF, HGRN_WIDTH)
    ga_t += ga_part(1)
    hq = proj(C_HQ, HGRN_WIDTH)

    f = 0.5 * (1.0 + lb) + (0.5 * (1.0 - lb)) * jnp.tanh(0.5 * hf)
    g = jnp.log2(f)
    k = 1.0 - f
    q = _silu(hq)
    pause()
    ga_t += ga_part(2)
    gq = proj(C_GQ, GLA_QK_WIDTH)
    ga_t += ga_part(3)
    gk = proj(C_GK, GLA_QK_WIDTH)
    pause()
    for c in range(n_chunks):
        rows = slice(c * chunk, (c + 1) * chunk)
        e_q, e_k, e_end, e_mid, e_last = decay_factors(g, c)
        kn_ref[rows, KM_H:KM_H + HGRN_WIDTH] = (k[rows] * e_k).astype(BF16)
        kn_ref[rows, KE_H:KE_H + HGRN_WIDTH] = (k[rows] * e_end).astype(BF16)
        qm = (q[rows] * e_q).astype(BF16)
        for h in range(HEADS):
            put_tile(qt_ref, c, h, qm[:, LANES * h:LANES * (h + 1)])
        dec_ref[c, :, 0:HGRN_WIDTH] = e_mid
        dec_ref[c, :, DEC_COLS:DEC_COLS + HGRN_WIDTH] = e_last
        next_light()

    next_light()
    logits = _dot_tn(ga_t, wup_ref[...]) + bup_ref[...]
    g = _log_sigmoid(logits) * (LOG2_E / GATE_NORMALIZER)
    q = gq * (GLA_K ** -0.5)
    k = gk
    for c in range(n_chunks):
        rows = slice(c * chunk, (c + 1) * chunk)
        e_q, e_k, e_end, e_mid, e_last = decay_factors(g, c)
        qm = q[rows] * e_q
        ke = k[rows] * e_end
        kn_ref[rows, KM_G:KM_G + GLA_QK_WIDTH] = (k[rows] * e_k).astype(BF16)
        for h in range(HEADS):
            pair = slice(LANES * (h // 2), LANES * (h // 2 + 1))
            put_tile(qt_ref, c, HEADS + h, _own_half(qm[:, pair], h).astype(BF16))
            kn_ref[rows, KE_G + LANES * h:KE_G + LANES * (h + 1)] = (
                _own_half(ke[:, pair], h).astype(BF16))
        dec_ref[c, :, HGRN_WIDTH:DEC_COLS] = e_mid
        dec_ref[c, :, DEC_COLS + HGRN_WIDTH:2 * DEC_COLS] = e_last
        next_light()
    while light:
        next_light()


def _meta_kernel(x_ref, g_ref, wt_ref, wgat_ref, wup_ref, bup_ref, lbt_ref,
                 w_ref, wga_ref, kn_ref, v_ref, w_scr, q_ref, gate_ref, dec_ref):
    j = pl.program_id(0)
    n_blocks = w_scr.shape[0]

    @pl.when(j < n_blocks)
    def _():
        blk = wt_ref[...].T.astype(BF16)
        w_ref[0] = blk
        w_scr[j] = blk

    @pl.when(j == n_blocks)
    def _():
        wga_ref[...] = wgat_ref[...].astype(BF16)
        _project(x_ref, g_ref, w_scr, wga_ref, wup_ref, bup_ref, lbt_ref, kn_ref, q_ref, v_ref,
                 gate_ref, dec_ref, N_META, transposed=False)


def _meta(meta_tokens, g, w_in_t, w_up, b_up, lb_table):
    n_blocks = C_GA // W_BLOCK
    full = lambda a: pl.BlockSpec(a.shape, lambda j: (0,) * len(a.shape))
    last = n_blocks - 1
    out_shapes = [jax.ShapeDtypeStruct((n_blocks, D_MODEL, W_BLOCK), BF16),
                  jax.ShapeDtypeStruct((GATE_RANK, D_MODEL), BF16),
                  jax.ShapeDtypeStruct((N_META, KN_COLS), BF16),
                  jax.ShapeDtypeStruct((1, N_META, MIX_WIDTH), BF16)]
    return pl.pallas_call(
        _meta_kernel,
        grid=(n_blocks + 1,),
        in_specs=[
            full(meta_tokens), full(g),
            pl.BlockSpec((W_BLOCK, D_MODEL), lambda j: (jnp.minimum(j, last), 0)),
            pl.BlockSpec((GATE_RANK, D_MODEL), lambda j: (C_GA // GATE_RANK, 0)),
            full(w_up), full(b_up), full(lb_table),
        ],
        out_specs=[pl.BlockSpec((1, D_MODEL, W_BLOCK), lambda j: (jnp.minimum(j, last), 0, 0))]
        + [full(s) for s in out_shapes[1:]],
        out_shape=out_shapes,
        scratch_shapes=[pltpu.VMEM((n_blocks, D_MODEL, W_BLOCK), BF16),
                        pltpu.VMEM((1, N_META, MIX_WIDTH), BF16),
                        pltpu.VMEM((N_META, MIX_WIDTH), F32),
                        pltpu.VMEM((1, 1, 2 * DEC_COLS), F32)],
        compiler_params=pltpu.CompilerParams(
            dimension_semantics=("arbitrary",), vmem_limit_bytes=V7X_VMEM_LIMIT),
        name="meta",
    )(meta_tokens, g, w_in_t, w_in_t, w_up, b_up, lb_table)


def _side_by_side(a, b):
    return jnp.concatenate([a, b], axis=1)


def _block_diag(a, b):
    return jnp.concatenate([_side_by_side(a, jnp.zeros_like(b)),
                            _side_by_side(jnp.zeros_like(a), b)], axis=0)


def _seed_state(mkn_ref, mv_ref, st_ref):
    seed = [_dot_tn(mv_ref[0, :, LANES * i:LANES * (i + 1)], mkn_ref[:, ke_col:ke_col + LANES])
            for i, (_, ke_col, _) in enumerate(HEAD_TABLE)]
    for p in range(N_PAIRS):
        st_ref[p] = _side_by_side(seed[2 * p], seed[2 * p + 1])


def _recurrence_steps(kn_ref, qt_ref, vt_ref, dec_ref, hnorm_ref, gnorm_ref, o_ref, st_ref):
    key_before_query = _iota2(CHUNK, 0) <= _iota2(CHUNK, 1)
    key_before_query = _side_by_side(key_before_query, key_before_query)
    norms = (hnorm_ref[...],) * HEADS + (gnorm_ref[...],) * HEADS

    def pair(fn, p):
        return fn(2 * p), fn(2 * p + 1)

    for c in range(kn_ref.shape[0] // CHUNK):
        rows = slice(c * CHUNK, (c + 1) * CHUNK)
        dec = dec_ref[c]
        km = lambda i: kn_ref[rows, HEAD_TABLE[i][0]:HEAD_TABLE[i][0] + LANES]
        ke = lambda i: kn_ref[rows, HEAD_TABLE[i][1]:HEAD_TABLE[i][1] + LANES]
        e_mid = lambda i: dec[:, HEAD_TABLE[i][2]:HEAD_TABLE[i][2] + LANES]
        e_last = lambda i: dec[:, DEC_COLS + HEAD_TABLE[i][2]:DEC_COLS + HEAD_TABLE[i][2] + LANES]
        qm_t = lambda i: qt_ref[c, LANES * i:LANES * (i + 1), :]
        v_t = lambda i: vt_ref[c, LANES * i:LANES * (i + 1), :]
        first, outs = [], []
        for p in range(N_PAIRS):
            s_mid = st_ref[p] * _side_by_side(*pair(e_mid, p))
            lhs = jnp.concatenate([_side_by_side(*pair(km, p)), s_mid.astype(BF16)], axis=0)
            first.append(jnp.dot(lhs, _block_diag(*pair(qm_t, p)),
                                 preferred_element_type=F32))
        yield
        for p in range(N_PAIRS):
            scores_t = jnp.where(key_before_query, first[p][:CHUNK], 0.0).astype(BF16)
            outs.append(first[p][CHUNK:]
                        + jnp.dot(_side_by_side(*pair(v_t, p)),
                                  _block_diag(scores_t[:, :CHUNK], scores_t[:, CHUNK:]),
                                  preferred_element_type=F32))
        yield
        for p in range(N_PAIRS):
            st_ref[p] = (st_ref[p] * _side_by_side(*pair(e_last, p))
                         + jnp.dot(_side_by_side(*pair(v_t, p)), _block_diag(*pair(ke, p)),
                                   preferred_element_type=F32))
        yield
        for p in range(N_PAIRS):
            o_t = outs[p]
            o_t = o_t * lax.rsqrt(jnp.mean(o_t * o_t, axis=0, keepdims=True) + NORM_EPS)
            o = o_t.T
            for half, i in enumerate((2 * p, 2 * p + 1)):
                o_ref[rows, LANES * i:LANES * (i + 1)] = (
                    o[half * CHUNK:(half + 1) * CHUNK] * norms[i]).astype(o_ref.dtype)
        yield


STAGES_PER_PAUSE = (1, 1, 2, 1, 2, 1, 1, 2, 1, 2, 2)


def _fused_kernel(x_ref, g_ref, w_ref, wga_ref, wup_ref, bup_ref, lbt_ref, wout32_ref, w1_32_ref,
                  w2_32_ref, mkn_ref, mv_ref, hnorm_ref, gnorm_ref,
                  gate_ref, o_ref, wout_ref, w1_ref, w2_ref,
                  kn_scr, qt_scr, vt_scr, dec_scr, st_ref, *, n_tiles, steps_per_seq):
    s = pl.program_id(0)

    @pl.when(lax.rem(s + steps_per_seq - 1, steps_per_seq) == 0)
    def _():
        _seed_state(mkn_ref, mv_ref, st_ref)

    def project(fill, between=None):
        _project(x_ref, g_ref, w_ref, wga_ref, wup_ref, bup_ref, lbt_ref, kn_scr.at[fill],
                 qt_scr.at[fill], vt_scr.at[fill], gate_ref, dec_scr.at[fill], CHUNK,
                 transposed=True, between=between)
        wout_ref[...] = wout32_ref[...].astype(BF16)
        w1_ref[...] = w1_32_ref[...].astype(BF16)
        w2_ref[...] = w2_32_ref[...].astype(BF16)

    def recurrence(drain):
        return _recurrence_steps(kn_scr.at[drain], qt_scr.at[drain], vt_scr.at[drain],
                                 dec_scr.at[drain], hnorm_ref, gnorm_ref, o_ref, st_ref)

    @pl.when(s == 0)
    def _():
        project(0)

    for parity in (0, 1):
        @pl.when((s > 0) & (s < n_tiles) & (lax.rem(s, 2) == parity))
        def _():
            stages = recurrence(1 - parity)
            budget = list(STAGES_PER_PAUSE)

            def between():
                for _ in range(budget.pop(0) if budget else 1):
                    next(stages, None)

            project(parity, between)
            for _ in stages:
                pass

    @pl.when(s == n_tiles)
    def _():
        for _ in recurrence((n_tiles - 1) % 2):
            pass


def _fused(x2d, g, w_main, w_ga, w_up, b_up, lb_table, w_out, w1, w2, meta_kn, meta_v, hnorm,
           gnorm, seq, rows):
    n = x2d.shape[0]
    n_tiles = n // rows
    chunks = rows // CHUNK
    last = n_tiles - 1
    wout_rows = w_out.shape[0] // n_tiles
    w1_rows = w1.shape[0] // n_tiles
    w2_rows = 2 * w2.shape[0] // n_tiles
    assert wout_rows * n_tiles == w_out.shape[0] and w1_rows * n_tiles == w1.shape[0]
    assert w2_rows * n_tiles == 2 * w2.shape[0]
    const = lambda s: (0, 0)
    const3 = lambda s: (0, 0, 0)
    cur = lambda s: (jnp.minimum(s, last), 0)
    prev = lambda s: (jnp.maximum(s - 1, 0), 0)
    half_pace = lambda s: (jnp.minimum(s, last) // 2, 0)
    w_specs = [pl.BlockSpec((wout_rows, w_out.shape[1]), cur),
               pl.BlockSpec((w1_rows, w1.shape[1]), cur),
               pl.BlockSpec((w2_rows, w2.shape[1]), half_pace)]
    return pl.pallas_call(
        functools.partial(_fused_kernel, n_tiles=n_tiles, steps_per_seq=seq // rows),
        grid=(n_tiles + 1,),
        in_specs=[
            pl.BlockSpec((rows, D_MODEL), cur),
            pl.BlockSpec((1, D_MODEL), const),
            pl.BlockSpec(w_main.shape, const3, pipeline_mode=pl.Buffered(1)),
            pl.BlockSpec(w_ga.shape, const),
            pl.BlockSpec((GATE_RANK, GLA_QK_WIDTH), const),
            pl.BlockSpec((1, GLA_QK_WIDTH), const),
            pl.BlockSpec(lb_table.shape, const),
        ] + w_specs + [
            pl.BlockSpec(meta_kn.shape, const),
            pl.BlockSpec(meta_v.shape, const3),
            pl.BlockSpec((1, HEAD_V), const),
            pl.BlockSpec((1, HEAD_V), const),
        ],
        out_specs=[
            pl.BlockSpec((rows, MIX_WIDTH), cur),
            pl.BlockSpec((rows, MIX_WIDTH), prev),
        ] + w_specs,
        out_shape=[
            jax.ShapeDtypeStruct((n, MIX_WIDTH), F32),
            jax.ShapeDtypeStruct((n, MIX_WIDTH), BF16),
            jax.ShapeDtypeStruct(w_out.shape, BF16),
            jax.ShapeDtypeStruct(w1.shape, BF16),
            jax.ShapeDtypeStruct(w2.shape, BF16),
        ],
        scratch_shapes=[pltpu.VMEM((2, rows, KN_COLS), BF16),
                        pltpu.VMEM((2, chunks, MIX_WIDTH, CHUNK), BF16),
                        pltpu.VMEM((2, chunks, MIX_WIDTH, CHUNK), BF16),
                        pltpu.VMEM((2, chunks, 1, 2 * DEC_COLS), F32),
                        pltpu.VMEM((N_PAIRS, HEAD_V, 2 * LANES), F32)],
        compiler_params=pltpu.CompilerParams(
            dimension_semantics=("arbitrary",), vmem_limit_bytes=V7X_VMEM_LIMIT),
        name="mixer",
    )(x2d, g, w_main, w_ga, w_up, b_up, lb_table, w_out, w1, w2, meta_kn, meta_v, hnorm, gnorm)


def _ffn_kernel(x_ref, o_ref, gate_ref, wout_ref, g2_ref, w1_ref, w2_ref, g3_ref, y_ref):
    n = x_ref.shape[0]
    hs, us = [], []
    for rows in (slice(0, n // 2), slice(n // 2, n)):
        o = o_ref[rows, :].astype(F32) * _silu(gate_ref[rows, :])
        hs.append(x_ref[rows, :] + _dot(o, wout_ref[...]))
        us.append(_rmsnorm(hs[-1], g2_ref[...]).astype(BF16))
    h = jnp.concatenate(hs, axis=0)
    u = jnp.concatenate(us, axis=0)
    acts = []
    for j in range(0, FFN_HIDDEN, FFN_BLOCK):
        gate = jnp.dot(u, w1_ref[:, j:j + FFN_BLOCK], preferred_element_type=F32)
        up = jnp.dot(u, w1_ref[:, FFN_HIDDEN + j:FFN_HIDDEN + j + FFN_BLOCK],
                     preferred_element_type=F32)
        acts.append((_silu(gate) * up).astype(BF16))
    acts = jnp.concatenate(acts, axis=1)
    for rows in (slice(0, n // 2), slice(n // 2, 3 * n // 4), slice(3 * n // 4, n)):
        ffn = jnp.dot(acts[rows], w2_ref[...], preferred_element_type=F32)
        y_ref[rows, :] = _rmsnorm(h[rows] + ffn, g3_ref[...])


def _ffn(x2d, o, gates, w_out, g2, w1, w2, g3, rows):
    n = x2d.shape[0]
    const = lambda i: (0, 0)
    tile = lambda i: (i, 0)
    resident = functools.partial(pl.BlockSpec, index_map=const, pipeline_mode=pl.Buffered(1))
    return pl.pallas_call(
        _ffn_kernel,
        grid=(n // rows,),
        in_specs=[
            pl.BlockSpec((rows, D_MODEL), tile),
            pl.BlockSpec((rows, MIX_WIDTH), tile),
            pl.BlockSpec((rows, MIX_WIDTH), tile),
            resident((MIX_WIDTH, D_MODEL)),
            pl.BlockSpec((1, D_MODEL), const),
            resident((D_MODEL, 2 * FFN_HIDDEN)),
            resident((FFN_HIDDEN, D_MODEL)),
            pl.BlockSpec((1, D_MODEL), const),
        ],
        out_specs=pl.BlockSpec((rows, D_MODEL), tile),
        out_shape=jax.ShapeDtypeStruct((n, D_MODEL), F32),
        compiler_params=pltpu.CompilerParams(
            dimension_semantics=("arbitrary",), vmem_limit_bytes=V7X_VMEM_LIMIT),
        name="outproj_ffn",
    )(x2d, o, gates, w_out, g2, w1, w2, g3)


def kernel(x, meta_tokens, lb_table, norm_mix_g, w_in, w_gla_gate_up, b_gla_gate, hgrn_norm_g,
           gla_norm_g, w_out, norm_ffn_g, w_ffn_in, w_ffn_out, norm_final_g):
    batch, seq, _ = x.shape
    assert w_in.shape[0] == 1 and lb_table.shape[0] == 2, "single-layer block"
    assert seq % ROWS_MIXER == 0
    x2d = x.reshape(batch * seq, D_MODEL)
    g_mix = norm_mix_g[0][None, :]
    w_up = w_gla_gate_up[0]
    b_up = b_gla_gate[0][None, :]

    w_main, w_ga, meta_kn, meta_v = _meta(meta_tokens, g_mix, w_in[0].T, w_up, b_up, lb_table)
    gates, o, w_out_b, w1_b, w2_b = _fused(
        x2d, g_mix, w_main, w_ga, w_up, b_up, lb_table, w_out[0], w_ffn_in[0], w_ffn_out[0],
        meta_kn, meta_v, hgrn_norm_g[0][None, :], gla_norm_g[0][None, :], seq, ROWS_MIXER)
    y = _ffn(x2d, o, gates, w_out_b, norm_ffn_g[0][None, :], w1_b, w2_b, norm_final_g[None, :],
             ROWS_FFN)
    return y.reshape(batch, seq, D_MODEL)
```

```python
import functools

import jax
import jax.numpy as jnp
from jax import lax
from jax.experimental import pallas as pl
from jax.experimental.pallas import tpu as pltpu

F32 = jnp.float32
BF16 = jnp.bfloat16

D_MODEL = 1024
N_META = 16
CHUNK = 128
HEADS = 4
HEAD_V = 128
HGRN_WIDTH = HEADS * HEAD_V
GLA_K = 64
GLA_QK_WIDTH = HEADS * GLA_K
GLA_WIDTH = HEADS * HEAD_V
MIX_WIDTH = HGRN_WIDTH + GLA_WIDTH
GATE_RANK = 16
GATE_NORMALIZER = 16.0
LOG2_E = 1.4426950408889634
FFN_HIDDEN = 2816
NORM_EPS = 1e-6
LANES = 128
N_PAIRS = HEADS

C_HQ, C_HF, C_HI, C_HG = 0, 512, 1024, 1536
C_GQ, C_GK, C_GV, C_GG = 2048, 2304, 2560, 3072
C_GA = 3584
IN_COLS = C_GA + GATE_RANK
W_BLOCK = 1792

KM_H, KM_G = 0, HGRN_WIDTH
KE_H, KE_G = HGRN_WIDTH + GLA_QK_WIDTH, 2 * HGRN_WIDTH + GLA_QK_WIDTH
KN_COLS = KE_G + GLA_WIDTH
DEC_COLS = HGRN_WIDTH + GLA_QK_WIDTH

HEAD_TABLE = tuple(
    [(KM_H + LANES * h, KE_H + LANES * h, LANES * h) for h in range(HEADS)]
    + [(KM_G + LANES * (h // 2), KE_G + LANES * h, HGRN_WIDTH + LANES * (h // 2))
       for h in range(HEADS)])

V7X_VMEM_LIMIT = 58 * 1024 * 1024

ROWS_MIXER = 512
ROWS_FFN = 1024
FFN_BLOCK = 256


def _rmsnorm(x, g):
    return x * lax.rsqrt(jnp.mean(x * x, axis=-1, keepdims=True) + NORM_EPS) * g


def _dot(a, b):
    return jnp.dot(a.astype(BF16), b.astype(BF16), preferred_element_type=F32)


def _dot_nt(a, b):
    return lax.dot_general(a.astype(BF16), b.astype(BF16), (((1,), (1,)), ((), ())),
                           preferred_element_type=F32)


def _dot_tn(a, b):
    return lax.dot_general(a.astype(BF16), b.astype(BF16), (((0,), (0,)), ((), ())),
                           preferred_element_type=F32)


def _sigmoid(x):
    return 0.5 * jnp.tanh(0.5 * x) + 0.5


def _silu(x):
    return x * _sigmoid(x)


def _log_sigmoid(x):
    return jnp.minimum(x, 0.0) - jnp.log1p(jnp.exp(-jnp.abs(x)))


def _iota2(n, axis):
    return lax.broadcasted_iota(jnp.int32, (n, n), axis)


def _cumsum_rows(g, tri2_bf16):
    g_hi = g.astype(BF16)
    g_lo = (g - g_hi.astype(F32)).astype(BF16)
    return jnp.dot(tri2_bf16, jnp.concatenate([g_hi, g_lo], axis=0),
                   preferred_element_type=F32)


def _own_half(x, h):
    lane = lax.broadcasted_iota(jnp.int32, x.shape, 1)
    return jnp.where((lane >= GLA_K) == bool(h % 2), x, 0.0)


def _project(x_ref, g_ref, w_ref, wga_ref, wup_ref, bup_ref, lbt_ref, kn_ref, qt_ref, vt_ref,
             gate_ref, dec_ref, chunk, transposed, between=None):
    n_chunks = x_ref.shape[0] // chunk
    u = _rmsnorm(x_ref[...], g_ref[...]).astype(BF16)
    tri = (_iota2(chunk, 0) >= _iota2(chunk, 1)).astype(BF16)
    tri = jnp.concatenate([tri, tri], axis=1)
    mid = chunk // 2 - 1

    def proj(col, width):
        blk, off = divmod(col, W_BLOCK)
        assert off + width <= W_BLOCK
        return jnp.dot(u, w_ref[blk, :, off:off + width], preferred_element_type=F32)

    def decay_factors(g, c):
        b = _cumsum_rows(g[c * chunk:(c + 1) * chunk], tri)
        b_mid = b[mid:mid + 1, :]
        b_last = b[chunk - 1:, :]
        return (jnp.exp2(b - b_mid), jnp.exp2(b_mid - b), jnp.exp2(b_last - b),
                jnp.exp2(b_mid), jnp.exp2(b_last))

    def put_tile(ref, c, i, tile):
        if transposed:
            ref[c, LANES * i:LANES * (i + 1), :] = tile.T
        else:
            ref[c, :, LANES * i:LANES * (i + 1)] = tile

    t0 = lbt_ref[0:1, :]
    t1 = lbt_ref[1:2, :]
    m = jnp.maximum(t0, t1)
    e0 = jnp.exp(t0 - m)
    lb = e0 / (e0 + jnp.exp(t1 - m))

    def value_piece(col, first_head):
        def emit():
            v = proj(col, 2 * LANES).astype(BF16)
            for c in range(n_chunks):
                for h in range(2):
                    put_tile(vt_ref, c, first_head + h,
                             v[c * chunk:(c + 1) * chunk, LANES * h:LANES * (h + 1)])
        return emit

    def gate_piece(col, out_col):
        def emit():
            gate_ref[:, out_col:out_col + 2 * LANES] = proj(col, 2 * LANES)
        return emit

    light = [value_piece(C_HI, 0), value_piece(C_HI + 2 * LANES, 2),
             value_piece(C_GV, HEADS), value_piece(C_GV + 2 * LANES, HEADS + 2),
             gate_piece(C_HG, 0), gate_piece(C_HG + 2 * LANES, 2 * LANES),
             gate_piece(C_GG, HGRN_WIDTH), gate_piece(C_GG + 2 * LANES, HGRN_WIDTH + 2 * LANES)]

    def pause():
        if between is not None:
            between()

    def next_light():
        if light:
            light.pop(0)()
        pause()

    def ga_part(k):
        cols = slice(k * (D_MODEL // 4), (k + 1) * (D_MODEL // 4))
        return _dot_nt(wga_ref[:, cols], u[:, cols])

    ga_t = ga_part(0)
    hf = proj(C_HF, HGRN_WIDTH)
    ga_t += ga_part(1)
    hq = proj(C_HQ, HGRN_WIDTH)

    f = 0.5 * (1.0 + lb) + (0.5 * (1.0 - lb)) * jnp.tanh(0.5 * hf)
    g = jnp.log2(f)
    k = 1.0 - f
    q = _silu(hq)
    pause()
    ga_t += ga_part(2)
    gq = proj(C_GQ, GLA_QK_WIDTH)
    ga_t += ga_part(3)
    gk = proj(C_GK, GLA_QK_WIDTH)
    pause()
    for c in range(n_chunks):
        rows = slice(c * chunk, (c + 1) * chunk)
        e_q, e_k, e_end, e_mid, e_last = decay_factors(g, c)
        kn_ref[rows, KM_H:KM_H + HGRN_WIDTH] = (k[rows] * e_k).astype(BF16)
        kn_ref[rows, KE_H:KE_H + HGRN_WIDTH] = (k[rows] * e_end).astype(BF16)
        qm = (q[rows] * e_q).astype(BF16)
        for h in range(HEADS):
            put_tile(qt_ref, c, h, qm[:, LANES * h:LANES * (h + 1)])
        dec_ref[c, :, 0:HGRN_WIDTH] = e_mid
        dec_ref[c, :, DEC_COLS:DEC_COLS + HGRN_WIDTH] = e_last
        next_light()

    next_light()
    logits = _dot_tn(ga_t, wup_ref[...]) + bup_ref[...]
    g = _log_sigmoid(logits) * (LOG2_E / GATE_NORMALIZER)
    q = gq * (GLA_K ** -0.5)
    k = gk
    for c in range(n_chunks):
        rows = slice(c * chunk, (c + 1) * chunk)
        e_q, e_k, e_end, e_mid, e_last = decay_factors(g, c)
        qm = q[rows] * e_q
        ke = k[rows] * e_end
        kn_ref[rows, KM_G:KM_G + GLA_QK_WIDTH] = (k[rows] * e_k).astype(BF16)
        for h in range(HEADS):
            pair = slice(LANES * (h // 2), LANES * (h // 2 + 1))
            put_tile(qt_ref, c, HEADS + h, _own_half(qm[:, pair], h).astype(BF16))
            kn_ref[rows, KE_G + LANES * h:KE_G + LANES * (h + 1)] = (
                _own_half(ke[:, pair], h).astype(BF16))
        dec_ref[c, :, HGRN_WIDTH:DEC_COLS] = e_mid
        dec_ref[c, :, DEC_COLS + HGRN_WIDTH:2 * DEC_COLS] = e_last
        next_light()
    while light:
        next_light()


def _meta_kernel(x_ref, g_ref, wt_ref, wgat_ref, wup_ref, bup_ref, lbt_ref,
                 w_ref, wga_ref, kn_ref, v_ref, w_scr, q_ref, gate_ref, dec_ref):
    j = pl.program_id(0)
    n_blocks = w_scr.shape[0]

    @pl.when(j < n_blocks)
    def _():
        blk = wt_ref[...].T.astype(BF16)
        w_ref[0] = blk
        w_scr[j] = blk

    @pl.when(j == n_blocks)
    def _():
        wga_ref[...] = wgat_ref[...].astype(BF16)
        _project(x_ref, g_ref, w_scr, wga_ref, wup_ref, bup_ref, lbt_ref, kn_ref, q_ref, v_ref,
                 gate_ref, dec_ref, N_META, transposed=False)


def _meta(meta_tokens, g, w_in_t, w_up, b_up, lb_table):
    n_blocks = C_GA // W_BLOCK
    full = lambda a: pl.BlockSpec(a.shape, lambda j: (0,) * len(a.shape))
    last = n_blocks - 1
    out_shapes = [jax.ShapeDtypeStruct((n_blocks, D_MODEL, W_BLOCK), BF16),
                  jax.ShapeDtypeStruct((GATE_RANK, D_MODEL), BF16),
                  jax.ShapeDtypeStruct((N_META, KN_COLS), BF16),
                  jax.ShapeDtypeStruct((1, N_META, MIX_WIDTH), BF16)]
    return pl.pallas_call(
        _meta_kernel,
        grid=(n_blocks + 1,),
        in_specs=[
            full(meta_tokens), full(g),
            pl.BlockSpec((W_BLOCK, D_MODEL), lambda j: (jnp.minimum(j, last), 0)),
            pl.BlockSpec((GATE_RANK, D_MODEL), lambda j: (C_GA // GATE_RANK, 0)),
            full(w_up), full(b_up), full(lb_table),
        ],
        out_specs=[pl.BlockSpec((1, D_MODEL, W_BLOCK), lambda j: (jnp.minimum(j, last), 0, 0))]
        + [full(s) for s in out_shapes[1:]],
        out_shape=out_shapes,
        scratch_shapes=[pltpu.VMEM((n_blocks, D_MODEL, W_BLOCK), BF16),
                        pltpu.VMEM((1, N_META, MIX_WIDTH), BF16),
                        pltpu.VMEM((N_META, MIX_WIDTH), F32),
                        pltpu.VMEM((1, 1, 2 * DEC_COLS), F32)],
        compiler_params=pltpu.CompilerParams(
            dimension_semantics=("arbitrary",), vmem_limit_bytes=V7X_VMEM_LIMIT),
        name="meta",
    )(meta_tokens, g, w_in_t, w_in_t, w_up, b_up, lb_table)


def _side_by_side(a, b):
    return jnp.concatenate([a, b], axis=1)


def _block_diag(a, b):
    return jnp.concatenate([_side_by_side(a, jnp.zeros_like(b)),
                            _side_by_side(jnp.zeros_like(a), b)], axis=0)


def _seed_state(mkn_ref, mv_ref, st_ref):
    seed = [_dot_tn(mv_ref[0, :, LANES * i:LANES * (i + 1)], mkn_ref[:, ke_col:ke_col + LANES])
            for i, (_, ke_col, _) in enumerate(HEAD_TABLE)]
    for p in range(N_PAIRS):
        st_ref[p] = _side_by_side(seed[2 * p], seed[2 * p + 1])


def _recurrence_steps(kn_ref, qt_ref, vt_ref, dec_ref, hnorm_ref, gnorm_ref, o_ref, st_ref):
    key_before_query = _iota2(CHUNK, 0) <= _iota2(CHUNK, 1)
    key_before_query = _side_by_side(key_before_query, key_before_query)
    norms = (hnorm_ref[...],) * HEADS + (gnorm_ref[...],) * HEADS

    def pair(fn, p):
        return fn(2 * p), fn(2 * p + 1)

    for c in range(kn_ref.shape[0] // CHUNK):
        rows = slice(c * CHUNK, (c + 1) * CHUNK)
        dec = dec_ref[c]
        km = lambda i: kn_ref[rows, HEAD_TABLE[i][0]:HEAD_TABLE[i][0] + LANES]
        ke = lambda i: kn_ref[rows, HEAD_TABLE[i][1]:HEAD_TABLE[i][1] + LANES]
        e_mid = lambda i: dec[:, HEAD_TABLE[i][2]:HEAD_TABLE[i][2] + LANES]
        e_last = lambda i: dec[:, DEC_COLS + HEAD_TABLE[i][2]:DEC_COLS + HEAD_TABLE[i][2] + LANES]
        qm_t = lambda i: qt_ref[c, LANES * i:LANES * (i + 1), :]
        v_t = lambda i: vt_ref[c, LANES * i:LANES * (i + 1), :]
        first, outs = [], []
        for p in range(N_PAIRS):
            s_mid = st_ref[p] * _side_by_side(*pair(e_mid, p))
            lhs = jnp.concatenate([_side_by_side(*pair(km, p)), s_mid.astype(BF16)], axis=0)
            first.append(jnp.dot(lhs, _block_diag(*pair(qm_t, p)),
                                 preferred_element_type=F32))
        yield
        for p in range(N_PAIRS):
            scores_t = jnp.where(key_before_query, first[p][:CHUNK], 0.0).astype(BF16)
            outs.append(first[p][CHUNK:]
                        + jnp.dot(_side_by_side(*pair(v_t, p)),
                                  _block_diag(scores_t[:, :CHUNK], scores_t[:, CHUNK:]),
                                  preferred_element_type=F32))
        yield
        for p in range(N_PAIRS):
            st_ref[p] = (st_ref[p] * _side_by_side(*pair(e_last, p))
                         + jnp.dot(_side_by_side(*pair(v_t, p)), _block_diag(*pair(ke, p)),
                                   preferred_element_type=F32))
        yield
        for p in range(N_PAIRS):
            o_t = outs[p]
            o_t = o_t * lax.rsqrt(jnp.mean(o_t * o_t, axis=0, keepdims=True) + NORM_EPS)
            o = o_t.T
            for half, i in enumerate((2 * p, 2 * p + 1)):
                o_ref[rows, LANES * i:LANES * (i + 1)] = (
                    o[half * CHUNK:(half + 1) * CHUNK] * norms[i]).astype(o_ref.dtype)
        yield


STAGES_PER_PAUSE = (1, 1, 2, 1, 2, 1, 1, 2, 1, 2, 2)


def _fused_kernel(x_ref, g_ref, w_ref, wga_ref, wup_ref, bup_ref, lbt_ref, wout32_ref, w1_32_ref,
                  w2_32_ref, mkn_ref, mv_ref, hnorm_ref, gnorm_ref,
                  gate_ref, o_ref, wout_ref, w1_ref, w2_ref,
                  kn_scr, qt_scr, vt_scr, dec_scr, st_ref, *, n_tiles, steps_per_seq):
    s = pl.program_id(0)

    @pl.when(lax.rem(s + steps_per_seq - 1, steps_per_seq) == 0)
    def _():
        _seed_state(mkn_ref, mv_ref, st_ref)

    def project(fill, between=None):
        _project(x_ref, g_ref, w_ref, wga_ref, wup_ref, bup_ref, lbt_ref, kn_scr.at[fill],
                 qt_scr.at[fill], vt_scr.at[fill], gate_ref, dec_scr.at[fill], CHUNK,
                 transposed=True, between=between)
        wout_ref[...] = wout32_ref[...].astype(BF16)
        w1_ref[...] = w1_32_ref[...].astype(BF16)
        w2_ref[...] = w2_32_ref[...].astype(BF16)

    def recurrence(drain):
        return _recurrence_steps(kn_scr.at[drain], qt_scr.at[drain], vt_scr.at[drain],
                                 dec_scr.at[drain], hnorm_ref, gnorm_ref, o_ref, st_ref)

    @pl.when(s == 0)
    def _():
        project(0)

    for parity in (0, 1):
        @pl.when((s > 0) & (s < n_tiles) & (lax.rem(s, 2) == parity))
        def _():
            stages = recurrence(1 - parity)
            budget = list(STAGES_PER_PAUSE)

            def between():
                for _ in range(budget.pop(0) if budget else 1):
                    next(stages, None)

            project(parity, between)
            for _ in stages:
                pass

    @pl.when(s == n_tiles)
    def _():
        for _ in recurrence((n_tiles - 1) % 2):
            pass


def _fused(x2d, g, w_main, w_ga, w_up, b_up, lb_table, w_out, w1, w2, meta_kn, meta_v, hnorm,
           gnorm, seq, rows):
    n = x2d.shape[0]
    n_tiles = n // rows
    chunks = rows // CHUNK
    last = n_tiles - 1
    wout_rows = w_out.shape[0] // n_tiles
    w1_rows = w1.shape[0] // n_tiles
    w2_rows = 2 * w2.shape[0] // n_tiles
    assert wout_rows * n_tiles == w_out.shape[0] and w1_rows * n_tiles == w1.shape[0]
    assert w2_rows * n_tiles == 2 * w2.shape[0]
    const = lambda s: (0, 0)
    const3 = lambda s: (0, 0, 0)
    cur = lambda s: (jnp.minimum(s, last), 0)
    prev = lambda s: (jnp.maximum(s - 1, 0), 0)
    half_pace = lambda s: (jnp.minimum(s, last) // 2, 0)
    w_specs = [pl.BlockSpec((wout_rows, w_out.shape[1]), cur),
               pl.BlockSpec((w1_rows, w1.shape[1]), cur),
               pl.BlockSpec((w2_rows, w2.shape[1]), half_pace)]
    return pl.pallas_call(
        functools.partial(_fused_kernel, n_tiles=n_tiles, steps_per_seq=seq // rows),
        grid=(n_tiles + 1,),
        in_specs=[
            pl.BlockSpec((rows, D_MODEL), cur),
            pl.BlockSpec((1, D_MODEL), const),
            pl.BlockSpec(w_main.shape, const3, pipeline_mode=pl.Buffered(1)),
            pl.BlockSpec(w_ga.shape, const),
            pl.BlockSpec((GATE_RANK, GLA_QK_WIDTH), const),
            pl.BlockSpec((1, GLA_QK_WIDTH), const),
            pl.BlockSpec(lb_table.shape, const),
        ] + w_specs + [
            pl.BlockSpec(meta_kn.shape, const),
            pl.BlockSpec(meta_v.shape, const3),
            pl.BlockSpec((1, HEAD_V), const),
            pl.BlockSpec((1, HEAD_V), const),
        ],
        out_specs=[
            pl.BlockSpec((rows, MIX_WIDTH), cur),
            pl.BlockSpec((rows, MIX_WIDTH), prev),
        ] + w_specs,
        out_shape=[
            jax.ShapeDtypeStruct((n, MIX_WIDTH), F32),
            jax.ShapeDtypeStruct((n, MIX_WIDTH), BF16),
            jax.ShapeDtypeStruct(w_out.shape, BF16),
            jax.ShapeDtypeStruct(w1.shape, BF16),
            jax.ShapeDtypeStruct(w2.shape, BF16),
        ],
        scratch_shapes=[pltpu.VMEM((2, rows, KN_COLS), BF16),
                        pltpu.VMEM((2, chunks, MIX_WIDTH, CHUNK), BF16),
                        pltpu.VMEM((2, chunks, MIX_WIDTH, CHUNK), BF16),
                        pltpu.VMEM((2, chunks, 1, 2 * DEC_COLS), F32),
                        pltpu.VMEM((N_PAIRS, HEAD_V, 2 * LANES), F32)],
        compiler_params=pltpu.CompilerParams(
            dimension_semantics=("arbitrary",), vmem_limit_bytes=V7X_VMEM_LIMIT),
        name="mixer",
    )(x2d, g, w_main, w_ga, w_up, b_up, lb_table, w_out, w1, w2, meta_kn, meta_v, hnorm, gnorm)


def _ffn_kernel(x_ref, o_ref, gate_ref, wout_hbm, g2_ref, w1_hbm, w2_hbm, g3_ref, y_ref,
                wout_ref, w1_ref, w2_ref, sems):
    first_step = pl.program_id(0) == 0
    copies = [pltpu.make_async_copy(src, dst, sems.at[k]) for k, (src, dst) in
              enumerate(((wout_hbm, wout_ref), (w1_hbm, w1_ref), (w2_hbm, w2_ref)))]

    def body(loading):
        if loading:
            for copy in copies:
                copy.start()
            copies[0].wait()
        o = o_ref[...].astype(F32) * _silu(gate_ref[...])
        h = x_ref[...] + _dot(o, wout_ref[...])
        u = _rmsnorm(h, g2_ref[...]).astype(BF16)
        if loading:
            copies[1].wait()
        acts = []
        for j in range(0, FFN_HIDDEN, FFN_BLOCK):
            gate = jnp.dot(u, w1_ref[:, j:j + FFN_BLOCK], preferred_element_type=F32)
            up = jnp.dot(u, w1_ref[:, FFN_HIDDEN + j:FFN_HIDDEN + j + FFN_BLOCK],
                         preferred_element_type=F32)
            acts.append((_silu(gate) * up).astype(BF16))
        if loading:
            copies[2].wait()
        ffn = jnp.dot(jnp.concatenate(acts, axis=1), w2_ref[...], preferred_element_type=F32)
        y_ref[...] = _rmsnorm(h + ffn, g3_ref[...])

    pl.when(first_step)(functools.partial(body, True))
    pl.when(jnp.logical_not(first_step))(functools.partial(body, False))


def _ffn(x2d, o, gates, w_out, g2, w1, w2, g3, rows):
    n = x2d.shape[0]
    const = lambda i: (0, 0)
    tile = lambda i: (i, 0)
    in_hbm = pl.BlockSpec(memory_space=pl.ANY)
    return pl.pallas_call(
        _ffn_kernel,
        grid=(n // rows,),
        in_specs=[
            pl.BlockSpec((rows, D_MODEL), tile),
            pl.BlockSpec((rows, MIX_WIDTH), tile),
            pl.BlockSpec((rows, MIX_WIDTH), tile),
            in_hbm,
            pl.BlockSpec((1, D_MODEL), const),
            in_hbm,
            in_hbm,
            pl.BlockSpec((1, D_MODEL), const),
        ],
        out_specs=pl.BlockSpec((rows, D_MODEL), tile),
        out_shape=jax.ShapeDtypeStruct((n, D_MODEL), F32),
        scratch_shapes=[pltpu.VMEM(w_out.shape, BF16), pltpu.VMEM(w1.shape, BF16),
                        pltpu.VMEM(w2.shape, BF16), pltpu.SemaphoreType.DMA((3,))],
        compiler_params=pltpu.CompilerParams(
            dimension_semantics=("arbitrary",), vmem_limit_bytes=V7X_VMEM_LIMIT),
        name="outproj_ffn",
    )(x2d, o, gates, w_out, g2, w1, w2, g3)


def kernel(x, meta_tokens, lb_table, norm_mix_g, w_in, w_gla_gate_up, b_gla_gate, hgrn_norm_g,
           gla_norm_g, w_out, norm_ffn_g, w_ffn_in, w_ffn_out, norm_final_g):
    batch, seq, _ = x.shape
    assert w_in.shape[0] == 1 and lb_table.shape[0] == 2, "single-layer block"
    assert seq % ROWS_MIXER == 0
    x2d = x.reshape(batch * seq, D_MODEL)
    g_mix = norm_mix_g[0][None, :]
    w_up = w_gla_gate_up[0]
    b_up = b_gla_gate[0][None, :]

    w_main, w_ga, meta_kn, meta_v = _meta(meta_tokens, g_mix, w_in[0].T, w_up, b_up, lb_table)
    gates, o, w_out_b, w1_b, w2_b = _fused(
        x2d, g_mix, w_main, w_ga, w_up, b_up, lb_table, w_out[0], w_ffn_in[0], w_ffn_out[0],
        meta_kn, meta_v, hgrn_norm_g[0][None, :], gla_norm_g[0][None, :], seq, ROWS_MIXER)
    y = _ffn(x2d, o, gates, w_out_b, norm_ffn_g[0][None, :], w1_b, w2_b, norm_final_g[None, :],
             ROWS_FFN)
    return y.reshape(batch, seq, D_MODEL)
```

```python
import functools

import jax
import jax.numpy as jnp
from jax import lax
from jax.experimental import pallas as pl
from jax.experimental.pallas import tpu as pltpu

F32 = jnp.float32
BF16 = jnp.bfloat16

D_MODEL = 1024
N_META = 16
CHUNK = 128
HEADS = 4
HEAD_V = 128
HGRN_WIDTH = HEADS * HEAD_V
GLA_K = 64
GLA_QK_WIDTH = HEADS * GLA_K
GLA_WIDTH = HEADS * HEAD_V
MIX_WIDTH = HGRN_WIDTH + GLA_WIDTH
GATE_RANK = 16
GATE_NORMALIZER = 16.0
LOG2_E = 1.4426950408889634
FFN_HIDDEN = 2816
NORM_EPS = 1e-6
LANES = 128
N_PAIRS = HEADS

C_HQ, C_HF, C_HI, C_HG = 0, 512, 1024, 1536
C_GQ, C_GK, C_GV, C_GG = 2048, 2304, 2560, 3072
C_GA = 3584
IN_COLS = C_GA + GATE_RANK
W_BLOCK = 1792

KM_H, KM_G = 0, HGRN_WIDTH
KE_H, KE_G = HGRN_WIDTH + GLA_QK_WIDTH, 2 * HGRN_WIDTH + GLA_QK_WIDTH
KN_COLS = KE_G + GLA_WIDTH
DEC_COLS = HGRN_WIDTH + GLA_QK_WIDTH

HEAD_TABLE = tuple(
    [(KM_H + LANES * h, KE_H + LANES * h, LANES * h) for h in range(HEADS)]
    + [(KM_G + LANES * (h // 2), KE_G + LANES * h, HGRN_WIDTH + LANES * (h // 2))
       for h in range(HEADS)])

V7X_VMEM_LIMIT = 58 * 1024 * 1024

ROWS_MIXER = 512
ROWS_FFN = 1024
FFN_BLOCK = 256


def _rmsnorm(x, g):
    return x * lax.rsqrt(jnp.mean(x * x, axis=-1, keepdims=True) + NORM_EPS) * g


def _dot(a, b):
    return jnp.dot(a.astype(BF16), b.astype(BF16), preferred_element_type=F32)


def _dot_nt(a, b):
    return lax.dot_general(a.astype(BF16), b.astype(BF16), (((1,), (1,)), ((), ())),
                           preferred_element_type=F32)


def _dot_tn(a, b):
    return lax.dot_general(a.astype(BF16), b.astype(BF16), (((0,), (0,)), ((), ())),
                           preferred_element_type=F32)


def _sigmoid(x):
    return 0.5 * jnp.tanh(0.5 * x) + 0.5


def _silu(x):
    return x * _sigmoid(x)


def _log_sigmoid(x):
    return jnp.minimum(x, 0.0) - jnp.log1p(jnp.exp(-jnp.abs(x)))


def _iota2(n, axis):
    return lax.broadcasted_iota(jnp.int32, (n, n), axis)


def _cumsum_rows(g, tri2_bf16):
    g_hi = g.astype(BF16)
    g_lo = (g - g_hi.astype(F32)).astype(BF16)
    return jnp.dot(tri2_bf16, jnp.concatenate([g_hi, g_lo], axis=0),
                   preferred_element_type=F32)


def _own_half(x, h):
    lane = lax.broadcasted_iota(jnp.int32, x.shape, 1)
    return jnp.where((lane >= GLA_K) == bool(h % 2), x, 0.0)


def _project(x_ref, g_ref, w_ref, wga_ref, wup_ref, bup_ref, lbt_ref, kn_ref, qt_ref, vt_ref,
             gate_ref, dec_ref, chunk, transposed, between=None):
    n_chunks = x_ref.shape[0] // chunk
    u = _rmsnorm(x_ref[...], g_ref[...]).astype(BF16)
    tri = (_iota2(chunk, 0) >= _iota2(chunk, 1)).astype(BF16)
    tri = jnp.concatenate([tri, tri], axis=1)
    mid = chunk // 2 - 1

    def proj(col, width):
        blk, off = divmod(col, W_BLOCK)
        assert off + width <= W_BLOCK
        return jnp.dot(u, w_ref[blk, :, off:off + width], preferred_element_type=F32)

    def decay_factors(g, c):
        b = _cumsum_rows(g[c * chunk:(c + 1) * chunk], tri)
        b_mid = b[mid:mid + 1, :]
        b_last = b[chunk - 1:, :]
        return (jnp.exp2(b - b_mid), jnp.exp2(b_mid - b), jnp.exp2(b_last - b),
                jnp.exp2(b_mid), jnp.exp2(b_last))

    def put_tile(ref, c, i, tile):
        if transposed:
            ref[c, LANES * i:LANES * (i + 1), :] = tile.T
        else:
            ref[c, :, LANES * i:LANES * (i + 1)] = tile

    t0 = lbt_ref[0:1, :]
    t1 = lbt_ref[1:2, :]
    m = jnp.maximum(t0, t1)
    e0 = jnp.exp(t0 - m)
    lb = e0 / (e0 + jnp.exp(t1 - m))

    def value_piece(col, first_head):
        def emit():
            v = proj(col, 2 * LANES).astype(BF16)
            for c in range(n_chunks):
                for h in range(2):
                    put_tile(vt_ref, c, first_head + h,
                             v[c * chunk:(c + 1) * chunk, LANES * h:LANES * (h + 1)])
        return emit

    def gate_piece(col, out_col):
        def emit():
            gate_ref[:, out_col:out_col + 2 * LANES] = proj(col, 2 * LANES)
        return emit

    light = [value_piece(C_HI, 0), value_piece(C_HI + 2 * LANES, 2),
             value_piece(C_GV, HEADS), value_piece(C_GV + 2 * LANES, HEADS + 2),
             gate_piece(C_HG, 0), gate_piece(C_HG + 2 * LANES, 2 * LANES),
             gate_piece(C_GG, HGRN_WIDTH), gate_piece(C_GG + 2 * LANES, HGRN_WIDTH + 2 * LANES)]

    def pause():
        if between is not None:
            between()

    def next_light():
        if light:
            light.pop(0)()
        pause()

    def ga_part(k):
        cols = slice(k * (D_MODEL // 4), (k + 1) * (D_MODEL // 4))
        return _dot_nt(wga_ref[:, cols], u[:, cols])

    ga_t = ga_part(0)
    hf = proj(C_HF, HGRN_WIDTH)
    ga_t += ga_part(1)
    hq = proj(C_HQ, HGRN_WIDTH)

    f = 0.5 * (1.0 + lb) + (0.5 * (1.0 - lb)) * jnp.tanh(0.5 * hf)
    g = jnp.log2(f)
    k = 1.0 - f
    q = _silu(hq)
    pause()
    ga_t += ga_part(2)
    gq = proj(C_GQ, GLA_QK_WIDTH)
    ga_t += ga_part(3)
    gk = proj(C_GK, GLA_QK_WIDTH)
    pause()
    for c in range(n_chunks):
        rows = slice(c * chunk, (c + 1) * chunk)
        e_q, e_k, e_end, e_mid, e_last = decay_factors(g, c)
        kn_ref[rows, KM_H:KM_H + HGRN_WIDTH] = (k[rows] * e_k).astype(BF16)
        kn_ref[rows, KE_H:KE_H + HGRN_WIDTH] = (k[rows] * e_end).astype(BF16)
        qm = (q[rows] * e_q).astype(BF16)
        for h in range(HEADS):
            put_tile(qt_ref, c, h, qm[:, LANES * h:LANES * (h + 1)])
        dec_ref[c, :, 0:HGRN_WIDTH] = e_mid
        dec_ref[c, :, DEC_COLS:DEC_COLS + HGRN_WIDTH] = e_last
        next_light()

    next_light()
    logits = _dot_tn(ga_t, wup_ref[...]) + bup_ref[...]
    g = _log_sigmoid(logits) * (LOG2_E / GATE_NORMALIZER)
    q = gq * (GLA_K ** -0.5)
    k = gk
    for c in range(n_chunks):
        rows = slice(c * chunk, (c + 1) * chunk)
        e_q, e_k, e_end, e_mid, e_last = decay_factors(g, c)
        qm = q[rows] * e_q
        ke = k[rows] * e_end
        kn_ref[rows, KM_G:KM_G + GLA_QK_WIDTH] = (k[rows] * e_k).astype(BF16)
        for h in range(HEADS):
            pair = slice(LANES * (h // 2), LANES * (h // 2 + 1))
            put_tile(qt_ref, c, HEADS + h, _own_half(qm[:, pair], h).astype(BF16))
            kn_ref[rows, KE_G + LANES * h:KE_G + LANES * (h + 1)] = (
                _own_half(ke[:, pair], h).astype(BF16))
        dec_ref[c, :, HGRN_WIDTH:DEC_COLS] = e_mid
        dec_ref[c, :, DEC_COLS + HGRN_WIDTH:2 * DEC_COLS] = e_last
        next_light()
    while light:
        next_light()


def _meta_kernel(x_ref, g_ref, wt_ref, wgat_ref, wup_ref, bup_ref, lbt_ref,
                 w_ref, wga_ref, kn_ref, v_ref, w_scr, q_ref, gate_ref, dec_ref):
    j = pl.program_id(0)
    n_blocks = w_scr.shape[0]

    @pl.when(j < n_blocks)
    def _():
        blk = wt_ref[...].T.astype(BF16)
        w_ref[0] = blk
        w_scr[j] = blk

    @pl.when(j == n_blocks)
    def _():
        wga_ref[...] = wgat_ref[...].astype(BF16)
        _project(x_ref, g_ref, w_scr, wga_ref, wup_ref, bup_ref, lbt_ref, kn_ref, q_ref, v_ref,
                 gate_ref, dec_ref, N_META, transposed=False)


def _meta(meta_tokens, g, w_in_t, w_up, b_up, lb_table):
    n_blocks = C_GA // W_BLOCK
    full = lambda a: pl.BlockSpec(a.shape, lambda j: (0,) * len(a.shape))
    last = n_blocks - 1
    out_shapes = [jax.ShapeDtypeStruct((n_blocks, D_MODEL, W_BLOCK), BF16),
                  jax.ShapeDtypeStruct((GATE_RANK, D_MODEL), BF16),
                  jax.ShapeDtypeStruct((N_META, KN_COLS), BF16),
                  jax.ShapeDtypeStruct((1, N_META, MIX_WIDTH), BF16)]
    return pl.pallas_call(
        _meta_kernel,
        grid=(n_blocks + 1,),
        in_specs=[
            full(meta_tokens), full(g),
            pl.BlockSpec((W_BLOCK, D_MODEL), lambda j: (jnp.minimum(j, last), 0)),
            pl.BlockSpec((GATE_RANK, D_MODEL), lambda j: (C_GA // GATE_RANK, 0)),
            full(w_up), full(b_up), full(lb_table),
        ],
        out_specs=[pl.BlockSpec((1, D_MODEL, W_BLOCK), lambda j: (jnp.minimum(j, last), 0, 0))]
        + [full(s) for s in out_shapes[1:]],
        out_shape=out_shapes,
        scratch_shapes=[pltpu.VMEM((n_blocks, D_MODEL, W_BLOCK), BF16),
                        pltpu.VMEM((1, N_META, MIX_WIDTH), BF16),
                        pltpu.VMEM((N_META, MIX_WIDTH), F32),
                        pltpu.VMEM((1, 1, 2 * DEC_COLS), F32)],
        compiler_params=pltpu.CompilerParams(
            dimension_semantics=("arbitrary",), vmem_limit_bytes=V7X_VMEM_LIMIT),
        name="meta",
    )(meta_tokens, g, w_in_t, w_in_t, w_up, b_up, lb_table)


def _side_by_side(a, b):
    return jnp.concatenate([a, b], axis=1)


def _block_diag(a, b):
    return jnp.concatenate([_side_by_side(a, jnp.zeros_like(b)),
                            _side_by_side(jnp.zeros_like(a), b)], axis=0)


def _seed_state(mkn_ref, mv_ref, st_ref):
    seed = [_dot_tn(mv_ref[0, :, LANES * i:LANES * (i + 1)], mkn_ref[:, ke_col:ke_col + LANES])
            for i, (_, ke_col, _) in enumerate(HEAD_TABLE)]
    for p in range(N_PAIRS):
        st_ref[p] = _side_by_side(seed[2 * p], seed[2 * p + 1])


def _recurrence_steps(kn_ref, qt_ref, vt_ref, dec_ref, hnorm_ref, gnorm_ref, o_ref, st_ref):
    key_before_query = _iota2(CHUNK, 0) <= _iota2(CHUNK, 1)
    key_before_query = _side_by_side(key_before_query, key_before_query)
    norms = (hnorm_ref[...],) * HEADS + (gnorm_ref[...],) * HEADS

    def pair(fn, p):
        return fn(2 * p), fn(2 * p + 1)

    for c in range(kn_ref.shape[0] // CHUNK):
        rows = slice(c * CHUNK, (c + 1) * CHUNK)
        dec = dec_ref[c]
        km = lambda i: kn_ref[rows, HEAD_TABLE[i][0]:HEAD_TABLE[i][0] + LANES]
        ke = lambda i: kn_ref[rows, HEAD_TABLE[i][1]:HEAD_TABLE[i][1] + LANES]
        e_mid = lambda i: dec[:, HEAD_TABLE[i][2]:HEAD_TABLE[i][2] + LANES]
        e_last = lambda i: dec[:, DEC_COLS + HEAD_TABLE[i][2]:DEC_COLS + HEAD_TABLE[i][2] + LANES]
        qm_t = lambda i: qt_ref[c, LANES * i:LANES * (i + 1), :]
        v_t = lambda i: vt_ref[c, LANES * i:LANES * (i + 1), :]
        first, outs = [], []
        for p in range(N_PAIRS):
            s_mid = st_ref[p] * _side_by_side(*pair(e_mid, p))
            lhs = jnp.concatenate([_side_by_side(*pair(km, p)), s_mid.astype(BF16)], axis=0)
            first.append(jnp.dot(lhs, _block_diag(*pair(qm_t, p)),
                                 preferred_element_type=F32))
        yield
        for p in range(N_PAIRS):
            scores_t = jnp.where(key_before_query, first[p][:CHUNK], 0.0).astype(BF16)
            outs.append(first[p][CHUNK:]
                        + jnp.dot(_side_by_side(*pair(v_t, p)),
                                  _block_diag(scores_t[:, :CHUNK], scores_t[:, CHUNK:]),
                                  preferred_element_type=F32))
        yield
        for p in range(N_PAIRS):
            st_ref[p] = (st_ref[p] * _side_by_side(*pair(e_last, p))
                         + jnp.dot(_side_by_side(*pair(v_t, p)), _block_diag(*pair(ke, p)),
                                   preferred_element_type=F32))
        yield
        for p in range(N_PAIRS):
            o_t = outs[p]
            o_t = o_t * lax.rsqrt(jnp.mean(o_t * o_t, axis=0, keepdims=True) + NORM_EPS)
            o = o_t.T
            for half, i in enumerate((2 * p, 2 * p + 1)):
                o_ref[rows, LANES * i:LANES * (i + 1)] = (
                    o[half * CHUNK:(half + 1) * CHUNK] * norms[i]).astype(o_ref.dtype)
        yield


STAGES_PER_PAUSE = (1, 1, 2, 1, 2, 1, 1, 2, 1, 2, 2)


def _fused_kernel(x_ref, g_ref, w_ref, wga_ref, wup_ref, bup_ref, lbt_ref, wout32_ref, w1_32_ref,
                  w2_32_ref, mkn_ref, mv_ref, hnorm_ref, gnorm_ref,
                  gate_ref, o_ref, wout_ref, w1_ref, w2_ref,
                  kn_scr, qt_scr, vt_scr, dec_scr, st_ref, *, n_tiles, steps_per_seq):
    s = pl.program_id(0)

    @pl.when(lax.rem(s + steps_per_seq - 1, steps_per_seq) == 0)
    def _():
        _seed_state(mkn_ref, mv_ref, st_ref)

    def project(fill, between=None):
        _project(x_ref, g_ref, w_ref, wga_ref, wup_ref, bup_ref, lbt_ref, kn_scr.at[fill],
                 qt_scr.at[fill], vt_scr.at[fill], gate_ref, dec_scr.at[fill], CHUNK,
                 transposed=True, between=between)
        wout_ref[...] = wout32_ref[...].astype(BF16)
        w1_ref[...] = w1_32_ref[...].astype(BF16)
        w2_ref[...] = w2_32_ref[...].astype(BF16)

    def recurrence(drain):
        return _recurrence_steps(kn_scr.at[drain], qt_scr.at[drain], vt_scr.at[drain],
                                 dec_scr.at[drain], hnorm_ref, gnorm_ref, o_ref, st_ref)

    @pl.when(s == 0)
    def _():
        project(0)

    for parity in (0, 1):
        @pl.when((s > 0) & (s < n_tiles) & (lax.rem(s, 2) == parity))
        def _():
            stages = recurrence(1 - parity)
            budget = list(STAGES_PER_PAUSE)

            def between():
                for _ in range(budget.pop(0) if budget else 1):
                    next(stages, None)

            project(parity, between)
            for _ in stages:
                pass

    @pl.when(s == n_tiles)
    def _():
        for _ in recurrence((n_tiles - 1) % 2):
            pass


def _fused(x2d, g, w_main, w_ga, w_up, b_up, lb_table, w_out, w1, w2, meta_kn, meta_v, hnorm,
           gnorm, seq, rows):
    n = x2d.shape[0]
    n_tiles = n // rows
    chunks = rows // CHUNK
    last = n_tiles - 1
    wout_rows = w_out.shape[0] // n_tiles
    w1_rows = w1.shape[0] // n_tiles
    w2_rows = 2 * w2.shape[0] // n_tiles
    assert wout_rows * n_tiles == w_out.shape[0] and w1_rows * n_tiles == w1.shape[0]
    assert w2_rows * n_tiles == 2 * w2.shape[0]
    const = lambda s: (0, 0)
    const3 = lambda s: (0, 0, 0)
    cur = lambda s: (jnp.minimum(s, last), 0)
    prev = lambda s: (jnp.maximum(s - 1, 0), 0)
    half_pace = lambda s: (jnp.minimum(s, last) // 2, 0)
    w_specs = [pl.BlockSpec((wout_rows, w_out.shape[1]), cur),
               pl.BlockSpec((w1_rows, w1.shape[1]), cur),
               pl.BlockSpec((w2_rows, w2.shape[1]), half_pace)]
    return pl.pallas_call(
        functools.partial(_fused_kernel, n_tiles=n_tiles, steps_per_seq=seq // rows),
        grid=(n_tiles + 1,),
        in_specs=[
            pl.BlockSpec((rows, D_MODEL), cur),
            pl.BlockSpec((1, D_MODEL), const),
            pl.BlockSpec(w_main.shape, const3, pipeline_mode=pl.Buffered(1)),
            pl.BlockSpec(w_ga.shape, const),
            pl.BlockSpec((GATE_RANK, GLA_QK_WIDTH), const),
            pl.BlockSpec((1, GLA_QK_WIDTH), const),
            pl.BlockSpec(lb_table.shape, const),
        ] + w_specs + [
            pl.BlockSpec(meta_kn.shape, const),
            pl.BlockSpec(meta_v.shape, const3),
            pl.BlockSpec((1, HEAD_V), const),
            pl.BlockSpec((1, HEAD_V), const),
        ],
        out_specs=[
            pl.BlockSpec((rows, MIX_WIDTH), cur),
            pl.BlockSpec((rows, MIX_WIDTH), prev),
        ] + w_specs,
        out_shape=[
            jax.ShapeDtypeStruct((n, MIX_WIDTH), F32),
            jax.ShapeDtypeStruct((n, MIX_WIDTH), BF16),
            jax.ShapeDtypeStruct(w_out.shape, BF16),
            jax.ShapeDtypeStruct(w1.shape, BF16),
            jax.ShapeDtypeStruct(w2.shape, BF16),
        ],
        scratch_shapes=[pltpu.VMEM((2, rows, KN_COLS), BF16),
                        pltpu.VMEM((2, chunks, MIX_WIDTH, CHUNK), BF16),
                        pltpu.VMEM((2, chunks, MIX_WIDTH, CHUNK), BF16),
                        pltpu.VMEM((2, chunks, 1, 2 * DEC_COLS), F32),
                        pltpu.VMEM((N_PAIRS, HEAD_V, 2 * LANES), F32)],
        compiler_params=pltpu.CompilerParams(
            dimension_semantics=("arbitrary",), vmem_limit_bytes=V7X_VMEM_LIMIT),
        name="mixer",
    )(x2d, g, w_main, w_ga, w_up, b_up, lb_table, w_out, w1, w2, meta_kn, meta_v, hnorm, gnorm)


WEIGHT_COPY_PARTS = 8


def _ffn_kernel(x_ref, o_ref, gate_ref, wout_hbm, g2_ref, w1_hbm, w2_hbm, g3_ref, y_ref,
                wout_ref, w1_ref, w2_ref, sems):
    first_step = pl.program_id(0) == 0
    def row_copies(k, src, dst):
        rows = src.shape[0] // WEIGHT_COPY_PARTS
        assert rows * WEIGHT_COPY_PARTS == src.shape[0]
        return [pltpu.make_async_copy(src.at[pl.ds(p * rows, rows)], dst.at[pl.ds(p * rows, rows)],
                                      sems.at[k * WEIGHT_COPY_PARTS + p])
                for p in range(WEIGHT_COPY_PARTS)]

    copies = [row_copies(k, src, dst) for k, (src, dst) in
              enumerate(((wout_hbm, wout_ref), (w1_hbm, w1_ref), (w2_hbm, w2_ref)))]

    def wait(k):
        for copy in copies[k]:
            copy.wait()

    def body(loading):
        if loading:
            for group in copies:
                for copy in group:
                    copy.start()
            wait(0)
        o = o_ref[...].astype(F32) * _silu(gate_ref[...])
        h = x_ref[...] + _dot(o, wout_ref[...])
        u = _rmsnorm(h, g2_ref[...]).astype(BF16)
        if loading:
            wait(1)
        acts = []
        for j in range(0, FFN_HIDDEN, FFN_BLOCK):
            gate = jnp.dot(u, w1_ref[:, j:j + FFN_BLOCK], preferred_element_type=F32)
            up = jnp.dot(u, w1_ref[:, FFN_HIDDEN + j:FFN_HIDDEN + j + FFN_BLOCK],
                         preferred_element_type=F32)
            acts.append((_silu(gate) * up).astype(BF16))
        if loading:
            wait(2)
        ffn =jnp.dot(jnp.concatenate(acts, axis=1), w2_ref[...], preferred_element_type=F32)
        y_ref[...] = _rmsnorm(h + ffn, g3_ref[...])

    pl.when(first_step)(functools.partial(body, True))
    pl.when(jnp.logical_not(first_step))(functools.partial(body, False))


def _ffn(x2d, o, gates, w_out, g2, w1, w2, g3, rows):
    n = x2d.shape[0]
    const = lambda i: (0, 0)
    tile = lambda i: (i, 0)
    in_hbm = pl.BlockSpec(memory_space=pl.ANY)
    return pl.pallas_call(
        _ffn_kernel,
        grid=(n // rows,),
        in_specs=[
            pl.BlockSpec((rows, D_MODEL), tile),
            pl.BlockSpec((rows, MIX_WIDTH), tile),
            pl.BlockSpec((rows, MIX_WIDTH), tile),
            in_hbm,
            pl.BlockSpec((1, D_MODEL), const),
            in_hbm,
            in_hbm,
            pl.BlockSpec((1, D_MODEL), const),
        ],
        out_specs=pl.BlockSpec((rows, D_MODEL), tile),
        out_shape=jax.ShapeDtypeStruct((n, D_MODEL), F32),
        scratch_shapes=[pltpu.VMEM(w_out.shape, BF16), pltpu.VMEM(w1.shape, BF16),
                        pltpu.VMEM(w2.shape, BF16),
                        pltpu.SemaphoreType.DMA((3 * WEIGHT_COPY_PARTS,))],
        compiler_params=pltpu.CompilerParams(
            dimension_semantics=("arbitrary",), vmem_limit_bytes=V7X_VMEM_LIMIT),
        name="outproj_ffn",
    )(x2d, o, gates, w_out, g2, w1, w2, g3)


def kernel(x, meta_tokens, lb_table, norm_mix_g, w_in, w_gla_gate_up, b_gla_gate, hgrn_norm_g,
           gla_norm_g, w_out, norm_ffn_g, w_ffn_in, w_ffn_out, norm_final_g):
    batch, seq, _ = x.shape
    assert w_in.shape[0] == 1 and lb_table.shape[0] == 2, "single-layer block"
    assert seq % ROWS_MIXER == 0
    x2d = x.reshape(batch * seq, D_MODEL)
    g_mix = norm_mix_g[0][None, :]
    w_up = w_gla_gate_up[0]
    b_up = b_gla_gate[0][None, :]

    w_main, w_ga, meta_kn, meta_v = _meta(meta_tokens, g_mix, w_in[0].T, w_up, b_up, lb_table)
    gates, o, w_out_b, w1_b, w2_b = _fused(
        x2d, g_mix, w_main, w_ga, w_up, b_up, lb_table, w_out[0], w_ffn_in[0], w_ffn_out[0],
        meta_kn, meta_v, hgrn_norm_g[0][None, :], gla_norm_g[0][None, :], seq, ROWS_MIXER)
    y = _ffn(x2d, o, gates, w_out_b, norm_ffn_g[0][None, :], w1_b, w2_b, norm_final_g[None, :],
             ROWS_FFN)
    return y.reshape(batch, seq, D_MODEL)
```

```python
import functools

import jax
import jax.numpy as jnp
from jax import lax
from jax.experimental import pallas as pl
from jax.experimental.pallas import tpu as pltpu

F32 = jnp.float32
BF16 = jnp.bfloat16

D_MODEL = 1024
N_META = 16
CHUNK = 128
HEADS = 4
HEAD_V = 128
HGRN_WIDTH = HEADS * HEAD_V
GLA_K = 64
GLA_QK_WIDTH = HEADS * GLA_K
GLA_WIDTH = HEADS * HEAD_V
MIX_WIDTH = HGRN_WIDTH + GLA_WIDTH
GATE_RANK = 16
GATE_NORMALIZER = 16.0
LOG2_E = 1.4426950408889634
FFN_HIDDEN = 2816
NORM_EPS = 1e-6
LANES = 128
N_PAIRS = HEADS

C_HQ, C_HF, C_HI, C_HG = 0, 512, 1024, 1536
C_GQ, C_GK, C_GV, C_GG = 2048, 2304, 2560, 3072
C_GA = 3584
IN_COLS = C_GA + GATE_RANK
W_BLOCK = 1792

KM_H, KM_G = 0, HGRN_WIDTH
KE_H, KE_G = HGRN_WIDTH + GLA_QK_WIDTH, 2 * HGRN_WIDTH + GLA_QK_WIDTH
KN_COLS = KE_G + GLA_WIDTH
DEC_COLS = HGRN_WIDTH + GLA_QK_WIDTH

HEAD_TABLE = tuple(
    [(KM_H + LANES * h, KE_H + LANES * h, LANES * h) for h in range(HEADS)]
    + [(KM_G + LANES * (h // 2), KE_G + LANES * h, HGRN_WIDTH + LANES * (h // 2))
       for h in range(HEADS)])

V7X_VMEM_LIMIT = 58 * 1024 * 1024

ROWS_MIXER = 512
ROWS_FFN = 1024
FFN_BLOCK = 256


def _rmsnorm(x, g):
    return x * lax.rsqrt(jnp.mean(x * x, axis=-1, keepdims=True) + NORM_EPS) * g


def _dot(a, b):
    return jnp.dot(a.astype(BF16), b.astype(BF16), preferred_element_type=F32)


def _dot_nt(a, b):
    return lax.dot_general(a.astype(BF16), b.astype(BF16), (((1,), (1,)), ((), ())),
                           preferred_element_type=F32)


def _dot_tn(a, b):
    return lax.dot_general(a.astype(BF16), b.astype(BF16), (((0,), (0,)), ((), ())),
                           preferred_element_type=F32)


def _sigmoid(x):
    return 0.5 * jnp.tanh(0.5 * x) + 0.5


def _silu(x):
    return x * _sigmoid(x)


def _log_sigmoid(x):
    return jnp.minimum(x, 0.0) - jnp.log1p(jnp.exp(-jnp.abs(x)))


def _iota2(n, axis):
    return lax.broadcasted_iota(jnp.int32, (n, n), axis)


def _cumsum_rows(g, tri2_bf16):
    g_hi = g.astype(BF16)
    g_lo = (g - g_hi.astype(F32)).astype(BF16)
    return jnp.dot(tri2_bf16, jnp.concatenate([g_hi, g_lo], axis=0),
                   preferred_element_type=F32)


def _own_half(x, h):
    lane = lax.broadcasted_iota(jnp.int32, x.shape, 1)
    return jnp.where((lane >= GLA_K) == bool(h % 2), x, 0.0)


def _project(x_ref, g_ref, w_ref, wga_ref, wup_ref, bup_ref, lbt_ref, kn_ref, qt_ref, vt_ref,
             gate_ref, dec_ref, chunk, transposed, between=None):
    n_chunks = x_ref.shape[0] // chunk
    u = _rmsnorm(x_ref[...], g_ref[...]).astype(BF16)
    tri = (_iota2(chunk, 0) >= _iota2(chunk, 1)).astype(BF16)
    tri = jnp.concatenate([tri, tri], axis=1)
    mid = chunk // 2 - 1

    def proj(col, width):
        blk, off = divmod(col, W_BLOCK)
        assert off + width <= W_BLOCK
        return jnp.dot(u, w_ref[blk, :, off:off + width], preferred_element_type=F32)

    def decay_factors(g, c):
        b = _cumsum_rows(g[c * chunk:(c + 1) * chunk], tri)
        b_mid = b[mid:mid + 1, :]
        b_last = b[chunk - 1:, :]
        return (jnp.exp2(b - b_mid), jnp.exp2(b_mid - b), jnp.exp2(b_last - b),
                jnp.exp2(b_mid), jnp.exp2(b_last))

    def put_tile(ref, c, i, tile):
        if transposed:
            ref[c, LANES * i:LANES * (i + 1), :] = tile.T
        else:
            ref[c, :, LANES * i:LANES * (i + 1)] = tile

    t0 = lbt_ref[0:1, :]
    t1 = lbt_ref[1:2, :]
    m = jnp.maximum(t0, t1)
    e0 = jnp.exp(t0 - m)
    lb = e0 / (e0 + jnp.exp(t1 - m))

    def value_piece(col, first_head):
        def emit():
            v = proj(col, 2 * LANES).astype(BF16)
            for c in range(n_chunks):
                for h in range(2):
                    put_tile(vt_ref, c, first_head + h,
                             v[c * chunk:(c + 1) * chunk, LANES * h:LANES * (h + 1)])
        return emit

    def gate_piece(col, out_col):
        def emit():
            gate_ref[:, out_col:out_col + 2 * LANES] = proj(col, 2 * LANES)
        return emit

    light = [value_piece(C_HI, 0), value_piece(C_HI + 2 * LANES, 2),
             value_piece(C_GV, HEADS), value_piece(C_GV + 2 * LANES, HEADS + 2),
             gate_piece(C_HG, 0), gate_piece(C_HG + 2 * LANES, 2 * LANES),
             gate_piece(C_GG, HGRN_WIDTH), gate_piece(C_GG + 2 * LANES, HGRN_WIDTH + 2 * LANES)]

    def pause():
        if between is not None:
            between()

    def next_light():
        if light:
            light.pop(0)()
        pause()

    def ga_part(k):
        cols = slice(k * (D_MODEL // 4), (k + 1) * (D_MODEL // 4))
        return _dot_nt(wga_ref[:, cols], u[:, cols])

    ga_t = ga_part(0)
    hf = proj(C_HF, HGRN_WIDTH)
    ga_t += ga_part(1)
    hq = proj(C_HQ, HGRN_WIDTH)

    f = 0.5 * (1.0 + lb) + (0.5 * (1.0 - lb)) * jnp.tanh(0.5 * hf)
    g = jnp.log2(f)
    k = 1.0 - f
    q = _silu(hq)
    pause()
    ga_t += ga_part(2)
    gq = proj(C_GQ, GLA_QK_WIDTH)
    ga_t += ga_part(3)
    gk = proj(C_GK, GLA_QK_WIDTH)
    pause()
    for c in range(n_chunks):
        rows = slice(c * chunk, (c + 1) * chunk)
        e_q, e_k, e_end, e_mid, e_last = decay_factors(g, c)
        kn_ref[rows, KM_H:KM_H + HGRN_WIDTH] = (k[rows] * e_k).astype(BF16)
        kn_ref[rows, KE_H:KE_H + HGRN_WIDTH] = (k[rows] * e_end).astype(BF16)
        qm = (q[rows] * e_q).astype(BF16)
        for h in range(HEADS):
            put_tile(qt_ref, c, h, qm[:, LANES * h:LANES * (h + 1)])
        dec_ref[c, :, 0:HGRN_WIDTH] = e_mid
        dec_ref[c, :, DEC_COLS:DEC_COLS + HGRN_WIDTH] = e_last
        next_light()

    next_light()
    logits = _dot_tn(ga_t, wup_ref[...]) + bup_ref[...]
    g = _log_sigmoid(logits) * (LOG2_E / GATE_NORMALIZER)
    q = gq * (GLA_K ** -0.5)
    k = gk
    for c in range(n_chunks):
        rows = slice(c * chunk, (c + 1) * chunk)
        e_q, e_k, e_end, e_mid, e_last = decay_factors(g, c)
        qm = q[rows] * e_q
        ke = k[rows] * e_end
        kn_ref[rows, KM_G:KM_G + GLA_QK_WIDTH] = (k[rows] * e_k).astype(BF16)
        for h in range(HEADS):
            pair = slice(LANES * (h // 2), LANES * (h // 2 + 1))
            put_tile(qt_ref, c, HEADS + h, _own_half(qm[:, pair], h).astype(BF16))
            kn_ref[rows, KE_G + LANES * h:KE_G + LANES * (h + 1)] = (
                _own_half(ke[:, pair], h).astype(BF16))
        dec_ref[c, :, HGRN_WIDTH:DEC_COLS] = e_mid
        dec_ref[c, :, DEC_COLS + HGRN_WIDTH:2 * DEC_COLS] = e_last
        next_light()
    while light:
        next_light()


def _meta_kernel(x_ref, g_ref, wt_ref, wgat_ref, wup_ref, bup_ref, lbt_ref,
                 w_ref, wga_ref, kn_ref, v_ref, w_scr, q_ref, gate_ref, dec_ref):
    j = pl.program_id(0)
    n_blocks = w_scr.shape[0]

    @pl.when(j < n_blocks)
    def _():
        blk = wt_ref[...].T.astype(BF16)
        w_ref[0] = blk
        w_scr[j] = blk

    @pl.when(j == n_blocks)
    def _():
        wga_ref[...] = wgat_ref[...].astype(BF16)
        _project(x_ref, g_ref, w_scr, wga_ref, wup_ref, bup_ref, lbt_ref, kn_ref, q_ref, v_ref,
                 gate_ref, dec_ref, N_META, transposed=False)


def _meta(meta_tokens, g, w_in_t, w_up, b_up, lb_table):
    n_blocks = C_GA // W_BLOCK
    full = lambda a: pl.BlockSpec(a.shape, lambda j: (0,) * len(a.shape))
    last = n_blocks - 1
    out_shapes = [jax.ShapeDtypeStruct((n_blocks, D_MODEL, W_BLOCK), BF16),
                  jax.ShapeDtypeStruct((GATE_RANK, D_MODEL), BF16),
                  jax.ShapeDtypeStruct((N_META, KN_COLS), BF16),
                  jax.ShapeDtypeStruct((1, N_META, MIX_WIDTH), BF16)]
    return pl.pallas_call(
        _meta_kernel,
        grid=(n_blocks + 1,),
        in_specs=[
            full(meta_tokens), full(g),
            pl.BlockSpec((W_BLOCK, D_MODEL), lambda j: (jnp.minimum(j, last), 0)),
            pl.BlockSpec((GATE_RANK, D_MODEL), lambda j: (C_GA // GATE_RANK, 0)),
            full(w_up), full(b_up), full(lb_table),
        ],
        out_specs=[pl.BlockSpec((1, D_MODEL, W_BLOCK), lambda j: (jnp.minimum(j, last), 0, 0))]
        + [full(s) for s in out_shapes[1:]],
        out_shape=out_shapes,
        scratch_shapes=[pltpu.VMEM((n_blocks, D_MODEL, W_BLOCK), BF16),
                        pltpu.VMEM((1, N_META, MIX_WIDTH), BF16),
                        pltpu.VMEM((N_META, MIX_WIDTH), F32),
                        pltpu.VMEM((1, 1, 2 * DEC_COLS), F32)],
        compiler_params=pltpu.CompilerParams(
            dimension_semantics=("arbitrary",), vmem_limit_bytes=V7X_VMEM_LIMIT),
        name="meta",
    )(meta_tokens, g, w_in_t, w_in_t, w_up, b_up, lb_table)


def _side_by_side(a, b):
    return jnp.concatenate([a, b], axis=1)


def _block_diag(a, b):
    return jnp.concatenate([_side_by_side(a, jnp.zeros_like(b)),
                            _side_by_side(jnp.zeros_like(a), b)], axis=0)


def _seed_state(mkn_ref, mv_ref, st_ref):
    seed = [_dot_tn(mv_ref[0, :, LANES * i:LANES * (i + 1)], mkn_ref[:, ke_col:ke_col + LANES])
            for i, (_, ke_col, _) in enumerate(HEAD_TABLE)]
    for p in range(N_PAIRS):
        st_ref[p] = _side_by_side(seed[2 * p], seed[2 * p + 1])


def _recurrence_steps(kn_ref, qt_ref, vt_ref, dec_ref, hnorm_ref, gnorm_ref, o_ref, st_ref):
    key_before_query = _iota2(CHUNK, 0) <= _iota2(CHUNK, 1)
    key_before_query = _side_by_side(key_before_query, key_before_query)
    norms = (hnorm_ref[...],) * HEADS + (gnorm_ref[...],) * HEADS

    def pair(fn, p):
        return fn(2 * p), fn(2 * p + 1)

    for c in range(kn_ref.shape[0] // CHUNK):
        rows = slice(c * CHUNK, (c + 1) * CHUNK)
        dec = dec_ref[c]
        km = lambda i: kn_ref[rows, HEAD_TABLE[i][0]:HEAD_TABLE[i][0] + LANES]
        ke = lambda i: kn_ref[rows, HEAD_TABLE[i][1]:HEAD_TABLE[i][1] + LANES]
        e_mid = lambda i: dec[:, HEAD_TABLE[i][2]:HEAD_TABLE[i][2] + LANES]
        e_last = lambda i: dec[:, DEC_COLS + HEAD_TABLE[i][2]:DEC_COLS + HEAD_TABLE[i][2] + LANES]
        qm_t = lambda i: qt_ref[c, LANES * i:LANES * (i + 1), :]
        v_t = lambda i: vt_ref[c, LANES * i:LANES * (i + 1), :]
        first, outs = [], []
        for p in range(N_PAIRS):
            s_mid = st_ref[p] * _side_by_side(*pair(e_mid, p))
            lhs = jnp.concatenate([_side_by_side(*pair(km, p)), s_mid.astype(BF16)], axis=0)
            first.append(jnp.dot(lhs, _block_diag(*pair(qm_t, p)),
                                 preferred_element_type=F32))
        yield
        for p in range(N_PAIRS):
            scores_t = jnp.where(key_before_query, first[p][:CHUNK], 0.0).astype(BF16)
            outs.append(first[p][CHUNK:]
                        + jnp.dot(_side_by_side(*pair(v_t, p)),
                                  _block_diag(scores_t[:, :CHUNK], scores_t[:, CHUNK:]),
                                  preferred_element_type=F32))
        yield
        for p in range(N_PAIRS):
            st_ref[p] = (st_ref[p] * _side_by_side(*pair(e_last, p))
                         + jnp.dot(_side_by_side(*pair(v_t, p)), _block_diag(*pair(ke, p)),
                                   preferred_element_type=F32))
        yield
        for p in range(N_PAIRS):
            o_t = outs[p]
            o_t = o_t * lax.rsqrt(jnp.mean(o_t * o_t, axis=0, keepdims=True) + NORM_EPS)
            o = o_t.T
            for half, i in enumerate((2 * p, 2 * p + 1)):
                o_ref[rows, LANES * i:LANES * (i + 1)] = (
                    o[half * CHUNK:(half + 1) * CHUNK] * norms[i]).astype(o_ref.dtype)
        yield


STAGES_PER_PAUSE = (1, 1, 2, 1, 2, 1, 1, 2, 1, 2, 2)


def _fused_kernel(x_ref, g_ref, w_ref, wga_ref, wup_ref, bup_ref, lbt_ref, wout32_ref, w1_32_ref,
                  w2_32_ref, mkn_ref, mv_ref, hnorm_ref, gnorm_ref,
                  gate_ref, o_ref, wout_ref, w1_ref, w2_ref,
                  kn_scr, qt_scr, vt_scr, dec_scr, st_ref, *, n_tiles, steps_per_seq):
    s = pl.program_id(0)

    @pl.when(lax.rem(s + steps_per_seq - 1, steps_per_seq) == 0)
    def _():
        _seed_state(mkn_ref, mv_ref, st_ref)

    def project(fill, between=None):
        _project(x_ref, g_ref, w_ref, wga_ref, wup_ref, bup_ref, lbt_ref, kn_scr.at[fill],
                 qt_scr.at[fill], vt_scr.at[fill], gate_ref, dec_scr.at[fill], CHUNK,
                 transposed=True, between=between)
        wout_ref[...] = wout32_ref[...].astype(BF16)
        w1_ref[...] = w1_32_ref[...].astype(BF16)
        w2_ref[...] = w2_32_ref[...].astype(BF16)

    def recurrence(drain):
        return _recurrence_steps(kn_scr.at[drain], qt_scr.at[drain], vt_scr.at[drain],
                                 dec_scr.at[drain], hnorm_ref, gnorm_ref, o_ref, st_ref)

    @pl.when(s == 0)
    def _():
        for ref in (kn_scr, qt_scr, vt_scr, dec_scr):
            ref[1] = jnp.zeros(ref.shape[1:], ref.dtype)
        st_ref[...] = jnp.zeros(st_ref.shape, st_ref.dtype)

    for parity in (0, 1):
        @pl.when(lax.rem(s, 2) == parity)
        def _():
            stages = recurrence(1 - parity)
            budget = list(STAGES_PER_PAUSE)

            def between():
                for _ in range(budget.pop(0) if budget else 1):
                    next(stages, None)

            project(parity, between)
            for _ in stages:
                pass


def _fused(x2d, g, w_main, w_ga, w_up, b_up, lb_table, w_out, w1, w2, meta_kn, meta_v, hnorm,
           gnorm, seq, rows):
    n = x2d.shape[0]
    n_tiles = n // rows
    chunks = rows // CHUNK
    last = n_tiles - 1
    wout_rows = w_out.shape[0] // n_tiles
    w1_rows = w1.shape[0] // n_tiles
    w2_rows = 2 * w2.shape[0] // n_tiles
    assert wout_rows * n_tiles == w_out.shape[0] and w1_rows * n_tiles == w1.shape[0]
    assert w2_rows * n_tiles == 2 * w2.shape[0]
    const = lambda s: (0, 0)
    const3 = lambda s: (0, 0, 0)
    cur = lambda s: (jnp.minimum(s, last), 0)
    prev = lambda s: (jnp.maximum(s - 1, 0), 0)
    half_pace = lambda s: (jnp.minimum(s, last) // 2, 0)
    w_specs = [pl.BlockSpec((wout_rows, w_out.shape[1]), cur),
               pl.BlockSpec((w1_rows, w1.shape[1]), cur),
               pl.BlockSpec((w2_rows, w2.shape[1]), half_pace)]
    return pl.pallas_call(
        functools.partial(_fused_kernel, n_tiles=n_tiles, steps_per_seq=seq // rows),
        grid=(n_tiles + 1,),
        in_specs=[
            pl.BlockSpec((rows, D_MODEL), cur),
            pl.BlockSpec((1, D_MODEL), const),
            pl.BlockSpec(w_main.shape, const3, pipeline_mode=pl.Buffered(1)),
            pl.BlockSpec(w_ga.shape, const),
            pl.BlockSpec((GATE_RANK, GLA_QK_WIDTH), const),
            pl.BlockSpec((1, GLA_QK_WIDTH), const),
            pl.BlockSpec(lb_table.shape, const),
        ] + w_specs + [
            pl.BlockSpec(meta_kn.shape, const),
            pl.BlockSpec(meta_v.shape, const3),
            pl.BlockSpec((1, HEAD_V), const),
            pl.BlockSpec((1, HEAD_V), const),
        ],
        out_specs=[
            pl.BlockSpec((rows, MIX_WIDTH), cur),
            pl.BlockSpec((rows, MIX_WIDTH), prev),
        ] + w_specs,
        out_shape=[
            jax.ShapeDtypeStruct((n, MIX_WIDTH), F32),
            jax.ShapeDtypeStruct((n, MIX_WIDTH), BF16),
            jax.ShapeDtypeStruct(w_out.shape, BF16),
            jax.ShapeDtypeStruct(w1.shape, BF16),
            jax.ShapeDtypeStruct(w2.shape, BF16),
        ],
        scratch_shapes=[pltpu.VMEM((2, rows, KN_COLS), BF16),
                        pltpu.VMEM((2, chunks, MIX_WIDTH, CHUNK), BF16),
                        pltpu.VMEM((2, chunks, MIX_WIDTH, CHUNK), BF16),
                        pltpu.VMEM((2, chunks, 1, 2 * DEC_COLS), F32),
                        pltpu.VMEM((N_PAIRS, HEAD_V, 2 * LANES), F32)],
        compiler_params=pltpu.CompilerParams(
            dimension_semantics=("arbitrary",), vmem_limit_bytes=V7X_VMEM_LIMIT),
        name="mixer",
    )(x2d, g, w_main, w_ga, w_up, b_up, lb_table, w_out, w1, w2, meta_kn, meta_v, hnorm, gnorm)


def _ffn_kernel(x_ref, o_ref, gate_ref, wout_ref, g2_ref, w1_ref, w2_ref, g3_ref, y_ref):
    o = o_ref[...].astype(F32) * _silu(gate_ref[...])
    h = x_ref[...] + _dot(o, wout_ref[...])
    u = _rmsnorm(h, g2_ref[...]).astype(BF16)
    acts = []
    for j in range(0, FFN_HIDDEN, FFN_BLOCK):
        gate = jnp.dot(u, w1_ref[:, j:j + FFN_BLOCK], preferred_element_type=F32)
        up = jnp.dot(u, w1_ref[:, FFN_HIDDEN + j:FFN_HIDDEN + j + FFN_BLOCK],
                     preferred_element_type=F32)
        acts.append((_silu(gate) * up).astype(BF16))
    ffn = jnp.dot(jnp.concatenate(acts, axis=1), w2_ref[...], preferred_element_type=F32)
    y_ref[...] = _rmsnorm(h + ffn, g3_ref[...])


def _ffn(x2d, o, gates, w_out, g2, w1, w2, g3, rows):
    n = x2d.shape[0]
    const = lambda i: (0, 0)
    tile = lambda i: (i, 0)
    resident = functools.partial(pl.BlockSpec, index_map=const, pipeline_mode=pl.Buffered(1))
    return pl.pallas_call(
        _ffn_kernel,
        grid=(n // rows,),
        in_specs=[
            pl.BlockSpec((rows, D_MODEL), tile),
            pl.BlockSpec((rows, MIX_WIDTH), tile),
            pl.BlockSpec((rows, MIX_WIDTH), tile),
            resident((MIX_WIDTH, D_MODEL)),
            pl.BlockSpec((1, D_MODEL), const),
            resident((D_MODEL, 2 * FFN_HIDDEN)),
            resident((FFN_HIDDEN, D_MODEL)),
            pl.BlockSpec((1, D_MODEL), const),
        ],
        out_specs=pl.BlockSpec((rows, D_MODEL), tile),
        out_shape=jax.ShapeDtypeStruct((n, D_MODEL), F32),
        compiler_params=pltpu.CompilerParams(
            dimension_semantics=("arbitrary",), vmem_limit_bytes=V7X_VMEM_LIMIT),
        name="outproj_ffn",
    )(x2d, o, gates, w_out, g2, w1, w2, g3)


def kernel(x, meta_tokens, lb_table, norm_mix_g, w_in, w_gla_gate_up, b_gla_gate, hgrn_norm_g,
           gla_norm_g, w_out, norm_ffn_g, w_ffn_in, w_ffn_out, norm_final_g):
    batch, seq, _ = x.shape
    assert w_in.shape[0] == 1 and lb_table.shape[0] == 2, "single-layer block"
    assert seq % ROWS_MIXER == 0
    x2d = x.reshape(batch * seq, D_MODEL)
    g_mix = norm_mix_g[0][None, :]
    w_up = w_gla_gate_up[0]
    b_up = b_gla_gate[0][None, :]

    w_main, w_ga, meta_kn, meta_v = _meta(meta_tokens, g_mix, w_in[0].T, w_up, b_up, lb_table)
    gates, o, w_out_b, w1_b, w2_b = _fused(
        x2d, g_mix, w_main, w_ga, w_up, b_up, lb_table, w_out[0], w_ffn_in[0], w_ffn_out[0],
        meta_kn, meta_v, hgrn_norm_g[0][None, :], gla_norm_g[0][None, :], seq, ROWS_MIXER)
    y = _ffn(x2d, o, gates, w_out_b, norm_ffn_g[0][None, :], w1_b, w2_b, norm_final_g[None, :],
             ROWS_FFN)
    return y.reshape(batch, seq, D_MODEL)
```

```python
import functools

import jax
import jax.numpy as jnp
from jax import lax
from jax.experimental import pallas as pl
from jax.experimental.pallas import tpu as pltpu

F32 = jnp.float32
BF16 = jnp.bfloat16

D_MODEL = 1024
N_META = 16
CHUNK = 128
HEADS = 4
HEAD_V = 128
HGRN_WIDTH = HEADS * HEAD_V
GLA_K = 64
GLA_QK_WIDTH = HEADS * GLA_K
GLA_WIDTH = HEADS * HEAD_V
MIX_WIDTH = HGRN_WIDTH + GLA_WIDTH
GATE_RANK = 16
GATE_NORMALIZER = 16.0
LOG2_E = 1.4426950408889634
FFN_HIDDEN = 2816
NORM_EPS = 1e-6
LANES = 128
N_PAIRS = HEADS

C_HQ, C_HF, C_HI, C_HG = 0, 512, 1024, 1536
C_GQ, C_GK, C_GV, C_GG = 2048, 2304, 2560, 3072
C_GA = 3584
IN_COLS = C_GA + GATE_RANK
W_BLOCK = 1792

KM_H, KM_G = 0, HGRN_WIDTH
KE_H, KE_G = HGRN_WIDTH + GLA_QK_WIDTH, 2 * HGRN_WIDTH + GLA_QK_WIDTH
KN_COLS = KE_G + GLA_WIDTH
DEC_COLS = HGRN_WIDTH + GLA_QK_WIDTH

HEAD_TABLE = tuple(
    [(KM_H + LANES * h, KE_H + LANES * h, LANES * h) for h in range(HEADS)]
    + [(KM_G + LANES * (h // 2), KE_G + LANES * h, HGRN_WIDTH + LANES * (h // 2))
       for h in range(HEADS)])

V7X_VMEM_LIMIT = 58 * 1024 * 1024

ROWS_MIXER = 512
ROWS_FFN = 1024
FFN_BLOCK = 256


def _rmsnorm(x, g):
    return x * lax.rsqrt(jnp.mean(x * x, axis=-1, keepdims=True) + NORM_EPS) * g


def _dot(a, b):
    return jnp.dot(a.astype(BF16), b.astype(BF16), preferred_element_type=F32)


def _dot_nt(a, b):
    return lax.dot_general(a.astype(BF16), b.astype(BF16), (((1,), (1,)), ((), ())),
                           preferred_element_type=F32)


def _dot_tn(a, b):
    return lax.dot_general(a.astype(BF16), b.astype(BF16), (((0,), (0,)), ((), ())),
                           preferred_element_type=F32)


def _sigmoid(x):
    return 0.5 * jnp.tanh(0.5 * x) + 0.5


def _silu(x):
    return x * _sigmoid(x)


def _log_sigmoid(x):
    return jnp.minimum(x, 0.0) - jnp.log1p(jnp.exp(-jnp.abs(x)))


def _iota2(n, axis):
    return lax.broadcasted_iota(jnp.int32, (n, n), axis)


def _cumsum_rows(g, tri2_bf16):
    g_hi = g.astype(BF16)
    g_lo = (g - g_hi.astype(F32)).astype(BF16)
    return jnp.dot(tri2_bf16, jnp.concatenate([g_hi, g_lo], axis=0),
                   preferred_element_type=F32)


def _own_half(x, h):
    lane = lax.broadcasted_iota(jnp.int32, x.shape, 1)
    return jnp.where((lane >= GLA_K) == bool(h % 2), x, 0.0)


def _project(x_ref, g_ref, w_ref, wga_ref, wup_ref, bup_ref, lbt_ref, kn_ref, qt_ref, vt_ref,
             gate_ref, dec_ref, chunk, transposed, between=None):
    n_chunks = x_ref.shape[0] // chunk
    u = _rmsnorm(x_ref[...], g_ref[...]).astype(BF16)
    tri = (_iota2(chunk, 0) >= _iota2(chunk, 1)).astype(BF16)
    tri = jnp.concatenate([tri, tri], axis=1)
    mid = chunk // 2 - 1

    def proj(col, width):
        blk, off = divmod(col, W_BLOCK)
        assert off + width <= W_BLOCK
        return jnp.dot(u, w_ref[blk, :, off:off + width], preferred_element_type=F32)

    def decay_factors(g, c):
        b = _cumsum_rows(g[c * chunk:(c + 1) * chunk], tri)
        b_mid = b[mid:mid + 1, :]
        b_last = b[chunk - 1:, :]
        return (jnp.exp2(b - b_mid), jnp.exp2(b_mid - b), jnp.exp2(b_last - b),
                jnp.exp2(b_mid), jnp.exp2(b_last))

    def put_tile(ref, c, i, tile):
        if transposed:
            ref[c, LANES * i:LANES * (i + 1), :] = tile.T
        else:
            ref[c, :, LANES * i:LANES * (i + 1)] = tile

    t0 = lbt_ref[0:1, :]
    t1 = lbt_ref[1:2, :]
    m = jnp.maximum(t0, t1)
    e0 = jnp.exp(t0 - m)
    lb = e0 / (e0 + jnp.exp(t1 - m))

    def value_piece(col, first_head):
        def emit():
            v = proj(col, 2 * LANES).astype(BF16)
            for c in range(n_chunks):
                for h in range(2):
                    put_tile(vt_ref, c, first_head + h,
                             v[c * chunk:(c + 1) * chunk, LANES * h:LANES * (h + 1)])
        return emit

    def gate_piece(col, out_col):
        def emit():
            gate_ref[:, out_col:out_col + 2 * LANES] = proj(col, 2 * LANES)
        return emit

    light = [value_piece(C_HI, 0), value_piece(C_HI + 2 * LANES, 2),
             value_piece(C_GV, HEADS), value_piece(C_GV + 2 * LANES, HEADS + 2),
             gate_piece(C_HG, 0), gate_piece(C_HG + 2 * LANES, 2 * LANES),
             gate_piece(C_GG, HGRN_WIDTH), gate_piece(C_GG + 2 * LANES, HGRN_WIDTH + 2 * LANES)]

    def pause():
        if between is not None:
            between()

    def next_light():
        if light:
            light.pop(0)()
        pause()

    def ga_part(k):
        cols = slice(k * (D_MODEL // 4), (k + 1) * (D_MODEL // 4))
        return _dot_nt(wga_ref[:, cols], u[:, cols])

    ga_t = ga_part(0)
    hf = proj(C_HF, HGRN_WIDTH)
    ga_t += ga_part(1)
    hq = proj(C_HQ, HGRN_WIDTH)

    f = 0.5 * (1.0 + lb) + (0.5 * (1.0 - lb)) * jnp.tanh(0.5 * hf)
    g = jnp.log2(f)
    k = 1.0 - f
    q = _silu(hq)
    pause()
    ga_t += ga_part(2)
    gq = proj(C_GQ, GLA_QK_WIDTH)
    ga_t += ga_part(3)
    gk = proj(C_GK, GLA_QK_WIDTH)
    pause()
    for c in range(n_chunks):
        rows = slice(c * chunk, (c + 1) * chunk)
        e_q, e_k, e_end, e_mid, e_last = decay_factors(g, c)
        kn_ref[rows, KM_H:KM_H + HGRN_WIDTH] = (k[rows] * e_k).astype(BF16)
        kn_ref[rows, KE_H:KE_H + HGRN_WIDTH] = (k[rows] * e_end).astype(BF16)
        qm = (q[rows] * e_q).astype(BF16)
        for h in range(HEADS):
            put_tile(qt_ref, c, h, qm[:, LANES * h:LANES * (h + 1)])
        dec_ref[c, :, 0:HGRN_WIDTH] = e_mid
        dec_ref[c, :, DEC_COLS:DEC_COLS + HGRN_WIDTH] = e_last
        next_light()

    next_light()
    logits = _dot_tn(ga_t, wup_ref[...]) + bup_ref[...]
    g = _log_sigmoid(logits) * (LOG2_E / GATE_NORMALIZER)
    q = gq * (GLA_K ** -0.5)
    k = gk
    for c in range(n_chunks):
        rows = slice(c * chunk, (c + 1) * chunk)
        e_q, e_k, e_end, e_mid, e_last = decay_factors(g, c)
        qm = q[rows] * e_q
        ke = k[rows] * e_end
        kn_ref[rows, KM_G:KM_G + GLA_QK_WIDTH] = (k[rows] * e_k).astype(BF16)
        for h in range(HEADS):
            pair = slice(LANES * (h // 2), LANES * (h // 2 + 1))
            put_tile(qt_ref, c, HEADS + h, _own_half(qm[:, pair], h).astype(BF16))
            kn_ref[rows, KE_G + LANES * h:KE_G + LANES * (h + 1)] = (
                _own_half(ke[:, pair], h).astype(BF16))
        dec_ref[c, :, HGRN_WIDTH:DEC_COLS] = e_mid
        dec_ref[c, :, DEC_COLS + HGRN_WIDTH:2 * DEC_COLS] = e_last
        next_light()
    while light:
        next_light()


def _meta_kernel(x_ref, g_ref, wt_ref, wgat_ref, wup_ref, bup_ref, lbt_ref,
                 w_ref, wga_ref, kn_ref, v_ref, w_scr, q_ref, gate_ref, dec_ref):
    j = pl.program_id(0)
    n_blocks = w_scr.shape[0]

    @pl.when(j < n_blocks)
    def _():
        blk = wt_ref[...].T.astype(BF16)
        w_ref[0] = blk
        w_scr[j] = blk

    @pl.when(j == n_blocks)
    def _():
        wga_ref[...] = wgat_ref[...].astype(BF16)
        _project(x_ref, g_ref, w_scr, wga_ref, wup_ref, bup_ref, lbt_ref, kn_ref, q_ref, v_ref,
                 gate_ref, dec_ref, N_META, transposed=False)


def _meta(meta_tokens, g, w_in_t, w_up, b_up, lb_table):
    n_blocks = C_GA // W_BLOCK
    full = lambda a: pl.BlockSpec(a.shape, lambda j: (0,) * len(a.shape))
    last = n_blocks - 1
    out_shapes = [jax.ShapeDtypeStruct((n_blocks, D_MODEL, W_BLOCK), BF16),
                  jax.ShapeDtypeStruct((GATE_RANK, D_MODEL), BF16),
                  jax.ShapeDtypeStruct((N_META, KN_COLS), BF16),
                  jax.ShapeDtypeStruct((1, N_META, MIX_WIDTH), BF16)]
    return pl.pallas_call(
        _meta_kernel,
        grid=(n_blocks + 1,),
        in_specs=[
            full(meta_tokens), full(g),
            pl.BlockSpec((W_BLOCK, D_MODEL), lambda j: (jnp.minimum(j, last), 0)),
            pl.BlockSpec((GATE_RANK, D_MODEL), lambda j: (C_GA // GATE_RANK, 0)),
            full(w_up), full(b_up), full(lb_table),
        ],
        out_specs=[pl.BlockSpec((1, D_MODEL, W_BLOCK), lambda j: (jnp.minimum(j, last), 0, 0))]
        + [full(s) for s in out_shapes[1:]],
        out_shape=out_shapes,
        scratch_shapes=[pltpu.VMEM((n_blocks, D_MODEL, W_BLOCK), BF16),
                        pltpu.VMEM((1, N_META, MIX_WIDTH), BF16),
                        pltpu.VMEM((N_META, MIX_WIDTH), F32),
                        pltpu.VMEM((1, 1, 2 * DEC_COLS), F32)],
        compiler_params=pltpu.CompilerParams(
            dimension_semantics=("arbitrary",), vmem_limit_bytes=V7X_VMEM_LIMIT),
        name="meta",
    )(meta_tokens, g, w_in_t, w_in_t, w_up, b_up, lb_table)


def _side_by_side(a, b):
    return jnp.concatenate([a, b], axis=1)


def _block_diag(a, b):
    return jnp.concatenate([_side_by_side(a, jnp.zeros_like(b)),
                            _side_by_side(jnp.zeros_like(a), b)], axis=0)


def _seed_state(mkn_ref, mv_ref, st_ref):
    seed = [_dot_tn(mv_ref[0, :, LANES * i:LANES * (i + 1)], mkn_ref[:, ke_col:ke_col + LANES])
            for i, (_, ke_col, _) in enumerate(HEAD_TABLE)]
    for p in range(N_PAIRS):
        st_ref[p] = _side_by_side(seed[2 * p], seed[2 * p + 1])


def _recurrence_steps(kn_ref, qt_ref, vt_ref, dec_ref, hnorm_ref, gnorm_ref, o_ref, st_ref):
    key_before_query = _iota2(CHUNK, 0) <= _iota2(CHUNK, 1)
    key_before_query = _side_by_side(key_before_query, key_before_query)
    norms = (hnorm_ref[...],) * HEADS + (gnorm_ref[...],) * HEADS

    def pair(fn, p):
        return fn(2 * p), fn(2 * p + 1)

    for c in range(kn_ref.shape[0] // CHUNK):
        rows = slice(c * CHUNK, (c + 1) * CHUNK)
        dec = dec_ref[c]
        km = lambda i: kn_ref[rows, HEAD_TABLE[i][0]:HEAD_TABLE[i][0] + LANES]
        ke = lambda i: kn_ref[rows, HEAD_TABLE[i][1]:HEAD_TABLE[i][1] + LANES]
        e_mid = lambda i: dec[:, HEAD_TABLE[i][2]:HEAD_TABLE[i][2] + LANES]
        e_last = lambda i: dec[:, DEC_COLS + HEAD_TABLE[i][2]:DEC_COLS + HEAD_TABLE[i][2] + LANES]
        qm_t = lambda i: qt_ref[c, LANES * i:LANES * (i + 1), :]
        v_t = lambda i: vt_ref[c, LANES * i:LANES * (i + 1), :]
        first, outs = [], []
        for p in range(N_PAIRS):
            s_mid = st_ref[p] * _side_by_side(*pair(e_mid, p))
            lhs = jnp.concatenate([_side_by_side(*pair(km, p)), s_mid.astype(BF16)], axis=0)
            first.append(jnp.dot(lhs, _block_diag(*pair(qm_t, p)),
                                 preferred_element_type=F32))
        yield
        for p in range(N_PAIRS):
            scores_t = jnp.where(key_before_query, first[p][:CHUNK], 0.0).astype(BF16)
            outs.append(first[p][CHUNK:]
                        + jnp.dot(_side_by_side(*pair(v_t, p)),
                                  _block_diag(scores_t[:, :CHUNK], scores_t[:, CHUNK:]),
                                  preferred_element_type=F32))
        yield
        for p in range(N_PAIRS):
            st_ref[p] = (st_ref[p] * _side_by_side(*pair(e_last, p))
                         + jnp.dot(_side_by_side(*pair(v_t, p)), _block_diag(*pair(ke, p)),
                                   preferred_element_type=F32))
        yield
        for p in range(N_PAIRS):
            o_t = outs[p]
            o_t = o_t * lax.rsqrt(jnp.mean(o_t * o_t, axis=0, keepdims=True) + NORM_EPS)
            o = o_t.T
            for half, i in enumerate((2 * p, 2 * p + 1)):
                o_ref[rows, LANES * i:LANES * (i + 1)] = (
                    o[half * CHUNK:(half + 1) * CHUNK] * norms[i]).astype(o_ref.dtype)
        yield


STAGES_PER_PAUSE = (1, 1, 2, 1, 2, 1, 1, 2, 1, 2, 2)


def _fused_kernel(x_ref, g_ref, w_ref, wga_ref, wup_ref, bup_ref, lbt_ref, wout32_ref, w1_32_ref,
                  w2_32_ref, mkn_ref, mv_ref, hnorm_ref, gnorm_ref,
                  gate_ref, o_ref, wout_ref, w1_ref, w2_ref,
                  kn_scr, qt_scr, vt_scr, dec_scr, st_ref, *, n_tiles, steps_per_seq):
    s = pl.program_id(0)

    @pl.when(lax.rem(s + steps_per_seq - 1, steps_per_seq) == 0)
    def _():
        _seed_state(mkn_ref, mv_ref, st_ref)

    def project(fill, between=None):
        _project(x_ref, g_ref, w_ref, wga_ref, wup_ref, bup_ref, lbt_ref, kn_scr.at[fill],
                 qt_scr.at[fill], vt_scr.at[fill], gate_ref, dec_scr.at[fill], CHUNK,
                 transposed=True, between=between)
        wout_ref[...] = wout32_ref[...].astype(BF16)
        w1_ref[...] = w1_32_ref[...].astype(BF16)
        w2_ref[...] = w2_32_ref[...].astype(BF16)

    def recurrence(drain):
        return _recurrence_steps(kn_scr.at[drain], qt_scr.at[drain], vt_scr.at[drain],
                                 dec_scr.at[drain], hnorm_ref, gnorm_ref, o_ref, st_ref)

    @pl.when(s == 0)
    def _():
        project(0)

    for parity in (0, 1):
        @pl.when((s > 0) & (s < n_tiles) & (lax.rem(s, 2) == parity))
        def _():
            stages = recurrence(1 - parity)
            budget = list(STAGES_PER_PAUSE)

            def between():
                for _ in range(budget.pop(0) if budget else 1):
                    next(stages, None)

            project(parity, between)
            for _ in stages:
                pass

    @pl.when(s == n_tiles)
    def _():
        for _ in recurrence((n_tiles - 1) % 2):
            pass


def _fused(x2d, g, w_main, w_ga, w_up, b_up, lb_table, w_out, w1, w2, meta_kn, meta_v, hnorm,
           gnorm, seq, rows):
    n = x2d.shape[0]
    n_tiles = n // rows
    chunks = rows // CHUNK
    last = n_tiles - 1
    wout_rows = w_out.shape[0] // n_tiles
    w1_rows = w1.shape[0] // n_tiles
    w2_rows = 2 * w2.shape[0] // n_tiles
    assert wout_rows * n_tiles == w_out.shape[0] and w1_rows * n_tiles == w1.shape[0]
    assert w2_rows * n_tiles == 2 * w2.shape[0]
    const = lambda s: (0, 0)
    const3 = lambda s: (0, 0, 0)
    cur = lambda s: (jnp.minimum(s, last), 0)
    prev = lambda s: (jnp.maximum(s - 1, 0), 0)
    half_pace = lambda s: (jnp.minimum(s, last) // 2, 0)
    w_specs = [pl.BlockSpec((wout_rows, w_out.shape[1]), cur),
               pl.BlockSpec((w1_rows, w1.shape[1]), cur),
               pl.BlockSpec((w2_rows, w2.shape[1]), half_pace)]
    streamed_in = [pl.BlockSpec((rows, D_MODEL), cur)] + w_specs
    streamed_out = [pl.BlockSpec((rows, MIX_WIDTH), cur),
                    pl.BlockSpec((rows, MIX_WIDTH), prev)] + w_specs

    def outer(x_hbm, g_ref, w_ref, wga_ref, wup_ref, bup_ref, lbt_ref, wout32_hbm, w1_32_hbm,
              w2_32_hbm, mkn_ref, mv_ref, hnorm_ref, gnorm_ref,
              gate_hbm, o_hbm, wout_hbm, w1_hbm, w2_hbm, *scratch):
        def step(x_ref, wout32_ref, w1_32_ref, w2_32_ref, gate_ref, o_ref, wout_ref, w1_ref,
                 w2_ref):
            _fused_kernel(x_ref, g_ref, w_ref, wga_ref, wup_ref, bup_ref, lbt_ref, wout32_ref,
                          w1_32_ref, w2_32_ref, mkn_ref, mv_ref, hnorm_ref, gnorm_ref,
                          gate_ref, o_ref, wout_ref, w1_ref, w2_ref, *scratch,
                          n_tiles=n_tiles, steps_per_seq=seq // rows)

        pltpu.emit_pipeline(step, grid=(n_tiles + 1,), in_specs=streamed_in,
                            out_specs=streamed_out)(
            x_hbm, wout32_hbm, w1_32_hbm, w2_32_hbm, gate_hbm, o_hbm, wout_hbm, w1_hbm, w2_hbm)

    in_vmem = pl.BlockSpec(memory_space=pltpu.VMEM)
    in_hbm = pl.BlockSpec(memory_space=pl.ANY)
    return pl.pallas_call(
        outer,
        in_specs=[in_hbm] + [in_vmem] * 6 + [in_hbm] * 3 + [in_vmem] * 4,
        out_specs=[in_hbm] * 5,
        out_shape=[
            jax.ShapeDtypeStruct((n, MIX_WIDTH), F32),
            jax.ShapeDtypeStruct((n, MIX_WIDTH), BF16),
            jax.ShapeDtypeStruct(w_out.shape, BF16),
            jax.ShapeDtypeStruct(w1.shape, BF16),
            jax.ShapeDtypeStruct(w2.shape, BF16),
        ],
        scratch_shapes=[pltpu.VMEM((2, rows, KN_COLS), BF16),
                        pltpu.VMEM((2, chunks, MIX_WIDTH, CHUNK), BF16),
                        pltpu.VMEM((2, chunks, MIX_WIDTH, CHUNK), BF16),
                        pltpu.VMEM((2, chunks, 1, 2 * DEC_COLS), F32),
                        pltpu.VMEM((N_PAIRS, HEAD_V, 2 * LANES), F32)],
        compiler_params=pltpu.CompilerParams(vmem_limit_bytes=V7X_VMEM_LIMIT),
        name="mixer",
    )(x2d, g, w_main, w_ga, w_up, b_up, lb_table, w_out, w1, w2, meta_kn, meta_v, hnorm, gnorm)


def _ffn_kernel(x_ref, o_ref, gate_ref, wout_ref, g2_ref, w1_ref, w2_ref, g3_ref, y_ref):
    o = o_ref[...].astype(F32) * _silu(gate_ref[...])
    h = x_ref[...] + _dot(o, wout_ref[...])
    u = _rmsnorm(h, g2_ref[...]).astype(BF16)
    acts = []
    for j in range(0, FFN_HIDDEN, FFN_BLOCK):
        gate = jnp.dot(u, w1_ref[:, j:j + FFN_BLOCK], preferred_element_type=F32)
        up = jnp.dot(u, w1_ref[:, FFN_HIDDEN + j:FFN_HIDDEN + j + FFN_BLOCK],
                     preferred_element_type=F32)
        acts.append((_silu(gate) * up).astype(BF16))
    ffn = jnp.dot(jnp.concatenate(acts, axis=1), w2_ref[...], preferred_element_type=F32)
    y_ref[...] = _rmsnorm(h + ffn, g3_ref[...])


def _ffn(x2d, o, gates, w_out, g2, w1, w2, g3, rows):
    n = x2d.shape[0]
    const = lambda i: (0, 0)
    tile = lambda i: (i, 0)
    resident = functools.partial(pl.BlockSpec, index_map=const, pipeline_mode=pl.Buffered(1))
    return pl.pallas_call(
        _ffn_kernel,
        grid=(n // rows,),
        in_specs=[
            pl.BlockSpec((rows, D_MODEL), tile),
            pl.BlockSpec((rows, MIX_WIDTH), tile),
            pl.BlockSpec((rows, MIX_WIDTH), tile),
            resident((MIX_WIDTH, D_MODEL)),
            pl.BlockSpec((1, D_MODEL), const),
            resident((D_MODEL, 2 * FFN_HIDDEN)),
            resident((FFN_HIDDEN, D_MODEL)),
            pl.BlockSpec((1, D_MODEL), const),
        ],
        out_specs=pl.BlockSpec((rows, D_MODEL), tile),
        out_shape=jax.ShapeDtypeStruct((n, D_MODEL), F32),
        compiler_params=pltpu.CompilerParams(
            dimension_semantics=("arbitrary",), vmem_limit_bytes=V7X_VMEM_LIMIT),
        name="outproj_ffn",
    )(x2d, o, gates, w_out, g2, w1, w2, g3)


def kernel(x, meta_tokens, lb_table, norm_mix_g, w_in, w_gla_gate_up, b_gla_gate, hgrn_norm_g,
           gla_norm_g, w_out, norm_ffn_g, w_ffn_in, w_ffn_out, norm_final_g):
    batch, seq, _ = x.shape
    assert w_in.shape[0] == 1 and lb_table.shape[0] == 2, "single-layer block"
    assert seq % ROWS_MIXER == 0
    x2d = x.reshape(batch * seq, D_MODEL)
    g_mix = norm_mix_g[0][None, :]
    w_up = w_gla_gate_up[0]
    b_up = b_gla_gate[0][None, :]

    w_main, w_ga, meta_kn, meta_v = _meta(meta_tokens, g_mix, w_in[0].T, w_up, b_up, lb_table)
    gates, o, w_out_b, w1_b, w2_b = _fused(
        x2d, g_mix, w_main, w_ga, w_up, b_up, lb_table, w_out[0], w_ffn_in[0], w_ffn_out[0],
        meta_kn, meta_v, hgrn_norm_g[0][None, :], gla_norm_g[0][None, :], seq, ROWS_MIXER)
    y = _ffn(x2d, o, gates, w_out_b, norm_ffn_g[0][None, :], w1_b, w2_b, norm_final_g[None, :],
             ROWS_FFN)
    return y.reshape(batch, seq, D_MODEL)
```

```python
import functools

import jax
import jax.numpy as jnp
from jax import lax
from jax.experimental import pallas as pl
from jax.experimental.pallas import tpu as pltpu

F32 = jnp.float32
BF16 = jnp.bfloat16

D_MODEL = 1024
N_META = 16
CHUNK = 128
HEADS = 4
HEAD_V = 128
HGRN_WIDTH = HEADS * HEAD_V
GLA_K = 64
GLA_QK_WIDTH = HEADS * GLA_K
GLA_WIDTH = HEADS * HEAD_V
MIX_WIDTH = HGRN_WIDTH + GLA_WIDTH
GATE_RANK = 16
GATE_NORMALIZER = 16.0
LOG2_E = 1.4426950408889634
FFN_HIDDEN = 2816
NORM_EPS = 1e-6
LANES = 128
N_PAIRS = HEADS

C_HQ, C_HF, C_HI, C_HG = 0, 512, 1024, 1536
C_GQ, C_GK, C_GV, C_GG = 2048, 2304, 2560, 3072
C_GA = 3584
IN_COLS = C_GA + GATE_RANK
W_BLOCK = 1792

KM_H, KM_G = 0, HGRN_WIDTH
KE_H, KE_G = HGRN_WIDTH + GLA_QK_WIDTH, 2 * HGRN_WIDTH + GLA_QK_WIDTH
KN_COLS = KE_G + GLA_WIDTH
DEC_COLS = HGRN_WIDTH + GLA_QK_WIDTH

HEAD_TABLE = tuple(
    [(KM_H + LANES * h, KE_H + LANES * h, LANES * h) for h in range(HEADS)]
    + [(KM_G + LANES * (h // 2), KE_G + LANES * h, HGRN_WIDTH + LANES * (h // 2))
       for h in range(HEADS)])

V7X_VMEM_LIMIT = 58 * 1024 * 1024

ROWS_MIXER = 512
ROWS_FFN = 1024
FFN_BLOCK = 256


def _rmsnorm(x, g):
    return x * lax.rsqrt(jnp.mean(x * x, axis=-1, keepdims=True) + NORM_EPS) * g


def _dot(a, b):
    return jnp.dot(a.astype(BF16), b.astype(BF16), preferred_element_type=F32)


def _dot_nt(a, b):
    return lax.dot_general(a.astype(BF16), b.astype(BF16), (((1,), (1,)), ((), ())),
                           preferred_element_type=F32)


def _dot_tn(a, b):
    return lax.dot_general(a.astype(BF16), b.astype(BF16), (((0,), (0,)), ((), ())),
                           preferred_element_type=F32)


def _sigmoid(x):
    return 0.5 * jnp.tanh(0.5 * x) + 0.5


def _silu(x):
    return x * _sigmoid(x)


def _log_sigmoid(x):
    return jnp.minimum(x, 0.0) - jnp.log1p(jnp.exp(-jnp.abs(x)))


def _iota2(n, axis):
    return lax.broadcasted_iota(jnp.int32, (n, n), axis)


def _cumsum_rows(g, tri2_bf16):
    g_hi = g.astype(BF16)
    g_lo = (g - g_hi.astype(F32)).astype(BF16)
    return jnp.dot(tri2_bf16, jnp.concatenate([g_hi, g_lo], axis=0),
                   preferred_element_type=F32)


def _own_half(x, h):
    lane = lax.broadcasted_iota(jnp.int32, x.shape, 1)
    return jnp.where((lane >= GLA_K) == bool(h % 2), x, 0.0)


def _project(x_ref, g_ref, w_ref, wga_ref, wup_ref, bup_ref, lbt_ref, kn_ref, qt_ref, vt_ref,
             gate_ref, dec_ref, chunk, transposed, between=None):
    n_chunks = x_ref.shape[0] // chunk
    u = _rmsnorm(x_ref[...], g_ref[...]).astype(BF16)
    tri = (_iota2(chunk, 0) >= _iota2(chunk, 1)).astype(BF16)
    tri = jnp.concatenate([tri, tri], axis=1)
    mid = chunk // 2 - 1

    def proj(col, width):
        blk, off = divmod(col, W_BLOCK)
        assert off + width <= W_BLOCK
        return jnp.dot(u, w_ref[blk, :, off:off + width], preferred_element_type=F32)

    def decay_factors(g, c):
        b = _cumsum_rows(g[c * chunk:(c + 1) * chunk], tri)
        b_mid = b[mid:mid + 1, :]
        b_last = b[chunk - 1:, :]
        return (jnp.exp2(b - b_mid), jnp.exp2(b_mid - b), jnp.exp2(b_last - b),
                jnp.exp2(b_mid), jnp.exp2(b_last))

    def put_tile(ref, c, i, tile):
        if transposed:
            ref[c, LANES * i:LANES * (i + 1), :] = tile.T
        else:
            ref[c, :, LANES * i:LANES * (i + 1)] = tile

    t0 = lbt_ref[0:1, :]
    t1 = lbt_ref[1:2, :]
    m = jnp.maximum(t0, t1)
    e0 = jnp.exp(t0 - m)
    lb = e0 / (e0 + jnp.exp(t1 - m))

    def value_piece(col, first_head):
        def emit():
            v = proj(col, 2 * LANES).astype(BF16)
            for c in range(n_chunks):
                for h in range(2):
                    put_tile(vt_ref, c, first_head + h,
                             v[c * chunk:(c + 1) * chunk, LANES * h:LANES * (h + 1)])
        return emit

    def gate_piece(col, out_col):
        def emit():
            gate_ref[:, out_col:out_col + 2 * LANES] = proj(col, 2 * LANES)
        return emit

    light = [value_piece(C_HI, 0), value_piece(C_HI + 2 * LANES, 2),
             value_piece(C_GV, HEADS), value_piece(C_GV + 2 * LANES, HEADS + 2),
             gate_piece(C_HG, 0), gate_piece(C_HG + 2 * LANES, 2 * LANES),
             gate_piece(C_GG, HGRN_WIDTH), gate_piece(C_GG + 2 * LANES, HGRN_WIDTH + 2 * LANES)]

    def pause():
        if between is not None:
            between()

    def next_light():
        if light:
            light.pop(0)()
        pause()

    def ga_part(k):
        cols = slice(k * (D_MODEL // 4), (k + 1) * (D_MODEL // 4))
        return _dot_nt(wga_ref[:, cols], u[:, cols])

    ga_t = ga_part(0)
    hf = proj(C_HF, HGRN_WIDTH)
    ga_t += ga_part(1)
    hq = proj(C_HQ, HGRN_WIDTH)

    f = 0.5 * (1.0 + lb) + (0.5 * (1.0 - lb)) * jnp.tanh(0.5 * hf)
    g = jnp.log2(f)
    k = 1.0 - f
    q = _silu(hq)
    pause()
    ga_t += ga_part(2)
    gq = proj(C_GQ, GLA_QK_WIDTH)
    ga_t += ga_part(3)
    gk = proj(C_GK, GLA_QK_WIDTH)
    pause()
    for c in range(n_chunks):
        rows = slice(c * chunk, (c + 1) * chunk)
        e_q, e_k, e_end, e_mid, e_last = decay_factors(g, c)
        kn_ref[rows, KM_H:KM_H + HGRN_WIDTH] = (k[rows] * e_k).astype(BF16)
        kn_ref[rows, KE_H:KE_H + HGRN_WIDTH] = (k[rows] * e_end).astype(BF16)
        qm = (q[rows] * e_q).astype(BF16)
        for h in range(HEADS):
            put_tile(qt_ref, c, h, qm[:, LANES * h:LANES * (h + 1)])
        dec_ref[c, :, 0:HGRN_WIDTH] = e_mid
        dec_ref[c, :, DEC_COLS:DEC_COLS + HGRN_WIDTH] = e_last
        next_light()

    next_light()
    logits = _dot_tn(ga_t, wup_ref[...]) + bup_ref[...]
    g = _log_sigmoid(logits) * (LOG2_E / GATE_NORMALIZER)
    q = gq * (GLA_K ** -0.5)
    k = gk
    for c in range(n_chunks):
        rows = slice(c * chunk, (c + 1) * chunk)
        e_q, e_k, e_end, e_mid, e_last = decay_factors(g, c)
        qm = q[rows] * e_q
        ke = k[rows] * e_end
        kn_ref[rows, KM_G:KM_G + GLA_QK_WIDTH] = (k[rows] * e_k).astype(BF16)
        for h in range(HEADS):
            pair = slice(LANES * (h // 2), LANES * (h // 2 + 1))
            put_tile(qt_ref, c, HEADS + h, _own_half(qm[:, pair], h).astype(BF16))
            kn_ref[rows, KE_G + LANES * h:KE_G + LANES * (h + 1)] = (
                _own_half(ke[:, pair], h).astype(BF16))
        dec_ref[c, :, HGRN_WIDTH:DEC_COLS] = e_mid
        dec_ref[c, :, DEC_COLS + HGRN_WIDTH:2 * DEC_COLS] = e_last
        next_light()
    while light:
        next_light()


def _meta_kernel(x_ref, g_ref, wt_ref, wgat_ref, wup_ref, bup_ref, lbt_ref,
                 w_ref, wga_ref, kn_ref, v_ref, w_scr, q_ref, gate_ref, dec_ref):
    j = pl.program_id(0)
    n_blocks = w_scr.shape[0]

    @pl.when(j < n_blocks)
    def _():
        blk = wt_ref[...].T.astype(BF16)
        w_ref[0] = blk
        w_scr[j] = blk

    @pl.when(j == n_blocks)
    def _():
        wga_ref[...] = wgat_ref[...].astype(BF16)
        _project(x_ref, g_ref, w_scr, wga_ref, wup_ref, bup_ref, lbt_ref, kn_ref, q_ref, v_ref,
                 gate_ref, dec_ref, N_META, transposed=False)


def _meta(meta_tokens, g, w_in_t, w_up, b_up, lb_table):
    n_blocks = C_GA // W_BLOCK
    full = lambda a: pl.BlockSpec(a.shape, lambda j: (0,) * len(a.shape))
    last = n_blocks - 1
    out_shapes = [jax.ShapeDtypeStruct((n_blocks, D_MODEL, W_BLOCK), BF16),
                  jax.ShapeDtypeStruct((GATE_RANK, D_MODEL), BF16),
                  jax.ShapeDtypeStruct((N_META, KN_COLS), BF16),
                  jax.ShapeDtypeStruct((1, N_META, MIX_WIDTH), BF16)]
    return pl.pallas_call(
        _meta_kernel,
        grid=(n_blocks + 1,),
        in_specs=[
            full(meta_tokens), full(g),
            pl.BlockSpec((W_BLOCK, D_MODEL), lambda j: (jnp.minimum(j, last), 0)),
            pl.BlockSpec((GATE_RANK, D_MODEL), lambda j: (C_GA // GATE_RANK, 0)),
            full(w_up), full(b_up), full(lb_table),
        ],
        out_specs=[pl.BlockSpec((1, D_MODEL, W_BLOCK), lambda j: (jnp.minimum(j, last), 0, 0))]
        + [full(s) for s in out_shapes[1:]],
        out_shape=out_shapes,
        scratch_shapes=[pltpu.VMEM((n_blocks, D_MODEL, W_BLOCK), BF16),
                        pltpu.VMEM((1, N_META, MIX_WIDTH), BF16),
                        pltpu.VMEM((N_META, MIX_WIDTH), F32),
                        pltpu.VMEM((1, 1, 2 * DEC_COLS), F32)],
        compiler_params=pltpu.CompilerParams(
            dimension_semantics=("arbitrary",), vmem_limit_bytes=V7X_VMEM_LIMIT),
        name="meta",
    )(meta_tokens, g, w_in_t, w_in_t, w_up, b_up, lb_table)


def _side_by_side(a, b):
    return jnp.concatenate([a, b], axis=1)


def _block_diag(a, b):
    return jnp.concatenate([_side_by_side(a, jnp.zeros_like(b)),
                            _side_by_side(jnp.zeros_like(a), b)], axis=0)


def _seed_state(mkn_ref, mv_ref, st_ref):
    seed = [_dot_tn(mv_ref[0, :, LANES * i:LANES * (i + 1)], mkn_ref[:, ke_col:ke_col + LANES])
            for i, (_, ke_col, _) in enumerate(HEAD_TABLE)]
    for p in range(N_PAIRS):
        st_ref[p] = _side_by_side(seed[2 * p], seed[2 * p + 1])


def _recurrence_steps(kn_ref, qt_ref, vt_ref, dec_ref, hnorm_ref, gnorm_ref, o_ref, st_ref):
    key_before_query = _iota2(CHUNK, 0) <= _iota2(CHUNK, 1)
    key_before_query = _side_by_side(key_before_query, key_before_query)
    norms = (hnorm_ref[...],) * HEADS + (gnorm_ref[...],) * HEADS

    def pair(fn, p):
        return fn(2 * p), fn(2 * p + 1)

    for c in range(kn_ref.shape[0] // CHUNK):
        rows = slice(c * CHUNK, (c + 1) * CHUNK)
        dec = dec_ref[c]
        km = lambda i: kn_ref[rows, HEAD_TABLE[i][0]:HEAD_TABLE[i][0] + LANES]
        ke = lambda i: kn_ref[rows, HEAD_TABLE[i][1]:HEAD_TABLE[i][1] + LANES]
        e_mid = lambda i: dec[:, HEAD_TABLE[i][2]:HEAD_TABLE[i][2] + LANES]
        e_last = lambda i: dec[:, DEC_COLS + HEAD_TABLE[i][2]:DEC_COLS + HEAD_TABLE[i][2] + LANES]
        qm_t = lambda i: qt_ref[c, LANES * i:LANES * (i + 1), :]
        v_t = lambda i: vt_ref[c, LANES * i:LANES * (i + 1), :]
        first, outs = [], []
        for p in range(N_PAIRS):
            s_mid = st_ref[p] * _side_by_side(*pair(e_mid, p))
            lhs = jnp.concatenate([_side_by_side(*pair(km, p)), s_mid.astype(BF16)], axis=0)
            first.append(jnp.dot(lhs, _block_diag(*pair(qm_t, p)),
                                 preferred_element_type=F32))
        yield
        for p in range(N_PAIRS):
            scores_t = jnp.where(key_before_query, first[p][:CHUNK], 0.0).astype(BF16)
            outs.append(first[p][CHUNK:]
                        + jnp.dot(_side_by_side(*pair(v_t, p)),
                                  _block_diag(scores_t[:, :CHUNK], scores_t[:, CHUNK:]),
                                  preferred_element_type=F32))
        yield
        for p in range(N_PAIRS):
            st_ref[p] = (st_ref[p] * _side_by_side(*pair(e_last, p))
                         + jnp.dot(_side_by_side(*pair(v_t, p)), _block_diag(*pair(ke, p)),
                                   preferred_element_type=F32))
        yield
        for p in range(N_PAIRS):
            o_t = outs[p]
            o_t = o_t * lax.rsqrt(jnp.mean(o_t * o_t, axis=0, keepdims=True) + NORM_EPS)
            o = o_t.T
            for half, i in enumerate((2 * p, 2 * p + 1)):
                o_ref[rows, LANES * i:LANES * (i + 1)] = (
                    o[half * CHUNK:(half + 1) * CHUNK] * norms[i]).astype(o_ref.dtype)
        yield


STAGES_PER_PAUSE = (1, 1, 2, 1, 2, 1, 1, 2, 1, 2, 2)


def _fused_kernel(x_ref, g_ref, w_ref, wga_ref, wup_ref, bup_ref, lbt_ref, wout32_ref, w1_32_ref,
                  w2_32_ref, mkn_ref, mv_ref, hnorm_ref, gnorm_ref,
                  gate_ref, o_ref, wout_ref, w1_ref, w2_ref,
                  kn_scr, qt_scr, vt_scr, dec_scr, st_ref, *, n_tiles, steps_per_seq):
    s = pl.program_id(0)

    @pl.when(lax.rem(s + steps_per_seq - 1, steps_per_seq) == 0)
    def _():
        _seed_state(mkn_ref, mv_ref, st_ref)

    def project(fill, between=None):
        _project(x_ref, g_ref, w_ref, wga_ref, wup_ref, bup_ref, lbt_ref, kn_scr.at[fill],
                 qt_scr.at[fill], vt_scr.at[fill], gate_ref, dec_scr.at[fill], CHUNK,
                 transposed=True, between=between)
        wout_ref[...] = wout32_ref[...].astype(BF16)
        w1_ref[...] = w1_32_ref[...].astype(BF16)
        w2_ref[...] = w2_32_ref[...].astype(BF16)

    def recurrence(drain):
        return _recurrence_steps(kn_scr.at[drain], qt_scr.at[drain], vt_scr.at[drain],
                                 dec_scr.at[drain], hnorm_ref, gnorm_ref, o_ref, st_ref)

    @pl.when(s == 0)
    def _():
        project(0)

    for parity in (0, 1):
        @pl.when((s > 0) & (s < n_tiles) & (lax.rem(s, 2) == parity))
        def _():
            stages = recurrence(1 - parity)
            budget = list(STAGES_PER_PAUSE)

            def between():
                for _ in range(budget.pop(0) if budget else 1):
                    next(stages, None)

            project(parity, between)
            for _ in stages:
                pass

    @pl.when(s == n_tiles)
    def _():
        for _ in recurrence((n_tiles - 1) % 2):
            pass


def _fused(x2d, g, w_main, w_ga, w_up, b_up, lb_table, w_out, w1, w2, meta_kn, meta_v, hnorm,
           gnorm, seq, rows):
    n = x2d.shape[0]
    n_tiles = n // rows
    chunks = rows // CHUNK
    last = n_tiles - 1
    wout_rows = w_out.shape[0] // n_tiles
    w1_rows = w1.shape[0] // n_tiles
    w2_rows = 2 * w2.shape[0] // n_tiles
    assert wout_rows * n_tiles == w_out.shape[0] and w1_rows * n_tiles == w1.shape[0]
    assert w2_rows * n_tiles == 2 * w2.shape[0]
    const = lambda s: (0, 0)
    const3 = lambda s: (0, 0, 0)
    cur = lambda s: (jnp.minimum(s, last), 0)
    prev = lambda s: (jnp.maximum(s - 1, 0), 0)
    half_pace = lambda s: (jnp.minimum(s, last) // 2, 0)
    w_specs = [pl.BlockSpec((wout_rows, w_out.shape[1]), cur),
               pl.BlockSpec((w1_rows, w1.shape[1]), cur),
               pl.BlockSpec((w2_rows, w2.shape[1]), half_pace)]
    streamed_in = [pl.BlockSpec((rows, D_MODEL), cur)] + w_specs
    streamed_out = [pl.BlockSpec((rows, MIX_WIDTH), cur),
                    pl.BlockSpec((rows, MIX_WIDTH), prev)] + w_specs

    def outer(x_hbm, g_ref, w_ref, wga_ref, wup_ref, bup_ref, lbt_ref, wout32_hbm, w1_32_hbm,
              w2_32_hbm, mkn_ref, mv_ref, hnorm_ref, gnorm_ref,
              gate_hbm, o_hbm, wout_hbm, w1_hbm, w2_hbm, *scratch):
        def step(x_ref, wout32_ref, w1_32_ref, w2_32_ref, gate_ref, o_ref, wout_ref, w1_ref,
                 w2_ref):
            _fused_kernel(x_ref, g_ref, w_ref, wga_ref, wup_ref, bup_ref, lbt_ref, wout32_ref,
                          w1_32_ref, w2_32_ref, mkn_ref, mv_ref, hnorm_ref, gnorm_ref,
                          gate_ref, o_ref, wout_ref, w1_ref, w2_ref, *scratch,
                          n_tiles=n_tiles, steps_per_seq=seq // rows)

        pltpu.emit_pipeline(step, grid=(n_tiles + 1,), in_specs=streamed_in,
                            out_specs=streamed_out)(
            x_hbm, wout32_hbm, w1_32_hbm, w2_32_hbm, gate_hbm, o_hbm, wout_hbm, w1_hbm, w2_hbm)

    in_vmem = pl.BlockSpec(memory_space=pltpu.VMEM)
    in_hbm = pl.BlockSpec(memory_space=pl.ANY)
    return pl.pallas_call(
        outer,
        in_specs=[in_hbm] + [in_vmem] * 6 + [in_hbm] * 3 + [in_vmem] * 4,
        out_specs=[in_hbm] * 5,
        out_shape=[
            jax.ShapeDtypeStruct((n, MIX_WIDTH), F32),
            jax.ShapeDtypeStruct((n, MIX_WIDTH), BF16),
            jax.ShapeDtypeStruct(w_out.shape, BF16),
            jax.ShapeDtypeStruct(w1.shape, BF16),
            jax.ShapeDtypeStruct(w2.shape, BF16),
        ],
        scratch_shapes=[pltpu.VMEM((2, rows, KN_COLS), BF16),
                        pltpu.VMEM((2, chunks, MIX_WIDTH, CHUNK), BF16),
                        pltpu.VMEM((2, chunks, MIX_WIDTH, CHUNK), BF16),
                        pltpu.VMEM((2, chunks, 1, 2 * DEC_COLS), F32),
                        pltpu.VMEM((N_PAIRS, HEAD_V, 2 * LANES), F32)],
        compiler_params=pltpu.CompilerParams(vmem_limit_bytes=V7X_VMEM_LIMIT),
        name="mixer",
    )(x2d, g, w_main, w_ga, w_up, b_up, lb_table, w_out, w1, w2, meta_kn, meta_v, hnorm, gnorm)


def _ffn_kernel(x_ref, o_ref, gate_ref, wout_ref, g2_ref, w1_ref, w2_ref, g3_ref, y_ref):
    o = o_ref[...].astype(F32) * _silu(gate_ref[...])
    h = x_ref[...] + _dot(o, wout_ref[...])
    u = _rmsnorm(h, g2_ref[...]).astype(BF16)
    acts = []
    for j in range(0, FFN_HIDDEN, FFN_BLOCK):
        gate = jnp.dot(u, w1_ref[:, j:j + FFN_BLOCK], preferred_element_type=F32)
        up = jnp.dot(u, w1_ref[:, FFN_HIDDEN + j:FFN_HIDDEN + j + FFN_BLOCK],
                     preferred_element_type=F32)
        acts.append((_silu(gate) * up).astype(BF16))
    ffn = jnp.dot(jnp.concatenate(acts, axis=1), w2_ref[...], preferred_element_type=F32)
    y_ref[...] = _rmsnorm(h + ffn, g3_ref[...])


def _ffn(x2d, o, gates, w_out, g2, w1, w2, g3, rows):
    n = x2d.shape[0]
    const = lambda i: (0, 0)
    tile = lambda i: (i, 0)
    def outer(x_hbm, o_hbm, gate_hbm, wout_ref, g2_ref, w1_ref, w2_ref, g3_ref, y_hbm):
        def step(x_ref, o_ref, gate_ref, y_ref):
            _ffn_kernel(x_ref, o_ref, gate_ref, wout_ref, g2_ref, w1_ref, w2_ref, g3_ref, y_ref)

        pltpu.emit_pipeline(
            step, grid=(n // rows,),
            in_specs=[pl.BlockSpec((rows, D_MODEL), tile), pl.BlockSpec((rows, MIX_WIDTH), tile),
                      pl.BlockSpec((rows, MIX_WIDTH), tile)],
            out_specs=[pl.BlockSpec((rows, D_MODEL), tile)])(x_hbm, o_hbm, gate_hbm, y_hbm)

    in_vmem = pl.BlockSpec(memory_space=pltpu.VMEM)
    in_hbm = pl.BlockSpec(memory_space=pl.ANY)
    return pl.pallas_call(
        outer,
        in_specs=[in_hbm] * 3 + [in_vmem] * 5,
        out_specs=in_hbm,
        out_shape=jax.ShapeDtypeStruct((n, D_MODEL), F32),
        compiler_params=pltpu.CompilerParams(vmem_limit_bytes=V7X_VMEM_LIMIT),
        name="outproj_ffn",
    )(x2d, o, gates, w_out, g2, w1, w2, g3)


def kernel(x, meta_tokens, lb_table, norm_mix_g, w_in, w_gla_gate_up, b_gla_gate, hgrn_norm_g,
           gla_norm_g, w_out, norm_ffn_g, w_ffn_in, w_ffn_out, norm_final_g):
    batch, seq, _ = x.shape
    assert w_in.shape[0] == 1 and lb_table.shape[0] == 2, "single-layer block"
    assert seq % ROWS_MIXER == 0
    x2d = x.reshape(batch * seq, D_MODEL)
    g_mix = norm_mix_g[0][None, :]
    w_up = w_gla_gate_up[0]
    b_up = b_gla_gate[0][None, :]

    w_main, w_ga, meta_kn, meta_v = _meta(meta_tokens, g_mix, w_in[0].T, w_up, b_up, lb_table)
    gates, o, w_out_b, w1_b, w2_b = _fused(
        x2d, g_mix, w_main, w_ga, w_up, b_up, lb_table, w_out[0], w_ffn_in[0], w_ffn_out[0],
        meta_kn, meta_v, hgrn_norm_g[0][None, :], gla_norm_g[0][None, :], seq, ROWS_MIXER)
    y = _ffn(x2d, o, gates, w_out_b, norm_ffn_g[0][None, :], w1_b, w2_b, norm_final_g[None, :],
             ROWS_FFN)
    return y.reshape(batch, seq, D_MODEL)
```
